```python
import math, functools
import jax
import jax.numpy as jnp
from jax import lax
import numpy as np

D_MODEL = 2048
BATCH = 1
SEQ = 8192
DEPTH = 2
DEC_BATCH = 32
DEC_SEQ = 1
PAST_LEN = 8192
PAGE_SIZE = 128

D_MIX = D_MODEL
SSM_DIM = D_MIX // 2
SSM_HEAD_DIM = 64
SSM_HEADS = SSM_DIM // SSM_HEAD_DIM
SSM_GROUPS = 4
SSM_STATE = 128
CONV_K = 4
CONV_CH = SSM_DIM + 2 * SSM_GROUPS * SSM_STATE
SSD_CHUNK = 128
NSA_DIM = D_MIX - SSM_DIM
HEAD_DIM = 64
ATT_HEADS = NSA_DIM // HEAD_DIM
KV_HEADS = 4
GQA = ATT_HEADS // KV_HEADS
CMP_STRIDE = 16
L_CMP = 2 * CMP_STRIDE
L_SEL = 64
N_SEL = 16
WINDOW = 512
Q_BLOCK = 128
KV_COLS = 2 * KV_HEADS * HEAD_DIM
IN_COLS = SSM_DIM + CONV_CH + SSM_HEADS + NSA_DIM + 3 * KV_COLS + 3 * ATT_HEADS
D_FF = -(-(8 * D_MODEL) // (3 * 256)) * 256
RMS_EPS = 1e-6
SEL_FORCE = 1e9

kernel_name = 'hymba_ssd_nsa_decode_step'


def rms_norm(x, g):
    xf = x.astype(jnp.float32)
    y = xf * lax.rsqrt(jnp.mean(xf * xf, axis=-1, keepdims=True) + RMS_EPS)
    return (y * g.astype(jnp.float32)).astype(x.dtype)


def masked_softmax(s, mask):
    s = jnp.where(mask, s, -jnp.inf)
    m = jnp.max(s, axis=-1, keepdims=True)
    m = jnp.where(jnp.isfinite(m), m, 0.0)
    e = jnp.where(mask, jnp.exp(s - m), 0.0)
    return e / jnp.maximum(jnp.sum(e, axis=-1, keepdims=True), 1e-30)


def split_proj(u):
    sizes = (SSM_DIM, CONV_CH, SSM_HEADS, NSA_DIM, KV_COLS, KV_COLS, KV_COLS, 3 * ATT_HEADS)
    return jnp.split(u, np.cumsum(sizes)[:-1].tolist(), axis=-1)


def norm_k(kv, g):
    return jnp.stack([rms_norm(kv[:, :, 0], g), kv[:, :, 1]], axis=2)


def ssd_scan(x, dt, a, bm, cm, h0):
    B, T, H, P = x.shape
    G, N = bm.shape[2:]
    Hg = H // G
    Q = SSD_CHUNK if T % SSD_CHUNK == 0 else T
    nc = T // Q
    xc = x.astype(jnp.float32).reshape(B, nc, Q, G, Hg, P)
    dtc = dt.reshape(B, nc, Q, G, Hg)
    bc = bm.astype(jnp.float32).reshape(B, nc, Q, G, N)
    cc = cm.astype(jnp.float32).reshape(B, nc, Q, G, N)
    acum = jnp.cumsum(dtc * a.reshape(G, Hg), axis=2)
    causal = jnp.tril(jnp.ones((Q, Q), dtype=bool))[:, :, None, None]
    seg = acum[:, :, :, None] - acum[:, :, None, :]
    decay = jnp.where(causal, jnp.exp(jnp.where(causal, seg, 0.0)), 0.0)
    cb = jnp.einsum('bcign,bcjgn->bcijg', cc, bc)
    y_diag = jnp.einsum('bcijg,bcijgh,bcjgh,bcjghp->bcighp', cb, decay, dtc, xc)
    decay_end = jnp.exp(acum[:, :, -1:] - acum)
    s_chunk = jnp.einsum('bcjgn,bcjgh,bcjghp->bcghpn', bc, dtc * decay_end, xc)
    chunk_decay = jnp.exp(acum[:, :, -1])

    def step(h, inp):
        s_c, d_c = inp
        return h * d_c[..., None, None] + s_c, h

    h_last, h_in = lax.scan(step, h0.astype(jnp.float32).reshape(B, G, Hg, P, N),
                            (jnp.moveaxis(s_chunk, 1, 0), jnp.moveaxis(chunk_decay, 1, 0)))
    h_in = jnp.moveaxis(h_in, 0, 1)
    y_off = jnp.einsum('bcign,bcghpn,bcigh->bcighp', cc, h_in, jnp.exp(acum))
    y = (y_diag + y_off).reshape(B, T, H, P)
    return y, h_last.reshape(B, H, P, N)


def ssd_mixer(z, xbc, dt_raw, conv_buf, h0, lp):
    B, T = z.shape[:2]
    xbc_full = jnp.concatenate([conv_buf.astype(xbc.dtype), xbc], axis=1)
    new_buf = xbc_full[:, -(CONV_K - 1):]
    conv = lax.conv_general_dilated(xbc_full, lp['conv_w'][:, None, :], window_strides=(1,),
                                    padding='VALID', dimension_numbers=('NWC', 'WIO', 'NWC'),
                                    feature_group_count=CONV_CH)
    xbc_act = jax.nn.silu(conv + lp['conv_b'])
    xs, bm, cm = jnp.split(xbc_act, [SSM_DIM, SSM_DIM + SSM_GROUPS * SSM_STATE], axis=-1)
    dt = jax.nn.softplus(dt_raw.astype(jnp.float32) + lp['dt_bias'].astype(jnp.float32))
    a = -jnp.exp(lp['a_log'].astype(jnp.float32))
    xh = xs.reshape(B, T, SSM_HEADS, SSM_HEAD_DIM)
    y, h = ssd_scan(xh, dt, a, bm.reshape(B, T, SSM_GROUPS, SSM_STATE),
                    cm.reshape(B, T, SSM_GROUPS, SSM_STATE), h0)
    y = y + lp['d_skip'].astype(jnp.float32)[:, None] * xh.astype(jnp.float32)
    y = y.reshape(B, T, SSM_DIM) * jax.nn.silu(z.astype(jnp.float32))
    y = rms_norm(y, lp['ssm_norm_g'])
    return y.astype(z.dtype), new_buf, h


def prepare_nsa_keys(kv_cmp, kv_slc, lp):
    B, T = kv_cmp.shape[:2]
    t_pad = -(-T // L_SEL) * L_SEL
    pad = ((0, 0), (0, t_pad - T), (0, 0), (0, 0), (0, 0))
    kv_cmp = jnp.pad(kv_cmp, pad)
    kv_slc = jnp.pad(kv_slc, pad)
    ch = kv_cmp.reshape(B, t_pad // CMP_STRIDE, CMP_STRIDE, 2, KV_HEADS, HEAD_DIM)
    blocks = jnp.concatenate([ch[:, :-1], ch[:, 1:]], axis=2)
    comp = jnp.einsum('bnlkhd,klde->bnkhe', blocks + lp['cmp_pe'][:, :, None, :], lp['cmp_w'])
    ck = rms_norm(comp[:, :, 0], lp['k_norm_g'][0])
    cv = comp[:, :, 1]
    ns = t_pad // L_SEL
    kb = kv_slc[:, :, 0].reshape(B, ns, L_SEL, KV_HEADS, HEAD_DIM).transpose(0, 3, 1, 2, 4)
    vb = kv_slc[:, :, 1].reshape(B, ns, L_SEL, KV_HEADS, HEAD_DIM).transpose(0, 3, 1, 2, 4)
    nc = ck.shape[1]
    i = jnp.arange(nc)[:, None] * CMP_STRIDE
    j = jnp.arange(ns)[None, :] * L_SEL
    cover = ((i < j + L_SEL) & (i + L_CMP > j)).astype(jnp.float32)
    return ck, cv, kb, vb, cover


def nsa_attend(q, q_pos, ck, cv, kb, vb, wk, wv, w_pos, gates, cover):
    B, Tq = q.shape[:2]
    nc, ns = ck.shape[1], kb.shape[2]
    scale = HEAD_DIM ** -0.5
    s_c = jnp.einsum('bthgd,bnhd->bhgtn', q, ck, preferred_element_type=jnp.float32) * scale
    m_c = (jnp.arange(nc) * CMP_STRIDE + L_CMP - 1)[None, :] <= q_pos[:, None]
    p_c = masked_softmax(s_c, m_c)
    o_c = jnp.einsum('bhgtn,bnhd->bthgd', p_c.astype(cv.dtype), cv)
    imp = jnp.einsum('bhgtn,ns->bhts', p_c, cover)
    blk = jnp.arange(ns)[None, :]
    valid = blk * L_SEL <= q_pos[:, None]
    cur = (q_pos // L_SEL)[:, None]
    forced = (blk == 0) | (blk == cur) | (blk == cur - 1)
    score = jnp.where(valid & forced, SEL_FORCE, jnp.where(valid, imp, -1.0))
    n_sel = min(N_SEL, ns)
    _, idx = lax.top_k(score, n_sel)
    bi = jnp.arange(B)[:, None, None, None]
    hi = jnp.arange(KV_HEADS)[None, :, None, None]
    k_sel = kb[bi, hi, idx].reshape(B, KV_HEADS, Tq, n_sel * L_SEL, HEAD_DIM)
    v_sel = vb[bi, hi, idx].reshape(B, KV_HEADS, Tq, n_sel * L_SEL, HEAD_DIM)
    kpos = (idx[..., None] * L_SEL + jnp.arange(L_SEL)).reshape(B, KV_HEADS, Tq, n_sel * L_SEL)
    m_s = (kpos <= q_pos[:, None])[:, :, None]
    s_s = jnp.einsum('bthgd,bhtkd->bhgtk', q, k_sel, preferred_element_type=jnp.float32) * scale
    p_s = masked_softmax(s_s, m_s)
    o_s = jnp.einsum('bhgtk,bhtkd->bthgd', p_s.astype(v_sel.dtype), v_sel)
    s_w = jnp.einsum('bthgd,bshd->bhgts', q, wk, preferred_element_type=jnp.float32) * scale
    m_w = ((w_pos[None, :] <= q_pos[:, None]) & (w_pos[None, :] >= q_pos[:, None] - WINDOW)
           & (w_pos[None, :] >= 0))
    p_w = masked_softmax(s_w, m_w)
    o_w = jnp.einsum('bhgts,bshd->bthgd', p_w.astype(wv.dtype), wv)
    return gates[..., 0, None] * o_c + gates[..., 1, None] * o_s + gates[..., 2, None] * o_w


def nsa_prompt(q, kvc, kvs, kvw, gates, lp):
    B, T = q.shape[:2]
    ck, cv, kb, vb, cover = prepare_nsa_keys(kvc, kvs, lp)
    win_pad = jnp.pad(kvw, ((0, 0), (WINDOW, 0), (0, 0), (0, 0), (0, 0)))

    def one_block(i):
        q0 = i * Q_BLOCK
        qb = lax.dynamic_slice_in_dim(q, q0, Q_BLOCK, axis=1)
        gb = lax.dynamic_slice_in_dim(gates, q0, Q_BLOCK, axis=1)
        wb = lax.dynamic_slice_in_dim(win_pad, q0, WINDOW + Q_BLOCK, axis=1)
        w_pos = q0 - WINDOW + jnp.arange(WINDOW + Q_BLOCK)
        q_pos = q0 + jnp.arange(Q_BLOCK)
        return nsa_attend(qb, q_pos, ck, cv, kb, vb, wb[:, :, 0], wb[:, :, 1], w_pos, gb, cover)

    o = lax.map(one_block, jnp.arange(T // Q_BLOCK))
    o = jnp.moveaxis(o, 0, 1).reshape(B, T, KV_HEADS, GQA, HEAD_DIM)
    return o, kvw[:, -min(WINDOW, T):]


def nsa_sample(q, kvc, kvs, kvw, gates, lp, cmp_pages, slc_pages, win_buf, page_table):
    B, T = q.shape[:2]
    past = page_table.shape[1] * cmp_pages.shape[1]

    def gather(pages):
        return pages[page_table].reshape(B, past, 2, KV_HEADS, HEAD_DIM)

    full_c = jnp.concatenate([gather(cmp_pages).astype(kvc.dtype), kvc], axis=1)
    full_s = jnp.concatenate([gather(slc_pages).astype(kvs.dtype), kvs], axis=1)
    ck, cv, kb, vb, cover = prepare_nsa_keys(full_c, full_s, lp)
    w_buf = win_buf.shape[1]
    w_all = jnp.concatenate([win_buf.astype(kvw.dtype), kvw], axis=1)
    w_pos = past - w_buf + jnp.arange(w_buf + T)
    q_pos = past + jnp.arange(T)
    o = nsa_attend(q, q_pos, ck, cv, kb, vb, w_all[:, :, 0], w_all[:, :, 1], w_pos, gates, cover)
    return o, w_all[:, -w_buf:]


def trunk_layer(x, conv_buf, h0, nsa_fn, lp):
    B, T, _ = x.shape
    u = rms_norm(x, lp['norm1_g']) @ lp['w_in']
    z, xbc, dt_raw, q, kvc, kvs, kvw, gt = split_proj(u)
    y_ssd, conv_new, h_new = ssd_mixer(z, xbc, dt_raw, conv_buf, h0, lp)
    q = rms_norm(q.reshape(B, T, ATT_HEADS, HEAD_DIM), lp['q_norm_g']).reshape(B, T, KV_HEADS, GQA, HEAD_DIM)
    kvc = kvc.reshape(B, T, 2, KV_HEADS, HEAD_DIM)
    kvs = norm_k(kvs.reshape(B, T, 2, KV_HEADS, HEAD_DIM), lp['k_norm_g'][1])
    kvw = norm_k(kvw.reshape(B, T, 2, KV_HEADS, HEAD_DIM), lp['k_norm_g'][2])
    gates = jax.nn.sigmoid(gt).reshape(B, T, KV_HEADS, GQA, 3)
    o_att, win_new = nsa_fn(q, kvc, kvs, kvw, gates, lp)
    mix = jnp.concatenate([y_ssd, o_att.reshape(B, T, NSA_DIM).astype(y_ssd.dtype)], axis=-1) @ lp['w_out']
    h = x + mix
    g, v = jnp.split(rms_norm(h, lp['norm2_g']) @ lp['w_gu'], 2, axis=-1)
    out = h + (jax.nn.silu(g) * v) @ lp['w_down']
    return out, (kvc, kvs, win_new, h_new, conv_new)


def setup_inputs(seed: int = 0) -> dict:
    key = jax.random.key(seed)
    ks = jax.random.split(key, 32)
    f32 = jnp.float32
    n_pages = PAST_LEN // PAGE_SIZE
    n_used = DEC_BATCH * n_pages
    n_pool = n_used + n_used // 4
    w_buf = min(WINDOW, PAST_LEN)

    def nrm(k, shape, s=1.0):
        return jax.random.normal(k, shape, f32) * s

    dt0 = jnp.exp(jax.random.uniform(ks[12], (DEPTH, SSM_HEADS), f32, math.log(1e-3), math.log(1e-1)))
    return {
        'x_prompt': nrm(ks[0], (BATCH, SEQ, D_MODEL)),
        'x_sample': nrm(ks[1], (DEC_BATCH, DEC_SEQ, D_MODEL)),
        'cache_cmp_kv': nrm(ks[2], (DEPTH, n_pool, PAGE_SIZE, 2, KV_HEADS, HEAD_DIM)),
        'cache_slc_kv': nrm(ks[3], (DEPTH, n_pool, PAGE_SIZE, 2, KV_HEADS, HEAD_DIM)),
        'state_win_kv': nrm(ks[4], (DEPTH, DEC_BATCH, w_buf, 2, KV_HEADS, HEAD_DIM)),
        'state_ssm': nrm(ks[5], (DEPTH, DEC_BATCH, SSM_HEADS, SSM_HEAD_DIM, SSM_STATE), 0.5),
        'state_conv': nrm(ks[6], (DEPTH, DEC_BATCH, CONV_K - 1, CONV_CH)),
        'page_table': jax.random.permutation(ks[7], n_pool)[:n_used].reshape(DEC_BATCH, n_pages).astype(jnp.int32),
        'norm1_g': 1.0 + nrm(ks[8], (DEPTH, D_MODEL), 0.02),
        'w_in': nrm(ks[9], (DEPTH, D_MODEL, IN_COLS), D_MODEL ** -0.5),
        'conv_w': nrm(ks[10], (DEPTH, CONV_K, CONV_CH), CONV_K ** -0.5),
        'conv_b': nrm(ks[11], (DEPTH, CONV_CH), 0.02),
        'dt_bias': dt0 + jnp.log(-jnp.expm1(-dt0)),
        'a_log': jnp.log(jax.random.uniform(ks[13], (DEPTH, SSM_HEADS), f32, 1.0, 16.0)),
        'd_skip': 1.0 + nrm(ks[14], (DEPTH, SSM_HEADS), 0.02),
        'ssm_norm_g': 1.0 + nrm(ks[15], (DEPTH, SSM_DIM), 0.02),
        'q_norm_g': 1.0 + nrm(ks[16], (DEPTH, HEAD_DIM), 0.02),
        'k_norm_g': 1.0 + nrm(ks[17], (DEPTH, 3, HEAD_DIM), 0.02),
        'cmp_pe': nrm(ks[18], (DEPTH, L_CMP, 2, HEAD_DIM), 0.02),
        'cmp_w': nrm(ks[19], (DEPTH, 2, L_CMP, HEAD_DIM, HEAD_DIM), (L_CMP * HEAD_DIM) ** -0.5),
        'w_out': nrm(ks[20], (DEPTH, D_MIX, D_MODEL), D_MIX ** -0.5),
        'norm2_g': 1.0 + nrm(ks[21], (DEPTH, D_MODEL), 0.02),
        'w_gu': nrm(ks[22], (DEPTH, D_MODEL, 2 * D_FF), D_MODEL ** -0.5),
        'w_down': nrm(ks[23], (DEPTH, D_FF, D_MODEL), D_FF ** -0.5),
    }


def reference(x_prompt, x_sample, cache_cmp_kv, cache_slc_kv, state_win_kv, state_ssm, state_conv,
              page_table, norm1_g, w_in, conv_w, conv_b, dt_bias, a_log, d_skip, ssm_norm_g,
              q_norm_g, k_norm_g, cmp_pe, cmp_w, w_out, norm2_g, w_gu, w_down):
    xp, xs = x_prompt, x_sample
    bp = xp.shape[0]
    st_p, st_s = [], []
    for l in range(DEPTH):
        lp = {'norm1_g': norm1_g[l], 'w_in': w_in[l], 'conv_w': conv_w[l], 'conv_b': conv_b[l],
              'dt_bias': dt_bias[l], 'a_log': a_log[l], 'd_skip': d_skip[l], 'ssm_norm_g': ssm_norm_g[l],
              'q_norm_g': q_norm_g[l], 'k_norm_g': k_norm_g[l], 'cmp_pe': cmp_pe[l], 'cmp_w': cmp_w[l],
              'w_out': w_out[l], 'norm2_g': norm2_g[l], 'w_gu': w_gu[l], 'w_down': w_down[l]}
        conv0 = jnp.zeros((bp, CONV_K - 1, CONV_CH), xp.dtype)
        h0 = jnp.zeros((bp, SSM_HEADS, SSM_HEAD_DIM, SSM_STATE), jnp.float32)
        xp, sp = trunk_layer(xp, conv0, h0, nsa_prompt, lp)
        nsa_s = functools.partial(nsa_sample, cmp_pages=cache_cmp_kv[l], slc_pages=cache_slc_kv[l],
                                  win_buf=state_win_kv[l], page_table=page_table)
        xs, ss = trunk_layer(xs, state_conv[l], state_ssm[l], nsa_s, lp)
        st_p.append(sp)
        st_s.append(ss)
    cmp_p = jnp.stack([s[0] for s in st_p])
    cmp_s = jnp.stack([s[0] for s in st_s])
    slc_p = jnp.stack([s[1] for s in st_p])
    slc_s = jnp.stack([s[1] for s in st_s])
    win_p = jnp.stack([s[2] for s in st_p])
    win_s = jnp.stack([s[2] for s in st_s])
    ssm_p = jnp.stack([s[3] for s in st_p])
    ssm_s = jnp.stack([s[3] for s in st_s])
    conv_p = jnp.stack([s[4] for s in st_p])
    conv_s = jnp.stack([s[4] for s in st_s])
    return (xp, xs, cmp_p, cmp_s, slc_p, slc_s, win_p, win_s, ssm_p, ssm_s, conv_p, conv_s)
```

```python
import functools

import jax
import jax.numpy as jnp
from jax import lax
from jax.experimental import pallas as pl
from jax.experimental.pallas import tpu as pltpu

F32 = jnp.float32
BF16 = jnp.bfloat16

RMS_EPS = 1e-6
SSM_HEAD_DIM = 64
SSM_GROUPS = 4
SSM_STATE = 128
CONV_K = 4
SSD_CHUNK = 128
HEAD_DIM = 64
KV_HEADS = 4
GQA = 4
ATT_HEADS = KV_HEADS * GQA
CMP_STRIDE = 16
L_CMP = 32
L_SEL = 64
N_SEL = 16
WINDOW = 512
SEL_FORCE = 1e9
KV_COLS = 2 * KV_HEADS * HEAD_DIM
K_COLS = KV_HEADS * HEAD_DIM

LANES = 128
Q_TILE = 128
SEL_KEY_TILE = 512
PAGES_PER_STEP = 16
VMEM_LIMIT = 56 * 1024 * 1024
NEG_BIG = -1e30


def _cparams(sem):
    return pltpu.CompilerParams(dimension_semantics=sem, vmem_limit_bytes=VMEM_LIMIT)


def _sigmoid(x):
    return 1.0 / (1.0 + jnp.exp(-x))


def _silu(x):
    return x * _sigmoid(x)


def _softplus(x):
    return jnp.maximum(x, 0.0) + jnp.log(1.0 + jnp.exp(-jnp.abs(x)))


def _split3(x):
    hi = x.astype(BF16)
    r = x - hi.astype(F32)
    mid = r.astype(BF16)
    lo = (r - mid.astype(F32)).astype(BF16)
    return hi, mid, lo


def _dot(a, b):
    return jnp.dot(a, b, preferred_element_type=F32)


def _dot_nt(a, b):
    return lax.dot_general(a, b, (((1,), (1,)), ((), ())), preferred_element_type=F32)


def _dot_tn(a, b):
    return lax.dot_general(a, b, (((0,), (0,)), ((), ())), preferred_element_type=F32)


def _dot3(x, m):
    hi, mid, lo = _split3(x)
    return _dot(hi, m) + _dot(mid, m) + _dot(lo, m)


def _head_rms(x, g2):
    lane = lax.broadcasted_iota(jnp.int32, (1, LANES), 1)
    lo = lane < HEAD_DIM
    outs = []
    for j in range(x.shape[1] // LANES):
        xt = x[:, j * LANES:(j + 1) * LANES]
        sq = xt * xt
        s_lo = jnp.sum(jnp.where(lo, sq, 0.0), axis=-1, keepdims=True)
        s_hi = jnp.sum(jnp.where(lo, 0.0, sq), axis=-1, keepdims=True)
        ms = jnp.where(lo, s_lo, s_hi) * (1.0 / HEAD_DIM)
        outs.append(xt * lax.rsqrt(ms + RMS_EPS) * g2)
    return outs[0] if len(outs) == 1 else jnp.concatenate(outs, axis=-1)


def _masked_softmax(s, mask):
    s = jnp.where(mask, s, -jnp.inf)
    m = jnp.max(s, axis=-1, keepdims=True)
    m = jnp.where(m == -jnp.inf, 0.0, m)
    e = jnp.where(mask, jnp.exp(s - m), 0.0)
    return e / jnp.maximum(jnp.sum(e, axis=-1, keepdims=True), 1e-30)


def _top_k_mask(score, k):
    lane = lax.broadcasted_iota(jnp.int32, score.shape, 1).astype(F32)
    sel = jnp.zeros(score.shape, F32)
    removed = -3.0
    for _ in range(k):
        m = jnp.max(score, axis=-1, keepdims=True)
        idx = jnp.min(jnp.where(score == m, lane, 1e9), axis=-1, keepdims=True)
        hit = lane == idx
        sel = jnp.where(hit, 1.0, sel)
        score = jnp.where(hit, removed, score)
    return sel


def _norm_matmul_kernel(x_ref, g_ref, w_ref, o_ref, xn_ref):
    @pl.when(pl.program_id(1) == 0)
    def _():
        x = x_ref[...]
        ms = jnp.mean(x * x, axis=-1, keepdims=True)
        xn_ref[...] = (x * lax.rsqrt(ms + RMS_EPS) * g_ref[...]).astype(BF16)

    o_ref[...] = _dot(xn_ref[...], w_ref[...]).astype(o_ref.dtype)


def _norm_matmul(x, g, w, tm, tn):
    rows, d = x.shape
    n = w.shape[1]
    return pl.pallas_call(
        _norm_matmul_kernel,
        grid=(rows // tm, n // tn),
        in_specs=[pl.BlockSpec((tm, d), lambda i, j: (i, 0)),
                  pl.BlockSpec((1, d), lambda i, j: (0, 0)),
                  pl.BlockSpec((d, tn), lambda i, j: (0, j))],
        out_specs=pl.BlockSpec((tm, tn), lambda i, j: (i, j)),
        out_shape=jax.ShapeDtypeStruct((rows, n), F32),
        scratch_shapes=[pltpu.VMEM((tm, d), BF16)],
        compiler_params=_cparams(("parallel", "arbitrary")),
        name="in_proj",
    )(x, g, w)


def _ffn_up_kernel(x_ref, g_ref, wg_ref, wv_ref, o_ref, xn_ref):
    @pl.when(pl.program_id(1) == 0)
    def _():
        x = x_ref[...]
        ms = jnp.mean(x * x, axis=-1, keepdims=True)
        xn_ref[...] = (x * lax.rsqrt(ms + RMS_EPS) * g_ref[...]).astype(BF16)

    xn = xn_ref[...]
    gate = _dot(xn, wg_ref[...])
    val = _dot(xn, wv_ref[...])
    o_ref[...] = (_silu(gate) * val).astype(o_ref.dtype)


def _ffn_up(x, g, w_gu, tm, tn):
    rows, d = x.shape
    d_ff = w_gu.shape[1] // 2
    nj = d_ff // tn
    return pl.pallas_call(
        _ffn_up_kernel,
        grid=(rows // tm, nj),
        in_specs=[pl.BlockSpec((tm, d), lambda i, j: (i, 0)),
                  pl.BlockSpec((1, d), lambda i, j: (0, 0)),
                  pl.BlockSpec((d, tn), lambda i, j: (0, j)),
                  pl.BlockSpec((d, tn), lambda i, j: (0, j + nj))],
        out_specs=pl.BlockSpec((tm, tn), lambda i, j: (i, j)),
        out_shape=jax.ShapeDtypeStruct((rows, d_ff), BF16),
        scratch_shapes=[pltpu.VMEM((tm, d), BF16)],
        compiler_params=_cparams(("parallel", "arbitrary")),
        name="ffn_up",
    )(x, g, w_gu, w_gu)


def _ffn_down_kernel(a_ref, w_ref, h_ref, o_ref):
    o_ref[...] = h_ref[...] + _dot(a_ref[...], w_ref[...])


def _ffn_down(act, w_down, h, tm, tn):
    rows, d_ff = act.shape
    d = w_down.shape[1]
    return pl.pallas_call(
        _ffn_down_kernel,
        grid=(rows // tm, d // tn),
        in_specs=[pl.BlockSpec((tm, d_ff), lambda i, j: (i, 0)),
                  pl.BlockSpec((d_ff, tn), lambda i, j: (0, j)),
                  pl.BlockSpec((tm, tn), lambda i, j: (i, j))],
        out_specs=pl.BlockSpec((tm, tn), lambda i, j: (i, j)),
        out_shape=jax.ShapeDtypeStruct((rows, d), F32),
        compiler_params=_cparams(("parallel", "arbitrary")),
        name="ffn_down",
    )(act, w_down, h)


def _out_proj_kernel(n_att, x_ref, y_ref, *refs):
    att_refs = refs[:n_att]
    w1_ref, w2_ref, o_ref = refs[n_att:]
    att = att_refs[0][...]
    for r in att_refs[1:]:
        att = att + r[...]
    o_ref[...] = (x_ref[...] + _dot(y_ref[...], w1_ref[...])
                  + _dot(att.astype(BF16), w2_ref[...]))


def _out_proj(x, y_ssd, atts, w_out, tm, tn):
    rows, d = x.shape
    half = y_ssd.shape[1]
    n_att = len(atts)
    return pl.pallas_call(
        functools.partial(_out_proj_kernel, n_att),
        grid=(rows // tm, d // tn),
        in_specs=([pl.BlockSpec((tm, tn), lambda i, j: (i, j)),
                   pl.BlockSpec((tm, half), lambda i, j: (i, 0))]
                  + [pl.BlockSpec((tm, half), lambda i, j: (i, 0)) for _ in atts]
                  + [pl.BlockSpec((half, tn), lambda i, j: (0, j)),
                     pl.BlockSpec((half, tn), lambda i, j: (1, j))]),
        out_specs=pl.BlockSpec((tm, tn), lambda i, j: (i, j)),
        out_shape=jax.ShapeDtypeStruct((rows, d), F32),
        compiler_params=_cparams(("parallel", "arbitrary")),
        name="out_proj",
    )(x, y_ssd, *atts, w_out, w_out)


def _prep_kernel(q_ref, ks_ref, kw_ref, sm_ref, gq_ref, gks_ref, gkw_ref,
                 qn_ref, kvs_ref, kvw_ref, khs_ref, vhs_ref, khw_ref, vhw_ref, gates_ref):
    qn = _head_rms(q_ref[...], gq_ref[...]) * (HEAD_DIM ** -0.5)
    qn_ref[...] = qn.astype(BF16)
    for src, g_ref, full_ref, kh_ref, vh_ref in ((ks_ref, gks_ref, kvs_ref, khs_ref, vhs_ref),
                                                 (kw_ref, gkw_ref, kvw_ref, khw_ref, vhw_ref)):
        kv = src[...]
        kn = _head_rms(kv[:, :K_COLS], g_ref[...])
        v = kv[:, K_COLS:]
        full_ref[:, :K_COLS] = kn
        full_ref[:, K_COLS:] = v
        for h in range(KV_HEADS):
            kh_ref[h] = kn[:, h * HEAD_DIM:(h + 1) * HEAD_DIM].astype(BF16)
            vh_ref[h] = v[:, h * HEAD_DIM:(h + 1) * HEAD_DIM].astype(BF16)
    sig = _sigmoid(sm_ref[...])
    n_dt = LANES // 8
    for h in range(KV_HEADS):
        gates_ref[h] = pltpu.roll(sig, LANES - n_dt - 3 * GQA * h, axis=1)


def _prep(u, col, g_q, g_ks, g_kw, tm):
    rows = u.shape[0]
    nsa = ATT_HEADS * HEAD_DIM
    row_spec = lambda w, c: pl.BlockSpec((tm, w), lambda i, c=c: (i, c))
    head_out = jax.ShapeDtypeStruct((KV_HEADS, rows, HEAD_DIM), BF16)
    head_spec = pl.BlockSpec((KV_HEADS, tm, HEAD_DIM), lambda i: (0, i, 0))
    vec = pl.BlockSpec((1, LANES), lambda i: (0, 0))
    return pl.pallas_call(
        _prep_kernel,
        grid=(rows // tm,),
        in_specs=[row_spec(nsa, col["q"]), row_spec(KV_COLS, col["kvs"]),
                  row_spec(KV_COLS, col["kvw"]), row_spec(LANES, col["small"]), vec, vec, vec],
        out_specs=[pl.BlockSpec((tm, nsa), lambda i: (i, 0)),
                   pl.BlockSpec((tm, KV_COLS), lambda i: (i, 0)),
                   pl.BlockSpec((tm, KV_COLS), lambda i: (i, 0)),
                   head_spec, head_spec, head_spec, head_spec,
                   pl.BlockSpec((KV_HEADS, tm, LANES), lambda i: (0, i, 0))],
        out_shape=[jax.ShapeDtypeStruct((rows, nsa), BF16),
                   jax.ShapeDtypeStruct((rows, KV_COLS), F32),
                   jax.ShapeDtypeStruct((rows, KV_COLS), F32),
                   head_out, head_out, head_out, head_out,
                   jax.ShapeDtypeStruct((KV_HEADS, rows, LANES), F32)],
        compiler_params=_cparams(("parallel",)),
        name="head_prep",
    )(u, u, u, u, g_q, g_ks, g_kw)


def _gated_norm(y, xs, z, dsk, gn):
    y = (y + dsk * xs) * _silu(z)
    ms = jnp.mean(y * y, axis=-1, keepdims=True)
    return y * lax.rsqrt(ms + RMS_EPS) * gn


def _ssd_prompt_kernel(xbc_ref, z_ref, sm_ref, cw_ref, cb_ref, dtb_ref, alog_ref, dsk_ref, gn_ref,
                       y_ref, hout_ref, xb_ref, st_ref):
    c = pl.program_id(0)
    q = SSD_CHUNK
    ssm_dim = z_ref.shape[1]
    gw = SSM_STATE
    pad = 8

    @pl.when(c == 0)
    def _():
        xb_ref[0:pad, :] = jnp.zeros((pad, xb_ref.shape[1]), F32)
        st_ref[...] = jnp.zeros(st_ref.shape, F32)

    xb_ref[pad:pad + q, :] = xbc_ref[...]
    conv = cb_ref[...]
    for k in range(CONV_K):
        conv = conv + cw_ref[k:k + 1, :] * xb_ref[pl.ds(pad - (CONV_K - 1) + k, q), :]
    xb_ref[0:pad, :] = xb_ref[q:q + pad, :]
    act = _silu(conv)
    xs = act[:, :ssm_dim]
    bm = act[:, ssm_dim:ssm_dim + SSM_GROUPS * gw].astype(BF16)
    cm = act[:, ssm_dim + SSM_GROUPS * gw:].astype(BF16)

    lane = lax.broadcasted_iota(jnp.int32, (1, LANES), 1)
    n_heads = ssm_dim // SSM_HEAD_DIM
    dt = jnp.where(lane < n_heads, _softplus(sm_ref[...] + dtb_ref[...]), 0.0)
    a = -jnp.exp(alog_ref[...])
    ri = lax.broadcasted_iota(jnp.int32, (q, q), 0)
    ci = lax.broadcasted_iota(jnp.int32, (q, q), 1)
    causal = ri >= ci
    tri = jnp.where(causal, 1.0, 0.0).astype(BF16)
    da_hi, da_mid, da_lo = _split3(dt * a)
    acum = _dot(tri, da_hi) + _dot(tri, da_mid) + _dot(tri, da_lo)
    acum_t = acum.T
    dt_t = dt.T
    e_acum = jnp.exp(acum)
    a_last = acum[q - 1:q, :]
    w_end = dt * jnp.exp(a_last - acum)
    e_last = jnp.exp(a_last)
    lo = lane < SSM_HEAD_DIM
    row_lo = lax.broadcasted_iota(jnp.int32, (LANES, 1), 0) < SSM_HEAD_DIM

    ys = []
    heads_per_group = n_heads // SSM_GROUPS
    for pr in range(n_heads // 2):
        h0, h1 = 2 * pr, 2 * pr + 1
        g = h0 // heads_per_group
        cg = cm[:, g * gw:(g + 1) * gw]
        bg = bm[:, g * gw:(g + 1) * gw]
        cb = _dot_nt(cg, bg)
        x2 = xs[:, pr * LANES:(pr + 1) * LANES]
        x2b = x2.astype(BF16)
        yd = []
        for h in (h0, h1):
            seg = acum[:, h:h + 1] - acum_t[h:h + 1, :]
            dec = jnp.where(causal, jnp.exp(jnp.where(causal, seg, 0.0)), 0.0)
            m = (cb * dec * dt_t[h:h + 1, :]).astype(BF16)
            yd.append(_dot(m, x2b))
        y_diag = jnp.where(lo, yd[0], yd[1])
        sp = st_ref[pr * LANES:(pr + 1) * LANES, :]
        ea = jnp.where(lo, e_acum[:, h0:h0 + 1], e_acum[:, h1:h1 + 1])
        y_off = _dot_nt(cg, sp.astype(BF16)) * ea
        w2 = jnp.where(lo, w_end[:, h0:h0 + 1], w_end[:, h1:h1 + 1])
        xw_t = (x2 * w2).T.astype(BF16)
        cd = jnp.where(row_lo, e_last[:, h0:h0 + 1], e_last[:, h1:h1 + 1])
        st_ref[pr * LANES:(pr + 1) * LANES, :] = sp * cd + _dot(xw_t, bg)
        ys.append(y_diag + y_off)
    y = jnp.concatenate(ys, axis=-1)
    y_ref[...] = _gated_norm(y, xs, z_ref[...], dsk_ref[...], gn_ref[...]).astype(y_ref.dtype)

    @pl.when(c == pl.num_programs(0) - 1)
    def _():
        hout_ref[...] = st_ref[...]


def _ssd_prompt(u, col, lw):
    t = u.shape[0]
    ssm_dim = lw["dsk"].shape[1]
    conv_ch = lw["conv_w"].shape[1]
    q = SSD_CHUNK
    full = lambda r, w: pl.BlockSpec((r, w), lambda c: (0, 0))
    return pl.pallas_call(
        _ssd_prompt_kernel,
        grid=(t // q,),
        in_specs=[pl.BlockSpec((q, conv_ch), lambda c: (c, col["xbc"])),
                  pl.BlockSpec((q, ssm_dim), lambda c: (c, col["z"])),
                  pl.BlockSpec((q, LANES), lambda c: (c, col["small"])),
                  full(CONV_K, conv_ch), full(1, conv_ch), full(1, LANES), full(1, LANES),
                  full(1, ssm_dim), full(1, ssm_dim)],
        out_specs=[pl.BlockSpec((q, ssm_dim), lambda c: (c, 0)),
                   pl.BlockSpec((ssm_dim, SSM_STATE), lambda c: (0, 0))],
        out_shape=[jax.ShapeDtypeStruct((t, ssm_dim), BF16),
                   jax.ShapeDtypeStruct((ssm_dim, SSM_STATE), F32)],
        scratch_shapes=[pltpu.VMEM((q + 8, conv_ch), F32),
                        pltpu.VMEM((ssm_dim, SSM_STATE), F32)],
        compiler_params=_cparams(("arbitrary",)),
        name="ssd_prompt",
    )(u, u, u, lw["conv_w"], lw["conv_b"], lw["dt_bias"], lw["a_log"], lw["dsk"], lw["ssm_norm_g"])


def _ssd_sample_kernel(xbc_ref, z_ref, sm_ref, cbuf_ref, h_ref, cw_ref, cb_ref, dtb_ref, alog_ref,
                       dsk_ref, gn_ref, exp_ref, y_ref, hout_ref):
    b = pl.program_id(0)
    ssm_dim = z_ref.shape[1]
    gw = SSM_STATE
    n_heads = ssm_dim // SSM_HEAD_DIM
    conv = cb_ref[...] + cw_ref[CONV_K - 1:CONV_K, :] * xbc_ref[pl.ds(b, 1), :]
    for k in range(CONV_K - 1):
        conv = conv + cw_ref[k:k + 1, :] * cbuf_ref[k, pl.ds(b, 1), :]
    act = _silu(conv)
    xs = act[:, :ssm_dim]
    lane = lax.broadcasted_iota(jnp.int32, (1, LANES), 1)
    dt_raw = jnp.broadcast_to(jnp.where(lane < n_heads, sm_ref[pl.ds(b, 1), :], 0.0), (8, LANES))
    dt = _softplus(_dot3(dt_raw, exp_ref[...])[0:1, :] + dtb_ref[...])
    dec = jnp.exp(dt * (-jnp.exp(alog_ref[...])))
    dtx = dt * xs
    row8 = lax.broadcasted_iota(jnp.int32, (8, 1), 0)

    def rows8(pieces):
        out = jnp.zeros((8, pieces[0].shape[1]), F32)
        for r, p in enumerate(pieces):
            out = jnp.where(row8 == r, p.astype(F32), out)
        return out.astype(BF16)

    dec8 = rows8(_split3(dec))
    dtx8 = rows8(_split3(dtx))
    ones8 = jnp.where(row8 < 3, 1.0, 0.0).astype(BF16) * jnp.ones((1, gw), BF16)
    gh = (n_heads // SSM_GROUPS) * SSM_HEAD_DIM
    ys = []
    for g in range(SSM_GROUPS):
        bg = act[:, ssm_dim + g * gw:ssm_dim + (g + 1) * gw].astype(BF16)
        cg = act[:, ssm_dim + (SSM_GROUPS + g) * gw:ssm_dim + (SSM_GROUPS + g + 1) * gw].astype(BF16)
        b8 = jnp.where(row8 < 3, 1.0, 0.0).astype(BF16) * bg
        c8 = jnp.where(row8 < 1, 1.0, 0.0).astype(BF16) * cg
        dec_col = _dot_tn(dec8[:, g * gh:(g + 1) * gh], ones8)
        dbx = _dot_tn(dtx8[:, g * gh:(g + 1) * gh], b8)
        s_new = h_ref[g * gh:(g + 1) * gh, :] * dec_col + dbx
        hout_ref[g * gh:(g + 1) * gh, :] = s_new
        ys.append(_dot_nt(c8, s_new.astype(BF16))[0:1, :])
    y = jnp.concatenate(ys, axis=-1)
    y_ref[pl.ds(b, 1), :] = _gated_norm(y, xs, z_ref[pl.ds(b, 1), :], dsk_ref[...], gn_ref[...])


def _ssd_sample(u, col, conv_t, h0, lw):
    bs = u.shape[0]
    ssm_dim = lw["dsk"].shape[1]
    conv_ch = lw["conv_w"].shape[1]
    full = lambda r, w: pl.BlockSpec((r, w), lambda b: (0, 0))
    return pl.pallas_call(
        _ssd_sample_kernel,
        grid=(bs,),
        in_specs=[pl.BlockSpec((bs, conv_ch), lambda b: (0, col["xbc"])),
                  pl.BlockSpec((bs, ssm_dim), lambda b: (0, col["z"])),
                  pl.BlockSpec((bs, LANES), lambda b: (0, col["small"])),
                  pl.BlockSpec((CONV_K - 1, bs, conv_ch), lambda b: (0, 0, 0)),
                  pl.BlockSpec((None, ssm_dim, SSM_STATE), lambda b: (b, 0, 0)),
                  full(CONV_K, conv_ch), full(1, conv_ch), full(1, ssm_dim), full(1, ssm_dim),
                  full(1, ssm_dim), full(1, ssm_dim), full(LANES, ssm_dim)],
        out_specs=[pl.BlockSpec((bs, ssm_dim), lambda b: (0, 0)),
                   pl.BlockSpec((None, ssm_dim, SSM_STATE), lambda b: (b, 0, 0))],
        out_shape=[jax.ShapeDtypeStruct((bs, ssm_dim), F32),
                   jax.ShapeDtypeStruct((bs, ssm_dim, SSM_STATE), F32)],
        compiler_params=_cparams(("arbitrary",)),
        name="ssd_sample",
    )(u, u, u, conv_t, h0, lw["conv_w"], lw["conv_b"], lw["dt_bias_e"], lw["a_log_e"], lw["dsk"],
      lw["ssm_norm_g"], lw["head_expand"])


def _compress_rows(buf_ref, n_blocks, pe_ref, w_ref, gk_ref):
    acc = [jnp.zeros((n_blocks, LANES), F32) for _ in range(KV_COLS // LANES)]
    for l in range(L_CMP):
        for p in range(KV_COLS // LANES):
            xl = buf_ref[p, pl.ds(l, n_blocks, stride=CMP_STRIDE), :] + pe_ref[l:l + 1, p * LANES:(p + 1) * LANES]
            acc[p] = acc[p] + _dot(xl.astype(BF16), w_ref[l, p // 2])
    ck = _head_rms(jnp.concatenate(acc[:2], axis=-1), gk_ref[...])
    cv = jnp.concatenate(acc[2:], axis=-1)
    return ck, cv


def _store_heads(ck, cv, ck_ref, cv_ref):
    for h in range(KV_HEADS):
        ck_ref[h] = ck[:, h * HEAD_DIM:(h + 1) * HEAD_DIM].astype(BF16)
        cv_ref[h] = cv[:, h * HEAD_DIM:(h + 1) * HEAD_DIM].astype(BF16)


def _compress_prompt_kernel(main_ref, nxt_ref, pe_ref, w_ref, gk_ref, ck_ref, cv_ref, buf_ref):
    rows = main_ref.shape[0]
    for p in range(KV_COLS // LANES):
        buf_ref[p, 0:rows, :] = main_ref[:, p * LANES:(p + 1) * LANES]
        buf_ref[p, rows:rows + CMP_STRIDE, :] = nxt_ref[:, p * LANES:(p + 1) * LANES]
    ck, cv = _compress_rows(buf_ref, rows // CMP_STRIDE, pe_ref, w_ref, gk_ref)
    _store_heads(ck, cv, ck_ref, cv_ref)


def _compress_prompt(kvc_pad, lw, rows_per_step):
    t = kvc_pad.shape[0] - CMP_STRIDE
    nb = rows_per_step // CMP_STRIDE
    out = jax.ShapeDtypeStruct((KV_HEADS, t // CMP_STRIDE, HEAD_DIM), BF16)
    out_spec = pl.BlockSpec((KV_HEADS, nb, HEAD_DIM), lambda i: (0, i, 0))
    return pl.pallas_call(
        _compress_prompt_kernel,
        grid=(t // rows_per_step,),
        in_specs=[pl.BlockSpec((rows_per_step, KV_COLS), lambda i: (i, 0)),
                  pl.BlockSpec((CMP_STRIDE, KV_COLS), lambda i: ((i + 1) * nb, 0)),
                  pl.BlockSpec((L_CMP, KV_COLS), lambda i: (0, 0)),
                  pl.BlockSpec((L_CMP, 2, LANES, LANES), lambda i: (0, 0, 0, 0)),
                  pl.BlockSpec((1, LANES), lambda i: (0, 0))],
        out_specs=[out_spec, out_spec],
        out_shape=[out, out],
        scratch_shapes=[pltpu.VMEM((KV_COLS // LANES, rows_per_step + CMP_STRIDE, LANES), F32)],
        compiler_params=_cparams(("parallel",)),
        name="compress_prompt",
    )(kvc_pad, kvc_pad, lw["cmp_pe_rows"], lw["cmp_w_pairs"], lw["g_kc"])


def _compress_sample_kernel(pt_ref, *refs):
    n_pg = PAGES_PER_STEP
    page_refs = refs[:n_pg]
    nxt_ref, pe_ref, w_ref, gk_ref, ck_ref, cv_ref, buf_ref = refs[n_pg:]
    pg = page_refs[0].shape[0]
    for p in range(KV_COLS // LANES):
        for k in range(n_pg):
            buf_ref[p, k * pg:(k + 1) * pg, :] = page_refs[k][:, p * LANES:(p + 1) * LANES]
        buf_ref[p, n_pg * pg:n_pg * pg + CMP_STRIDE, :] = nxt_ref[:, p * LANES:(p + 1) * LANES]
    ck, cv = _compress_rows(buf_ref, n_pg * pg // CMP_STRIDE, pe_ref, w_ref, gk_ref)
    _store_heads(ck, cv, ck_ref, cv_ref)


def _compress_sample(cache2d, page_table, lw):
    bs, n_pages = page_table.shape
    pg = cache2d.shape[1]
    n_pg = PAGES_PER_STEP
    nb = n_pg * pg // CMP_STRIDE
    past = n_pages * pg

    def page_map(b, j, pt, k):
        return (pt[b, j * n_pg + k], 0, 0)

    def next_map(b, j, pt):
        return (pt[b, jnp.minimum((j + 1) * n_pg, n_pages - 1)], 0, 0)

    out = jax.ShapeDtypeStruct((bs, KV_HEADS, past // CMP_STRIDE, HEAD_DIM), BF16)
    out_spec = pl.BlockSpec((None, KV_HEADS, nb, HEAD_DIM), lambda b, j, pt: (b, 0, j, 0))
    grid_spec = pltpu.PrefetchScalarGridSpec(
        num_scalar_prefetch=1,
        grid=(bs, n_pages // n_pg),
        in_specs=([pl.BlockSpec((None, pg, KV_COLS), functools.partial(page_map, k=k)) for k in range(n_pg)]
                  + [pl.BlockSpec((None, CMP_STRIDE, KV_COLS), next_map),
                     pl.BlockSpec((L_CMP, KV_COLS), lambda b, j, pt: (0, 0)),
                     pl.BlockSpec((L_CMP, 2, LANES, LANES), lambda b, j, pt: (0, 0, 0, 0)),
                     pl.BlockSpec((1, LANES), lambda b, j, pt: (0, 0))]),
        out_specs=[out_spec, out_spec],
        scratch_shapes=[pltpu.VMEM((KV_COLS // LANES, n_pg * pg + CMP_STRIDE, LANES), F32)],
    )
    return pl.pallas_call(
        _compress_sample_kernel,
        grid_spec=grid_spec,
        out_shape=[out, out],
        compiler_params=_cparams(("parallel", "parallel")),
        name="compress_sample",
    )(page_table, *([cache2d] * (n_pg + 1)), lw["cmp_pe_rows"], lw["cmp_w_pairs"], lw["g_kc"])


def _cover_matrix(nc, n_sel_blocks, width):
    n = lax.broadcasted_iota(jnp.int32, (nc, width), 0) * CMP_STRIDE
    s = lax.broadcasted_iota(jnp.int32, (nc, width), 1)
    hit = (n < s * L_SEL + L_SEL) & (n + L_CMP > s * L_SEL) & (s < n_sel_blocks)
    return jnp.where(hit, 1.0, 0.0).astype(BF16)


def _block_scores(imp, q_pos, n_sel_blocks):
    blk = lax.broadcasted_iota(jnp.int32, imp.shape, 1)
    valid = blk * L_SEL <= q_pos
    cur = q_pos // L_SEL
    forced = (blk == 0) | (blk == cur) | (blk == cur - 1)
    score = jnp.where(valid & forced, SEL_FORCE, jnp.where(valid, imp, -1.0))
    return jnp.where(blk < n_sel_blocks, score, -2.0)


def _cmp_prompt_kernel(q_ref, ck_ref, cv_ref, gates_ref, oc_ref, sel_ref):
    i = pl.program_id(0)
    tq = q_ref.shape[0]
    nc = ck_ref.shape[1]
    n_sel_blocks = nc * CMP_STRIDE // L_SEL
    q_pos = i * tq + lax.broadcasted_iota(jnp.int32, (tq, 1), 0)
    n_end = lax.broadcasted_iota(jnp.int32, (1, nc), 1) * CMP_STRIDE + (L_CMP - 1)
    mask = (n_end <= q_pos) & (n_end < nc * CMP_STRIDE)
    cover = _cover_matrix(nc, n_sel_blocks, LANES)
    q = q_ref[...]
    outs = []
    for h in range(KV_HEADS):
        ck = ck_ref[h]
        cv = cv_ref[h]
        gates = gates_ref[h]
        p_sum = jnp.zeros((tq, nc), F32)
        for g in range(GQA):
            hg = h * GQA + g
            p = _masked_softmax(_dot_nt(q[:, hg * HEAD_DIM:(hg + 1) * HEAD_DIM], ck), mask)
            outs.append(_dot(p.astype(BF16), cv) * gates[:, 3 * g:3 * g + 1])
            p_sum = p_sum + p
        imp = _dot3(p_sum, cover)
        sel_ref[h] = _top_k_mask(_block_scores(imp, q_pos, n_sel_blocks), N_SEL).astype(sel_ref.dtype)
    oc_ref[...] = jnp.concatenate(outs, axis=-1)


def _cmp_prompt(qn, ck, cv, gates_h, t):
    nsa = qn.shape[1]
    nc = ck.shape[1]
    tq = Q_TILE
    kv_spec = pl.BlockSpec((KV_HEADS, nc, HEAD_DIM), lambda i: (0, 0, 0))
    return pl.pallas_call(
        _cmp_prompt_kernel,
        grid=(t // tq,),
        in_specs=[pl.BlockSpec((tq, nsa), lambda i: (i, 0)), kv_spec, kv_spec,
                  pl.BlockSpec((KV_HEADS, tq, LANES), lambda i: (0, i, 0))],
        out_specs=[pl.BlockSpec((tq, nsa), lambda i: (i, 0)),
                   pl.BlockSpec((KV_HEADS, tq, LANES), lambda i: (0, i, 0))],
        out_shape=[jax.ShapeDtypeStruct((t, nsa), F32),
                   jax.ShapeDtypeStruct((KV_HEADS, t, LANES), BF16)],
        compiler_params=_cparams(("parallel",)),
        name="cmp_topk_prompt",
    )(qn, ck, cv, gates_h)


def _stack_gqa(q):
    return jnp.concatenate([q[:, g * HEAD_DIM:(g + 1) * HEAD_DIM] for g in range(GQA)], axis=0)


def _unstack_gated(o, gates, branch, tq):
    return jnp.concatenate(
        [o[g * tq:(g + 1) * tq] * gates[:, 3 * g + branch:3 * g + branch + 1] for g in range(GQA)], axis=-1)


def _sel_prompt_kernel(q_ref, k_ref, v_ref, sel_ref, gates_ref, o_ref):
    i = pl.program_id(1)
    tq = q_ref.shape[0]
    tk = SEL_KEY_TILE
    bpt = tk // L_SEL
    q4 = _stack_gqa(q_ref[...])
    sel = sel_ref[...]
    q_pos = i * tq + lax.broadcasted_iota(jnp.int32, (tq, 1), 0)
    blk_of_key = (lax.broadcasted_iota(jnp.int32, (LANES, tk), 0)
                  - lax.broadcasted_iota(jnp.int32, (LANES, tk), 1) // L_SEL)
    key_off = lax.broadcasted_iota(jnp.int32, (1, tk), 1)
    n_tiles = (i * tq + tq - 1) // tk + 1

    def body(j, carry):
        m, l, acc = carry
        k = k_ref[pl.ds(pl.multiple_of(j * tk, tk), tk), :]
        v = v_ref[pl.ds(pl.multiple_of(j * tk, tk), tk), :]
        s = _dot_nt(q4, k).reshape(GQA, tq, tk)
        expand = jnp.where(blk_of_key == j * bpt, 1.0, 0.0).astype(BF16)
        picked = _dot(sel, expand)
        ok = (picked > 0.5) & (key_off + j * tk <= q_pos)
        s = jnp.where(ok[None], s, NEG_BIG)
        m_new = jnp.maximum(m, jnp.max(s, axis=-1, keepdims=True))
        alpha = jnp.exp(m - m_new)
        p = jnp.where(ok[None], jnp.exp(s - m_new), 0.0)
        l = alpha * l + jnp.sum(p, axis=-1, keepdims=True)
        pv = _dot(p.reshape(GQA * tq, tk).astype(BF16), v).reshape(GQA, tq, HEAD_DIM)
        return m_new, l, alpha * acc + pv

    m0 = jnp.full((GQA, tq, 1), NEG_BIG, F32)
    l0 = jnp.zeros((GQA, tq, 1), F32)
    a0 = jnp.zeros((GQA, tq, HEAD_DIM), F32)
    m, l, acc = lax.fori_loop(0, n_tiles, body, (m0, l0, a0))
    o = (acc / jnp.maximum(l, 1e-30)).reshape(GQA * tq, HEAD_DIM)
    o_ref[...] = _unstack_gated(o, gates_ref[...], 1, tq)


def _sel_prompt(qn, k_h, v_h, sel, gates_h, t):
    nsa = qn.shape[1]
    tq = Q_TILE
    kv_spec = pl.BlockSpec((None, t, HEAD_DIM), lambda h, i: (h, 0, 0))
    return pl.pallas_call(
        _sel_prompt_kernel,
        grid=(KV_HEADS, t // tq),
        in_specs=[pl.BlockSpec((tq, GQA * HEAD_DIM), lambda h, i: (i, h)), kv_spec, kv_spec,
                  pl.BlockSpec((None, tq, LANES), lambda h, i: (h, i, 0)),
                  pl.BlockSpec((None, tq, LANES), lambda h, i: (h, i, 0))],
        out_specs=pl.BlockSpec((tq, GQA * HEAD_DIM), lambda h, i: (i, h)),
        out_shape=jax.ShapeDtypeStruct((t, nsa), F32),
        compiler_params=_cparams(("parallel", "arbitrary")),
        name="sel_prompt",
    )(qn, k_h, v_h, sel, gates_h)


def _win_prompt_kernel(q_ref, k_ref, v_ref, gates_ref, o_ref):
    i = pl.program_id(1)
    tq = q_ref.shape[0]
    span = WINDOW + tq
    start = pl.multiple_of(jnp.maximum(i * tq - WINDOW, 0), tq)
    q4 = _stack_gqa(q_ref[...])
    k = k_ref[pl.ds(start, span), :]
    v = v_ref[pl.ds(start, span), :]
    q_pos = i * tq + lax.broadcasted_iota(jnp.int32, (tq, 1), 0)
    k_pos = start + lax.broadcasted_iota(jnp.int32, (1, span), 1)
    ok = (k_pos <= q_pos) & (k_pos >= q_pos - WINDOW)
    s = _dot_nt(q4, k).reshape(GQA, tq, span)
    p = _masked_softmax(s, ok[None])
    o = _dot(p.reshape(GQA * tq, span).astype(BF16), v)
    o_ref[...] = _unstack_gated(o, gates_ref[...], 2, tq)


def _win_prompt(qn, k_h, v_h, gates_h, t):
    nsa = qn.shape[1]
    tq = Q_TILE
    kv_spec = pl.BlockSpec((None, t, HEAD_DIM), lambda h, i: (h, 0, 0))
    return pl.pallas_call(
        _win_prompt_kernel,
        grid=(KV_HEADS, t // tq),
        in_specs=[pl.BlockSpec((tq, GQA * HEAD_DIM), lambda h, i: (i, h)), kv_spec, kv_spec,
                  pl.BlockSpec((None, tq, LANES), lambda h, i: (h, i, 0))],
        out_specs=pl.BlockSpec((tq, GQA * HEAD_DIM), lambda h, i: (i, h)),
        out_shape=jax.ShapeDtypeStruct((t, nsa), F32),
        compiler_params=_cparams(("parallel", "arbitrary")),
        name="win_prompt",
    )(qn, k_h, v_h, gates_h)


def _own_head_rows(parts, width):
    row_head = lax.broadcasted_iota(jnp.int32, (ATT_HEADS, 1), 0) // GQA
    out = jnp.zeros((ATT_HEADS, width), F32)
    for h in range(KV_HEADS):
        out = jnp.where(row_head == h, parts[h], out)
    return out


def _cmp_sample_kernel(past, q_ref, ck_ref, cv_ref, g3_ref, oc_ref, sel_ref):
    nc = ck_ref.shape[1]
    width = sel_ref.shape[1]
    n_sel_blocks = past // L_SEL + 1
    q = q_ref[...]
    n_end = lax.broadcasted_iota(jnp.int32, (1, nc), 1) * CMP_STRIDE + (L_CMP - 1)
    mask = n_end <= past
    s = _own_head_rows([_dot_nt(q, ck_ref[h]) for h in range(KV_HEADS)], nc)
    p = _masked_softmax(s, mask)
    pb = p.astype(BF16)
    o = _own_head_rows([_dot(pb, cv_ref[h]) for h in range(KV_HEADS)], HEAD_DIM)
    oc_ref[...] = o * g3_ref[:, 0:1]
    row_head = lax.broadcasted_iota(jnp.int32, (ATT_HEADS, 1), 0) // GQA
    row8 = lax.broadcasted_iota(jnp.int32, (8, 1), 0)
    p_sum = jnp.zeros((8, nc), F32)
    for h in range(KV_HEADS):
        ph = jnp.sum(jnp.where(row_head == h, p, 0.0), axis=0, keepdims=True)
        p_sum = jnp.where(row8 == h, ph, p_sum)
    imp = _dot3(p_sum, _cover_matrix(nc, n_sel_blocks, width))
    q_pos = jnp.full((8, 1), past, jnp.int32)
    sel_ref[...] = _top_k_mask(_block_scores(imp, q_pos, n_sel_blocks), N_SEL)


def _cmp_sample(q3, ck_s, cv_s, g3, past):
    bs = q3.shape[0]
    nc = ck_s.shape[2]
    width = -(-(past // L_SEL + 1) // LANES) * LANES
    kv_spec = pl.BlockSpec((None, KV_HEADS, nc, HEAD_DIM), lambda b: (b, 0, 0, 0))
    return pl.pallas_call(
        functools.partial(_cmp_sample_kernel, past),
        grid=(bs,),
        in_specs=[pl.BlockSpec((None, ATT_HEADS, HEAD_DIM), lambda b: (b, 0, 0)), kv_spec, kv_spec,
                  pl.BlockSpec((None, ATT_HEADS, LANES), lambda b: (b, 0, 0))],
        out_specs=[pl.BlockSpec((None, ATT_HEADS, HEAD_DIM), lambda b: (b, 0, 0)),
                   pl.BlockSpec((None, 8, width), lambda b: (b, 0, 0))],
        out_shape=[jax.ShapeDtypeStruct((bs, ATT_HEADS, HEAD_DIM), F32),
                   jax.ShapeDtypeStruct((bs, 8, width), F32)],
        compiler_params=_cparams(("parallel",)),
        name="cmp_topk_sample",
    )(q3, ck_s, cv_s, g3)


def _diag_heads(x):
    return _own_head_rows([x[:, h * HEAD_DIM:(h + 1) * HEAD_DIM] for h in range(KV_HEADS)], HEAD_DIM)


def _sel_win_sample_kernel(past, pt_ref, *refs):
    n_pg = PAGES_PER_STEP
    page_refs = refs[:n_pg]
    (qbd_ref, selg_ref, selnew_ref, knew_ref, win_ref, wnew_ref, oc_ref, g3_ref,
     o_ref, m_ref, l_ref, acc_ref) = refs[n_pg:]
    j = pl.program_id(1)
    pg = page_refs[0].shape[0]
    qbd = qbd_ref[...]
    bpp = pg // L_SEL

    @pl.when(j == 0)
    def _():
        m_ref[...] = jnp.full(m_ref.shape, NEG_BIG, F32)
        l_ref[...] = jnp.zeros(l_ref.shape, F32)
        acc_ref[...] = jnp.zeros(acc_ref.shape, F32)

    sel = selg_ref[...].astype(BF16)
    n_blk = n_pg * bpp
    expand = jnp.where(lax.broadcasted_iota(jnp.int32, (n_blk, n_pg * pg), 0)
                       == lax.broadcasted_iota(jnp.int32, (n_blk, n_pg * pg), 1) // L_SEL, 1.0, 0.0).astype(BF16)
    picked = _dot(sel, expand)
    m, l, acc = m_ref[...], l_ref[...], acc_ref[...]
    for k in range(n_pg):
        page = page_refs[k][...]
        kk = page[:, :K_COLS].astype(BF16)
        vv = page[:, K_COLS:].astype(BF16)
        ok = picked[:, k * pg:(k + 1) * pg] > 0.5
        s = jnp.where(ok, _dot_nt(qbd, kk), NEG_BIG)
        m_new = jnp.maximum(m, jnp.max(s, axis=-1, keepdims=True))
        alpha = jnp.exp(m - m_new)
        p = jnp.where(ok, jnp.exp(s - m_new), 0.0)
        l = alpha * l + jnp.sum(p, axis=-1, keepdims=True)
        acc = alpha * acc + _dot(p.astype(BF16), vv)
        m = m_new
    m_ref[...], l_ref[...], acc_ref[...] = m, l, acc

    @pl.when(j == pl.num_programs(1) - 1)
    def _():
        qf = qbd.astype(F32)
        kn = knew_ref[...]
        s_new = jnp.sum(qf * kn[:, :K_COLS].astype(BF16).astype(F32), axis=-1, keepdims=True)
        ok_new = selnew_ref[:, 0:1] > 0.5
        s_new = jnp.where(ok_new, s_new, NEG_BIG)
        m2 = jnp.maximum(m, s_new)
        a2 = jnp.exp(m - m2)
        p_new = jnp.where(ok_new, jnp.exp(s_new - m2), 0.0)
        l2 = a2 * l + p_new
        acc2 = a2 * acc + p_new.astype(BF16).astype(F32) * kn[:, K_COLS:].astype(BF16).astype(F32)
        o_s = _diag_heads(acc2 / jnp.maximum(l2, 1e-30))
        w = win_ref[...]
        w_buf = w.shape[0]
        wn = wnew_ref[...]
        w_pos = past - w_buf + lax.broadcasted_iota(jnp.int32, (1, w_buf), 1)
        ok_w = (w_pos <= past) & (w_pos >= past - WINDOW) & (w_pos >= 0)
        s_w = jnp.where(ok_w, _dot_nt(qbd, w[:, :K_COLS].astype(BF16)), -jnp.inf)
        s_wn = jnp.sum(qf * wn[:, :K_COLS].astype(BF16).astype(F32), axis=-1, keepdims=True)
        m_w = jnp.maximum(jnp.max(s_w, axis=-1, keepdims=True), s_wn)
        p_w = jnp.where(ok_w, jnp.exp(s_w - m_w), 0.0)
        p_wn = jnp.exp(s_wn - m_w)
        l_w = jnp.sum(p_w, axis=-1, keepdims=True) + p_wn
        p_w = p_w / l_w
        p_wn = p_wn / l_w
        o_w = _diag_heads(_dot(p_w.astype(BF16), w[:, K_COLS:].astype(BF16))
                          + p_wn.astype(BF16).astype(F32) * wn[:, K_COLS:].astype(BF16).astype(F32))
        g3 = g3_ref[...]
        o_ref[...] = oc_ref[...] + g3[:, 1:2] * o_s + g3[:, 2:3] * o_w


def _sel_win_sample(cache2d, page_table, qbd, sel_groups, sel_new, kvs_new, win2d, kvw_new, oc, g3, past):
    bs, n_pages = page_table.shape
    pg = cache2d.shape[1]
    n_pg = PAGES_PER_STEP
    bps = n_pg * pg // L_SEL
    w_buf = win2d.shape[1]

    def page_map(b, j, pt, k):
        return (pt[b, j * n_pg + k], 0, 0)

    per_b = lambda *shape: pl.BlockSpec((None,) + shape, lambda b, j, pt: (b,) + (0,) * len(shape))
    grid_spec = pltpu.PrefetchScalarGridSpec(
        num_scalar_prefetch=1,
        grid=(bs, n_pages // n_pg),
        in_specs=([pl.BlockSpec((None, pg, KV_COLS), functools.partial(page_map, k=k)) for k in range(n_pg)]
                  + [per_b(ATT_HEADS, K_COLS),
                     pl.BlockSpec((None, None, ATT_HEADS, bps), lambda b, j, pt: (b, j, 0, 0)),
                     per_b(ATT_HEADS, LANES), per_b(1, KV_COLS), per_b(w_buf, KV_COLS), per_b(1, KV_COLS),
                     per_b(ATT_HEADS, HEAD_DIM), per_b(ATT_HEADS, LANES)]),
        out_specs=per_b(ATT_HEADS, HEAD_DIM),
        scratch_shapes=[pltpu.VMEM((ATT_HEADS, 1), F32), pltpu.VMEM((ATT_HEADS, 1), F32),
                        pltpu.VMEM((ATT_HEADS, K_COLS), F32)],
    )
    return pl.pallas_call(
        functools.partial(_sel_win_sample_kernel, past),
        grid_spec=grid_spec,
        out_shape=jax.ShapeDtypeStruct((bs, ATT_HEADS, HEAD_DIM), F32),
        compiler_params=_cparams(("parallel", "arbitrary")),
        name="sel_win_sample",
    )(page_table, *([cache2d] * n_pg), qbd, sel_groups, sel_new, kvs_new, win2d, kvw_new, oc, g3)


def _column_layout(d_model):
    ssm_dim = d_model // 2
    conv_ch = ssm_dim + 2 * SSM_GROUPS * SSM_STATE
    nsa = d_model - ssm_dim
    n_heads = ssm_dim // SSM_HEAD_DIM
    sizes = (ssm_dim, conv_ch, n_heads, nsa, KV_COLS, KV_COLS, KV_COLS, 3 * ATT_HEADS)
    offs = [0]
    for s in sizes:
        offs.append(offs[-1] + s)
    src = dict(zip(("z", "xbc", "dt", "q", "kvc", "kvs", "kvw", "gt"), zip(offs[:-1], sizes)))
    order = ("xbc", "z", "q", "kvc", "kvs", "kvw")
    col, start = {}, {}
    pos = 0
    for name in order:
        o, w = src[name]
        assert pos % w == 0
        col[name] = pos // w
        start[name] = pos
        pos += w
    col["small"] = pos // LANES
    start["small"] = pos
    return src, order, col, start, pos + LANES


def _layer_weights(l, p, d_model):
    src, order, col, start, n_cols = _column_layout(d_model)
    w_in = p["w_in"][l]
    n_heads = src["dt"][1]
    small = jnp.concatenate([w_in[:, src["dt"][0]:src["dt"][0] + n_heads],
                             w_in[:, src["gt"][0]:src["gt"][0] + src["gt"][1]]], axis=1)
    small = jnp.pad(small, ((0, 0), (0, LANES - small.shape[1])))
    w_in_packed = jnp.concatenate([w_in[:, src[n][0]:src[n][0] + src[n][1]] for n in order] + [small],
                                  axis=1).astype(BF16)
    ssm_dim = src["z"][1]
    rep = lambda v: jnp.repeat(v, SSM_HEAD_DIM)[None, :]
    pad_l = lambda v: jnp.pad(v, (0, LANES - v.shape[0]))[None, :]
    tile2 = lambda v: jnp.tile(v, 2)[None, :]
    cw = jnp.transpose(p["cmp_w"][l], (1, 0, 2, 3))
    zero = jnp.zeros_like(cw)
    pairs = jnp.concatenate([jnp.concatenate([cw, zero], axis=-1),
                             jnp.concatenate([zero, cw], axis=-1)], axis=-2).astype(BF16)
    pe = p["cmp_pe"][l]
    pe_rows = jnp.broadcast_to(pe[:, :, None, :], (L_CMP, 2, KV_HEADS, HEAD_DIM)).reshape(L_CMP, KV_COLS)
    head_expand = (jnp.arange(LANES)[:, None] == jnp.arange(ssm_dim)[None, :] // SSM_HEAD_DIM).astype(BF16)
    return {
        "col": col, "start": start,
        "norm1_g": p["norm1_g"][l][None, :], "w_in": w_in_packed,
        "conv_w": p["conv_w"][l], "conv_b": p["conv_b"][l][None, :],
        "dt_bias": pad_l(p["dt_bias"][l]), "a_log": pad_l(p["a_log"][l]),
        "dt_bias_e": rep(p["dt_bias"][l]), "a_log_e": rep(p["a_log"][l]),
        "dsk": rep(p["d_skip"][l]), "ssm_norm_g": p["ssm_norm_g"][l][None, :],
        "head_expand": head_expand,
        "g_q": tile2(p["q_norm_g"][l]), "g_kc": tile2(p["k_norm_g"][l, 0]),
        "g_ks": tile2(p["k_norm_g"][l, 1]), "g_kw": tile2(p["k_norm_g"][l, 2]),
        "cmp_pe_rows": pe_rows, "cmp_w_pairs": pairs,
        "w_out": p["w_out"][l].astype(BF16), "norm2_g": p["norm2_g"][l][None, :],
        "w_gu": p["w_gu"][l].astype(BF16), "w_down": p["w_down"][l].astype(BF16),
    }


def _pick(n, prefs):
    for c in prefs:
        if n % c == 0:
            return c
    return n


def _dense_tail(x, y_ssd, atts, lw, tm):
    d = x.shape[1]
    h = _out_proj(x, y_ssd, atts, lw["w_out"], tm, _pick(d, (512,)))
    d_ff = lw["w_down"].shape[0]
    act = _ffn_up(h, lw["norm2_g"], lw["w_gu"], tm, _pick(d_ff, (512, 256, 128)))
    return _ffn_down(act, lw["w_down"], h, tm, _pick(d, (512,)))


def kernel(x_prompt, x_sample, cache_cmp_kv, cache_slc_kv, state_win_kv, state_ssm, state_conv, page_table,
           norm1_g, w_in, conv_w, conv_b, dt_bias, a_log, d_skip, ssm_norm_g, q_norm_g, k_norm_g,
           cmp_pe, cmp_w, w_out, norm2_g, w_gu, w_down):
    params = dict(norm1_g=norm1_g, w_in=w_in, conv_w=conv_w, conv_b=conv_b, dt_bias=dt_bias, a_log=a_log,
                  d_skip=d_skip, ssm_norm_g=ssm_norm_g, q_norm_g=q_norm_g, k_norm_g=k_norm_g, cmp_pe=cmp_pe,
                  cmp_w=cmp_w, w_out=w_out, norm2_g=norm2_g, w_gu=w_gu, w_down=w_down)
    bp, t, d = x_prompt.shape
    bs, dec_t, _ = x_sample.shape
    depth = w_in.shape[0]
    n_pool, pg = cache_cmp_kv.shape[1:3]
    n_pages = page_table.shape[1]
    past = n_pages * pg
    w_buf = state_win_kv.shape[2]
    assert bp == 1 and dec_t == 1
    assert t % (PAGES_PER_STEP * pg) == 0 and past % (PAGES_PER_STEP * pg) == 0 and t // L_SEL <= LANES
    ssm_dim = d // 2
    n_heads = ssm_dim // SSM_HEAD_DIM
    kv_shape = (2, KV_HEADS, HEAD_DIM)

    xp = x_prompt[0]
    xs = x_sample[:, 0]
    outs = {k: [] for k in ("cmp_p", "cmp_s", "slc_p", "slc_s", "win_p", "win_s", "ssm_p", "ssm_s",
                            "conv_p", "conv_s")}
    tm_p = _pick(t, (512, 256, 128))
    for l in range(depth):
        lw = _layer_weights(l, params, d)
        col, start = lw["col"], lw["start"]
        n_cols = lw["w_in"].shape[1]
        tn_in = _pick(n_cols, (1152, 640, 384, 128))

        u = _norm_matmul(xp, lw["norm1_g"], lw["w_in"], tm_p, tn_in)
        qn, kvs_n, kvw_n, k_s, v_s, k_w, v_w, gates_h = _prep(u, col, lw["g_q"], lw["g_ks"], lw["g_kw"],
                                                               _pick(t, (256, 128)))
        y_ssd, h_fin = _ssd_prompt(u, col, lw)
        kvc = u[:, start["kvc"]:start["kvc"] + KV_COLS]
        xbc_tail = u[t - (CONV_K - 1):, start["xbc"]:start["xbc"] + lw["conv_w"].shape[1]]
        ck, cv = _compress_prompt(jnp.pad(kvc, ((0, CMP_STRIDE), (0, 0))), lw, PAGES_PER_STEP * pg)
        o_c, sel = _cmp_prompt(qn, ck, cv, gates_h, t)
        o_s = _sel_prompt(qn, k_s, v_s, sel, gates_h, t)
        o_w = _win_prompt(qn, k_w, v_w, gates_h, t)
        xp = _dense_tail(xp, y_ssd, [o_c, o_s, o_w], lw, tm_p)
        outs["cmp_p"].append(kvc.reshape((1, t) + kv_shape))
        outs["slc_p"].append(kvs_n.reshape((1, t) + kv_shape))
        outs["win_p"].append(kvw_n[t - min(WINDOW, t):].reshape((1, min(WINDOW, t)) + kv_shape))
        outs["ssm_p"].append(h_fin.reshape(1, n_heads, SSM_HEAD_DIM, SSM_STATE))
        outs["conv_p"].append(xbc_tail[None])

        us = _norm_matmul(xs, lw["norm1_g"], lw["w_in"], bs, tn_in)
        qn_s, kvs_s, kvw_s, _, _, _, _, gates_s = _prep(us, col, lw["g_q"], lw["g_ks"], lw["g_kw"], bs)
        conv_t = jnp.transpose(state_conv[l], (1, 0, 2))
        y_ssd_s, h_new = _ssd_sample(us, col, conv_t, state_ssm[l].reshape(bs, ssm_dim, SSM_STATE), lw)
        xbc_s = us[:, start["xbc"]:start["xbc"] + lw["conv_w"].shape[1]]
        kvc_s = us[:, start["kvc"]:start["kvc"] + KV_COLS]
        cmp2d = cache_cmp_kv[l].reshape(n_pool, pg, KV_COLS)
        slc2d = cache_slc_kv[l].reshape(n_pool, pg, KV_COLS)
        win2d = state_win_kv[l].reshape(bs, w_buf, KV_COLS)
        ck_s, cv_s = _compress_sample(cmp2d, page_table, lw)
        q3 = qn_s.reshape(bs, ATT_HEADS, HEAD_DIM)
        g3 = jnp.transpose(gates_s[:, :, :3 * GQA].reshape(KV_HEADS, bs, GQA, 3), (1, 0, 2, 3))
        g3 = jnp.pad(g3.reshape(bs, ATT_HEADS, 3), ((0, 0), (0, 0), (0, LANES - 3)))
        oc_s, sel_s = _cmp_sample(q3, ck_s, cv_s, g3, past)
        sel16 = jnp.repeat(sel_s[:, :KV_HEADS], GQA, axis=1)
        bps = PAGES_PER_STEP * pg // L_SEL
        sel_groups = jnp.transpose(sel16[:, :, :past // L_SEL].reshape(bs, ATT_HEADS, past // L_SEL // bps, bps),
                                   (0, 2, 1, 3))
        sel_new = jnp.pad(sel16[:, :, past // L_SEL:past // L_SEL + 1], ((0, 0), (0, 0), (0, LANES - 1)))
        head_of_col = jnp.arange(K_COLS) // HEAD_DIM
        qbd = jnp.where(head_of_col[None, None, :] == (jnp.arange(ATT_HEADS) // GQA)[None, :, None],
                        jnp.tile(q3, (1, 1, KV_HEADS)), jnp.zeros((), BF16))
        o_att_s = _sel_win_sample(slc2d, page_table, qbd, sel_groups, sel_new, kvs_s[:, None, :], win2d,
                                  kvw_s[:, None, :], oc_s, g3, past)
        xs = _dense_tail(xs, y_ssd_s.astype(BF16), [o_att_s.reshape(bs, ATT_HEADS * HEAD_DIM)], lw, bs)
        outs["cmp_s"].append(kvc_s.reshape((bs, 1) + kv_shape))
        outs["slc_s"].append(kvs_s.reshape((bs, 1) + kv_shape))
        outs["win_s"].append(jnp.concatenate([win2d, kvw_s[:, None, :]], axis=1)[:, -w_buf:]
                             .reshape((bs, w_buf) + kv_shape))
        outs["ssm_s"].append(h_new.reshape(bs, n_heads, SSM_HEAD_DIM, SSM_STATE))
        outs["conv_s"].append(jnp.concatenate([state_conv[l], xbc_s[:, None, :]], axis=1)[:, -(CONV_K - 1):])

    st = lambda k: jnp.stack(outs[k])
    return (xp[None], xs[:, None], st("cmp_p"), st("cmp_s"), st("slc_p"), st("slc_s"), st("win_p"),
            st("win_s"), st("ssm_p"), st("ssm_s"), st("conv_p"), st("conv_s"))
```

```python
import functools

import jax
import jax.numpy as jnp
from jax import lax
from jax.experimental import pallas as pl
from jax.experimental.pallas import tpu as pltpu

F32 = jnp.float32
BF16 = jnp.bfloat16

RMS_EPS = 1e-6
SSM_HEAD_DIM = 64
SSM_GROUPS = 4
SSM_STATE = 128
CONV_K = 4
SSD_CHUNK = 128
HEAD_DIM = 64
KV_HEADS = 4
GQA = 4
ATT_HEADS = KV_HEADS * GQA
CMP_STRIDE = 16
L_CMP = 32
L_SEL = 64
N_SEL = 16
WINDOW = 512
SEL_FORCE = 1e9
KV_COLS = 2 * KV_HEADS * HEAD_DIM
K_COLS = KV_HEADS * HEAD_DIM

LANES = 128
Q_TILE = 128
SEL_Q_TILE = 256
SEL_KEY_TILE = 512
PAGES_PER_STEP = 16
VMEM_LIMIT = 56 * 1024 * 1024
NEG_BIG = -1e30


def _cparams(sem):
    return pltpu.CompilerParams(dimension_semantics=sem, vmem_limit_bytes=VMEM_LIMIT)


def _sigmoid(x):
    return 1.0 / (1.0 + jnp.exp(-x))


def _silu(x):
    return x * _sigmoid(x)


def _softplus(x):
    return jnp.maximum(x, 0.0) + jnp.log(1.0 + jnp.exp(-jnp.abs(x)))


def _split3(x):
    hi = x.astype(BF16)
    r = x - hi.astype(F32)
    mid = r.astype(BF16)
    lo = (r - mid.astype(F32)).astype(BF16)
    return hi, mid, lo


def _dot(a, b):
    return jnp.dot(a, b, preferred_element_type=F32)


def _dot_nt(a, b):
    return lax.dot_general(a, b, (((1,), (1,)), ((), ())), preferred_element_type=F32)


def _dot_tn(a, b):
    return lax.dot_general(a, b, (((0,), (0,)), ((), ())), preferred_element_type=F32)


def _dot3(x, m):
    hi, mid, lo = _split3(x)
    return _dot(hi, m) + _dot(mid, m) + _dot(lo, m)


def _head_rms(x, g2):
    lane = lax.broadcasted_iota(jnp.int32, (1, LANES), 1)
    lo = lane < HEAD_DIM
    outs = []
    for j in range(x.shape[1] // LANES):
        xt = x[:, j * LANES:(j + 1) * LANES]
        sq = xt * xt
        s_lo = jnp.sum(jnp.where(lo, sq, 0.0), axis=-1, keepdims=True)
        s_hi = jnp.sum(jnp.where(lo, 0.0, sq), axis=-1, keepdims=True)
        ms = jnp.where(lo, s_lo, s_hi) * (1.0 / HEAD_DIM)
        outs.append(xt * lax.rsqrt(ms + RMS_EPS) * g2)
    return outs[0] if len(outs) == 1 else jnp.concatenate(outs, axis=-1)


def _masked_softmax(s, mask):
    s = jnp.where(mask, s, -jnp.inf)
    m = jnp.max(s, axis=-1, keepdims=True)
    m = jnp.where(m == -jnp.inf, 0.0, m)
    e = jnp.where(mask, jnp.exp(s - m), 0.0)
    return e / jnp.maximum(jnp.sum(e, axis=-1, keepdims=True), 1e-30)


def _top_k_mask(score, k):
    lane = lax.broadcasted_iota(jnp.int32, score.shape, 1).astype(F32)
    sel = jnp.zeros(score.shape, F32)
    removed = -3.0
    for _ in range(k):
        m = jnp.max(score, axis=-1, keepdims=True)
        idx = jnp.min(jnp.where(score == m, lane, 1e9), axis=-1, keepdims=True)
        hit = lane == idx
        sel = jnp.where(hit, 1.0, sel)
        score = jnp.where(hit, removed, score)
    return sel


def _top_k_mask_t(score_t, k, n_cand):
    sub = 8
    n_rows, cols = score_t.shape
    row_in_blk = lax.broadcasted_iota(jnp.int32, (sub, cols), 0)
    n_blk = -(-n_cand // sub)
    blocks = [score_t[r * sub:(r + 1) * sub, :] for r in range(n_blk)]
    counts = [jnp.zeros((sub, cols), F32) for _ in range(n_blk)]
    for c in range(n_cand):
        row = score_t[c:c + 1, :]
        for r in range(n_blk):
            gt = jnp.where(row > blocks[r], 1.0, 0.0)
            ge = jnp.where(row >= blocks[r], 1.0, 0.0)
            if r * sub + sub - 1 < c:
                beats = gt
            elif r * sub > c:
                beats = ge
            else:
                beats = jnp.where(row_in_blk + r * sub > c, ge, gt)
            counts[r] = counts[r] + beats
    picked = [jnp.where(cnt < k, 1.0, 0.0) for cnt in counts]
    if n_blk * sub < n_rows:
        picked.append(jnp.zeros((n_rows - n_blk * sub, cols), F32))
    return jnp.concatenate(picked, axis=0)


def _norm_matmul_kernel(x_ref, g_ref, w_ref, o_ref, xn_ref):
    @pl.when(pl.program_id(1) == 0)
    def _():
        x = x_ref[...]
        ms = jnp.mean(x * x, axis=-1, keepdims=True)
        xn_ref[...] = (x * lax.rsqrt(ms + RMS_EPS) * g_ref[...]).astype(BF16)

    o_ref[...] = _dot(xn_ref[...], w_ref[...]).astype(o_ref.dtype)


def _norm_matmul(x, g, w, tm, tn):
    rows, d = x.shape
    n = w.shape[1]
    return pl.pallas_call(
        _norm_matmul_kernel,
        grid=(rows // tm, n // tn),
        in_specs=[pl.BlockSpec((tm, d), lambda i, j: (i, 0)),
                  pl.BlockSpec((1, d), lambda i, j: (0, 0)),
                  pl.BlockSpec((d, tn), lambda i, j: (0, j))],
        out_specs=pl.BlockSpec((tm, tn), lambda i, j: (i, j)),
        out_shape=jax.ShapeDtypeStruct((rows, n), F32),
        scratch_shapes=[pltpu.VMEM((tm, d), BF16)],
        compiler_params=_cparams(("parallel", "arbitrary")),
        name="in_proj",
    )(x, g, w)


def _ffn_up_kernel(x_ref, g_ref, wg_ref, wv_ref, o_ref, xn_ref):
    @pl.when(pl.program_id(1) == 0)
    def _():
        x = x_ref[...]
        ms = jnp.mean(x * x, axis=-1, keepdims=True)
        xn_ref[...] = (x * lax.rsqrt(ms + RMS_EPS) * g_ref[...]).astype(BF16)

    xn = xn_ref[...]
    gate = _dot(xn, wg_ref[...])
    val = _dot(xn, wv_ref[...])
    o_ref[...] = (_silu(gate) * val).astype(o_ref.dtype)


def _ffn_up(x, g, w_gu, tm, tn):
    rows, d = x.shape
    d_ff = w_gu.shape[1] // 2
    nj = d_ff // tn
    return pl.pallas_call(
        _ffn_up_kernel,
        grid=(rows // tm, nj),
        in_specs=[pl.BlockSpec((tm, d), lambda i, j: (i, 0)),
                  pl.BlockSpec((1, d), lambda i, j: (0, 0)),
                  pl.BlockSpec((d, tn), lambda i, j: (0, j)),
                  pl.BlockSpec((d, tn), lambda i, j: (0, j + nj))],
        out_specs=pl.BlockSpec((tm, tn), lambda i, j: (i, j)),
        out_shape=jax.ShapeDtypeStruct((rows, d_ff), BF16),
        scratch_shapes=[pltpu.VMEM((tm, d), BF16)],
        compiler_params=_cparams(("parallel", "arbitrary")),
        name="ffn_up",
    )(x, g, w_gu, w_gu)


def _ffn_down_kernel(a_ref, w_ref, h_ref, o_ref):
    o_ref[...] = h_ref[...] + _dot(a_ref[...], w_ref[...])


def _ffn_down(act, w_down, h, tm, tn):
    rows, d_ff = act.shape
    d = w_down.shape[1]
    return pl.pallas_call(
        _ffn_down_kernel,
        grid=(rows // tm, d // tn),
        in_specs=[pl.BlockSpec((tm, d_ff), lambda i, j: (i, 0)),
                  pl.BlockSpec((d_ff, tn), lambda i, j: (0, j)),
                  pl.BlockSpec((tm, tn), lambda i, j: (i, j))],
        out_specs=pl.BlockSpec((tm, tn), lambda i, j: (i, j)),
        out_shape=jax.ShapeDtypeStruct((rows, d), F32),
        compiler_params=_cparams(("parallel", "arbitrary")),
        name="ffn_down",
    )(act, w_down, h)


def _out_proj_kernel(n_att, x_ref, y_ref, *refs):
    att_refs = refs[:n_att]
    w1_ref, w2_ref, o_ref = refs[n_att:]
    att = att_refs[0][...]
    for r in att_refs[1:]:
        att = att + r[...]
    o_ref[...] = (x_ref[...] + _dot(y_ref[...], w1_ref[...])
                  + _dot(att.astype(BF16), w2_ref[...]))


def _out_proj(x, y_ssd, atts, w_out, tm, tn):
    rows, d = x.shape
    half = y_ssd.shape[1]
    n_att = len(atts)
    return pl.pallas_call(
        functools.partial(_out_proj_kernel, n_att),
        grid=(rows // tm, d // tn),
        in_specs=([pl.BlockSpec((tm, tn), lambda i, j: (i, j)),
                   pl.BlockSpec((tm, half), lambda i, j: (i, 0))]
                  + [pl.BlockSpec((tm, half), lambda i, j: (i, 0)) for _ in atts]
                  + [pl.BlockSpec((half, tn), lambda i, j: (0, j)),
                     pl.BlockSpec((half, tn), lambda i, j: (1, j))]),
        out_specs=pl.BlockSpec((tm, tn), lambda i, j: (i, j)),
        out_shape=jax.ShapeDtypeStruct((rows, d), F32),
        compiler_params=_cparams(("parallel", "arbitrary")),
        name="out_proj",
    )(x, y_ssd, *atts, w_out, w_out)


def _prep_kernel(q_ref, ks_ref, kw_ref, sm_ref, gq_ref, gks_ref, gkw_ref,
                 qn_ref, kvs_ref, kvw_ref, khs_ref, vhs_ref, khw_ref, vhw_ref, gates_ref):
    qn = _head_rms(q_ref[...], gq_ref[...]) * (HEAD_DIM ** -0.5)
    qn_ref[...] = qn.astype(BF16)
    for src, g_ref, full_ref, kh_ref, vh_ref in ((ks_ref, gks_ref, kvs_ref, khs_ref, vhs_ref),
                                                 (kw_ref, gkw_ref, kvw_ref, khw_ref, vhw_ref)):
        kv = src[...]
        kn = _head_rms(kv[:, :K_COLS], g_ref[...])
        v = kv[:, K_COLS:]
        full_ref[:, :K_COLS] = kn
        full_ref[:, K_COLS:] = v
        for h in range(KV_HEADS):
            kh_ref[h] = kn[:, h * HEAD_DIM:(h + 1) * HEAD_DIM].astype(BF16)
            vh_ref[h] = v[:, h * HEAD_DIM:(h + 1) * HEAD_DIM].astype(BF16)
    sig = _sigmoid(sm_ref[...])
    n_dt = LANES // 8
    for h in range(KV_HEADS):
        gates_ref[h] = pltpu.roll(sig, LANES - n_dt - 3 * GQA * h, axis=1)


def _prep(u, col, g_q, g_ks, g_kw, tm):
    rows = u.shape[0]
    nsa = ATT_HEADS * HEAD_DIM
    row_spec = lambda w, c: pl.BlockSpec((tm, w), lambda i, c=c: (i, c))
    head_out = jax.ShapeDtypeStruct((KV_HEADS, rows, HEAD_DIM), BF16)
    head_spec = pl.BlockSpec((KV_HEADS, tm, HEAD_DIM), lambda i: (0, i, 0))
    vec = pl.BlockSpec((1, LANES), lambda i: (0, 0))
    return pl.pallas_call(
        _prep_kernel,
        grid=(rows // tm,),
        in_specs=[row_spec(nsa, col["q"]), row_spec(KV_COLS, col["kvs"]),
                  row_spec(KV_COLS, col["kvw"]), row_spec(LANES, col["small"]), vec, vec, vec],
        out_specs=[pl.BlockSpec((tm, nsa), lambda i: (i, 0)),
                   pl.BlockSpec((tm, KV_COLS), lambda i: (i, 0)),
                   pl.BlockSpec((tm, KV_COLS), lambda i: (i, 0)),
                   head_spec, head_spec, head_spec, head_spec,
                   pl.BlockSpec((KV_HEADS, tm, LANES), lambda i: (0, i, 0))],
        out_shape=[jax.ShapeDtypeStruct((rows, nsa), BF16),
                   jax.ShapeDtypeStruct((rows, KV_COLS), F32),
                   jax.ShapeDtypeStruct((rows, KV_COLS), F32),
                   head_out, head_out, head_out, head_out,
                   jax.ShapeDtypeStruct((KV_HEADS, rows, LANES), F32)],
        compiler_params=_cparams(("parallel",)),
        name="head_prep",
    )(u, u, u, u, g_q, g_ks, g_kw)


def _gated_norm(y, xs, z, dsk, gn):
    y = (y + dsk * xs) * _silu(z)
    ms = jnp.mean(y * y, axis=-1, keepdims=True)
    return y * lax.rsqrt(ms + RMS_EPS) * gn


def _ssd_prompt_kernel(xbc_ref, z_ref, sm_ref, cw_ref, cb_ref, dtb_ref, alog_ref, dsk_ref, gn_ref,
                       y_ref, hout_ref, xb_ref, st_ref):
    c = pl.program_id(0)
    q = SSD_CHUNK
    ssm_dim = z_ref.shape[1]
    gw = SSM_STATE
    pad = 8

    @pl.when(c == 0)
    def _():
        xb_ref[0:pad, :] = jnp.zeros((pad, xb_ref.shape[1]), F32)
        st_ref[...] = jnp.zeros(st_ref.shape, F32)

    xb_ref[pad:pad + q, :] = xbc_ref[...]
    conv = cb_ref[...]
    for k in range(CONV_K):
        conv = conv + cw_ref[k:k + 1, :] * xb_ref[pl.ds(pad - (CONV_K - 1) + k, q), :]
    xb_ref[0:pad, :] = xb_ref[q:q + pad, :]
    act = _silu(conv)
    xs = act[:, :ssm_dim]
    bm = act[:, ssm_dim:ssm_dim + SSM_GROUPS * gw].astype(BF16)
    cm = act[:, ssm_dim + SSM_GROUPS * gw:].astype(BF16)

    lane = lax.broadcasted_iota(jnp.int32, (1, LANES), 1)
    n_heads = ssm_dim // SSM_HEAD_DIM
    dt = jnp.where(lane < n_heads, _softplus(sm_ref[...] + dtb_ref[...]), 0.0)
    a = -jnp.exp(alog_ref[...])
    ri = lax.broadcasted_iota(jnp.int32, (q, q), 0)
    ci = lax.broadcasted_iota(jnp.int32, (q, q), 1)
    causal = ri >= ci
    tri = jnp.where(causal, 1.0, 0.0).astype(BF16)
    da_hi, da_mid, da_lo = _split3(dt * a)
    acum = _dot(tri, da_hi) + _dot(tri, da_mid) + _dot(tri, da_lo)
    acum_t = acum.T
    dt_t = dt.T
    e_acum = jnp.exp(acum)
    a_last = acum[q - 1:q, :]
    w_end = dt * jnp.exp(a_last - acum)
    e_last = jnp.exp(a_last)
    lo = lane < SSM_HEAD_DIM
    row_lo = lax.broadcasted_iota(jnp.int32, (LANES, 1), 0) < SSM_HEAD_DIM

    ys = []
    heads_per_group = n_heads // SSM_GROUPS
    for pr in range(n_heads // 2):
        h0, h1 = 2 * pr, 2 * pr + 1
        g = h0 // heads_per_group
        cg = cm[:, g * gw:(g + 1) * gw]
        bg = bm[:, g * gw:(g + 1) * gw]
        cb = _dot_nt(cg, bg)
        x2 = xs[:, pr * LANES:(pr + 1) * LANES]
        x2b = x2.astype(BF16)
        yd = []
        for h in (h0, h1):
            seg = acum[:, h:h + 1] - acum_t[h:h + 1, :]
            dec = jnp.where(causal, jnp.exp(jnp.where(causal, seg, 0.0)), 0.0)
            m = (cb * dec * dt_t[h:h + 1, :]).astype(BF16)
            yd.append(_dot(m, x2b))
        y_diag = jnp.where(lo, yd[0], yd[1])
        sp = st_ref[pr * LANES:(pr + 1) * LANES, :]
        ea = jnp.where(lo, e_acum[:, h0:h0 + 1], e_acum[:, h1:h1 + 1])
        y_off = _dot_nt(cg, sp.astype(BF16)) * ea
        w2 = jnp.where(lo, w_end[:, h0:h0 + 1], w_end[:, h1:h1 + 1])
        xw_t = (x2 * w2).T.astype(BF16)
        cd = jnp.where(row_lo, e_last[:, h0:h0 + 1], e_last[:, h1:h1 + 1])
        st_ref[pr * LANES:(pr + 1) * LANES, :] = sp * cd + _dot(xw_t, bg)
        ys.append(y_diag + y_off)
    y = jnp.concatenate(ys, axis=-1)
    y_ref[...] = _gated_norm(y, xs, z_ref[...], dsk_ref[...], gn_ref[...]).astype(y_ref.dtype)

    @pl.when(c == pl.num_programs(0) - 1)
    def _():
        hout_ref[...] = st_ref[...]


def _ssd_prompt(u, col, lw):
    t = u.shape[0]
    ssm_dim = lw["dsk"].shape[1]
    conv_ch = lw["conv_w"].shape[1]
    q = SSD_CHUNK
    full = lambda r, w: pl.BlockSpec((r, w), lambda c: (0, 0))
    return pl.pallas_call(
        _ssd_prompt_kernel,
        grid=(t // q,),
        in_specs=[pl.BlockSpec((q, conv_ch), lambda c: (c, col["xbc"])),
                  pl.BlockSpec((q, ssm_dim), lambda c: (c, col["z"])),
                  pl.BlockSpec((q, LANES), lambda c: (c, col["small"])),
                  full(CONV_K, conv_ch), full(1, conv_ch), full(1, LANES), full(1, LANES),
                  full(1, ssm_dim), full(1, ssm_dim)],
        out_specs=[pl.BlockSpec((q, ssm_dim), lambda c: (c, 0)),
                   pl.BlockSpec((ssm_dim, SSM_STATE), lambda c: (0, 0))],
        out_shape=[jax.ShapeDtypeStruct((t, ssm_dim), BF16),
                   jax.ShapeDtypeStruct((ssm_dim, SSM_STATE), F32)],
        scratch_shapes=[pltpu.VMEM((q + 8, conv_ch), F32),
                        pltpu.VMEM((ssm_dim, SSM_STATE), F32)],
        compiler_params=_cparams(("arbitrary",)),
        name="ssd_prompt",
    )(u, u, u, lw["conv_w"], lw["conv_b"], lw["dt_bias"], lw["a_log"], lw["dsk"], lw["ssm_norm_g"])


def _ssd_sample_kernel(xbc_ref, z_ref, sm_ref, cbuf_ref, h_ref, cw_ref, cb_ref, dtb_ref, alog_ref,
                       dsk_ref, gn_ref, exp_ref, y_ref, hout_ref):
    b = pl.program_id(0)
    ssm_dim = z_ref.shape[1]
    gw = SSM_STATE
    n_heads = ssm_dim // SSM_HEAD_DIM
    conv = cb_ref[...] + cw_ref[CONV_K - 1:CONV_K, :] * xbc_ref[pl.ds(b, 1), :]
    for k in range(CONV_K - 1):
        conv = conv + cw_ref[k:k + 1, :] * cbuf_ref[k, pl.ds(b, 1), :]
    act = _silu(conv)
    xs = act[:, :ssm_dim]
    lane = lax.broadcasted_iota(jnp.int32, (1, LANES), 1)
    dt_raw = jnp.broadcast_to(jnp.where(lane < n_heads, sm_ref[pl.ds(b, 1), :], 0.0), (8, LANES))
    dt = _softplus(_dot3(dt_raw, exp_ref[...])[0:1, :] + dtb_ref[...])
    dec = jnp.exp(dt * (-jnp.exp(alog_ref[...])))
    dtx = dt * xs
    row8 = lax.broadcasted_iota(jnp.int32, (8, 1), 0)

    def rows8(pieces):
        out = jnp.zeros((8, pieces[0].shape[1]), F32)
        for r, p in enumerate(pieces):
            out = jnp.where(row8 == r, p.astype(F32), out)
        return out.astype(BF16)

    dec8 = rows8(_split3(dec))
    dtx8 = rows8(_split3(dtx))
    ones8 = jnp.where(row8 < 3, 1.0, 0.0).astype(BF16) * jnp.ones((1, gw), BF16)
    gh = (n_heads // SSM_GROUPS) * SSM_HEAD_DIM
    ys = []
    for g in range(SSM_GROUPS):
        bg = act[:, ssm_dim + g * gw:ssm_dim + (g + 1) * gw].astype(BF16)
        cg = act[:, ssm_dim + (SSM_GROUPS + g) * gw:ssm_dim + (SSM_GROUPS + g + 1) * gw].astype(BF16)
        b8 = jnp.where(row8 < 3, 1.0, 0.0).astype(BF16) * bg
        c8 = jnp.where(row8 < 1, 1.0, 0.0).astype(BF16) * cg
        dec_col = _dot_tn(dec8[:, g * gh:(g + 1) * gh], ones8)
        dbx = _dot_tn(dtx8[:, g * gh:(g + 1) * gh], b8)
        s_new = h_ref[g * gh:(g + 1) * gh, :] * dec_col + dbx
        hout_ref[g * gh:(g + 1) * gh, :] = s_new
        ys.append(_dot_nt(c8, s_new.astype(BF16))[0:1, :])
    y = jnp.concatenate(ys, axis=-1)
    y_ref[pl.ds(b, 1), :] = _gated_norm(y, xs, z_ref[pl.ds(b, 1), :], dsk_ref[...], gn_ref[...])


def _ssd_sample(u, col, conv_t, h0, lw):
    bs = u.shape[0]
    ssm_dim = lw["dsk"].shape[1]
    conv_ch = lw["conv_w"].shape[1]
    full = lambda r, w: pl.BlockSpec((r, w), lambda b: (0, 0))
    return pl.pallas_call(
        _ssd_sample_kernel,
        grid=(bs,),
        in_specs=[pl.BlockSpec((bs, conv_ch), lambda b: (0, col["xbc"])),
                  pl.BlockSpec((bs, ssm_dim), lambda b: (0, col["z"])),
                  pl.BlockSpec((bs, LANES), lambda b: (0, col["small"])),
                  pl.BlockSpec((CONV_K - 1, bs, conv_ch), lambda b: (0, 0, 0)),
                  pl.BlockSpec((None, ssm_dim, SSM_STATE), lambda b: (b, 0, 0)),
                  full(CONV_K, conv_ch), full(1, conv_ch), full(1, ssm_dim), full(1, ssm_dim),
                  full(1, ssm_dim), full(1, ssm_dim), full(LANES, ssm_dim)],
        out_specs=[pl.BlockSpec((bs, ssm_dim), lambda b: (0, 0)),
                   pl.BlockSpec((None, ssm_dim, SSM_STATE), lambda b: (b, 0, 0))],
        out_shape=[jax.ShapeDtypeStruct((bs, ssm_dim), F32),
                   jax.ShapeDtypeStruct((bs, ssm_dim, SSM_STATE), F32)],
        compiler_params=_cparams(("arbitrary",)),
        name="ssd_sample",
    )(u, u, u, conv_t, h0, lw["conv_w"], lw["conv_b"], lw["dt_bias_e"], lw["a_log_e"], lw["dsk"],
      lw["ssm_norm_g"], lw["head_expand"])


def _compress_rows(buf_ref, xcat_ref, pe_ref, w_ref, gk_ref):
    n_blocks = xcat_ref.shape[0]
    acc = []
    for p in range(KV_COLS // LANES):
        for l in range(L_CMP):
            xl = buf_ref[p, pl.ds(l, n_blocks, stride=CMP_STRIDE), :] + pe_ref[l:l + 1, p * LANES:(p + 1) * LANES]
            xcat_ref[:, l * LANES:(l + 1) * LANES] = xl.astype(BF16)
        acc.append(_dot(xcat_ref[...], w_ref[p // 2]))
    ck = _head_rms(jnp.concatenate(acc[:2], axis=-1), gk_ref[...])
    cv = jnp.concatenate(acc[2:], axis=-1)
    return ck, cv


def _store_heads(ck, cv, ck_ref, cv_ref):
    for h in range(KV_HEADS):
        ck_ref[h] = ck[:, h * HEAD_DIM:(h + 1) * HEAD_DIM].astype(BF16)
        cv_ref[h] = cv[:, h * HEAD_DIM:(h + 1) * HEAD_DIM].astype(BF16)


def _compress_prompt_kernel(main_ref, nxt_ref, pe_ref, w_ref, gk_ref, ck_ref, cv_ref, buf_ref, xcat_ref):
    rows = main_ref.shape[0]
    for p in range(KV_COLS // LANES):
        buf_ref[p, 0:rows, :] = main_ref[:, p * LANES:(p + 1) * LANES]
        buf_ref[p, rows:rows + CMP_STRIDE, :] = nxt_ref[:, p * LANES:(p + 1) * LANES]
    ck, cv = _compress_rows(buf_ref, xcat_ref, pe_ref, w_ref, gk_ref)
    _store_heads(ck, cv, ck_ref, cv_ref)


def _compress_scratch(rows):
    return [pltpu.VMEM((KV_COLS // LANES, rows + CMP_STRIDE, LANES), F32),
            pltpu.VMEM((rows // CMP_STRIDE, L_CMP * LANES), BF16)]


def _compress_prompt(kvc_pad, lw, rows_per_step):
    t = kvc_pad.shape[0] - CMP_STRIDE
    nb = rows_per_step // CMP_STRIDE
    out = jax.ShapeDtypeStruct((KV_HEADS, t // CMP_STRIDE, HEAD_DIM), BF16)
    out_spec = pl.BlockSpec((KV_HEADS, nb, HEAD_DIM), lambda i: (0, i, 0))
    return pl.pallas_call(
        _compress_prompt_kernel,
        grid=(t // rows_per_step,),
        in_specs=[pl.BlockSpec((rows_per_step, KV_COLS), lambda i: (i, 0)),
                  pl.BlockSpec((CMP_STRIDE, KV_COLS), lambda i: ((i + 1) * nb, 0)),
                  pl.BlockSpec((L_CMP, KV_COLS), lambda i: (0, 0)),
                  pl.BlockSpec((2, L_CMP * LANES, LANES), lambda i: (0, 0, 0)),
                  pl.BlockSpec((1, LANES), lambda i: (0, 0))],
        out_specs=[out_spec, out_spec],
        out_shape=[out, out],
        scratch_shapes=_compress_scratch(rows_per_step),
        compiler_params=_cparams(("parallel",)),
        name="compress_prompt",
    )(kvc_pad, kvc_pad, lw["cmp_pe_rows"], lw["cmp_w_cat"], lw["g_kc"])


def _compress_sample_kernel(n_pg, pt_ref, *refs):
    page_refs = refs[:n_pg]
    nxt_ref, pe_ref, w_ref, gk_ref, ck_ref, cv_ref, buf_ref, xcat_ref = refs[n_pg:]
    pg = page_refs[0].shape[0]
    for p in range(KV_COLS // LANES):
        for k in range(n_pg):
            buf_ref[p, k * pg:(k + 1) * pg, :] = page_refs[k][:, p * LANES:(p + 1) * LANES]
        buf_ref[p, n_pg * pg:n_pg * pg + CMP_STRIDE, :] = nxt_ref[:, p * LANES:(p + 1) * LANES]
    ck, cv = _compress_rows(buf_ref, xcat_ref, pe_ref, w_ref, gk_ref)
    _store_heads(ck, cv, ck_ref, cv_ref)


def _compress_sample(cache2d, page_base, page_table, lw):
    bs, n_pages = page_table.shape
    pg = cache2d.shape[1]
    n_pg = _pick(n_pages, (2 * PAGES_PER_STEP, PAGES_PER_STEP))
    nb = n_pg * pg // CMP_STRIDE
    past = n_pages * pg

    def page_map(b, j, pt, k):
        return (page_base + pt[b, j * n_pg + k], 0, 0)

    def next_map(b, j, pt):
        return (page_base + pt[b, jnp.minimum((j + 1) * n_pg, n_pages - 1)], 0, 0)

    out = jax.ShapeDtypeStruct((bs, KV_HEADS, past // CMP_STRIDE, HEAD_DIM), BF16)
    out_spec = pl.BlockSpec((None, KV_HEADS, nb, HEAD_DIM), lambda b, j, pt: (b, 0, j, 0))
    grid_spec = pltpu.PrefetchScalarGridSpec(
        num_scalar_prefetch=1,
        grid=(bs, n_pages // n_pg),
        in_specs=([pl.BlockSpec((None, pg, KV_COLS), functools.partial(page_map, k=k)) for k in range(n_pg)]
                  + [pl.BlockSpec((None, CMP_STRIDE, KV_COLS), next_map),
                     pl.BlockSpec((L_CMP, KV_COLS), lambda b, j, pt: (0, 0)),
                     pl.BlockSpec((2, L_CMP * LANES, LANES), lambda b, j, pt: (0, 0, 0)),
                     pl.BlockSpec((1, LANES), lambda b, j, pt: (0, 0))]),
        out_specs=[out_spec, out_spec],
        scratch_shapes=_compress_scratch(n_pg * pg),
    )
    return pl.pallas_call(
        functools.partial(_compress_sample_kernel, n_pg),
        grid_spec=grid_spec,
        out_shape=[out, out],
        compiler_params=_cparams(("parallel", "parallel")),
        name="compress_sample",
    )(page_table, *([cache2d] * (n_pg + 1)), lw["cmp_pe_rows"], lw["cmp_w_cat"], lw["g_kc"])


def _cover_matrix(nc, n_sel_blocks, width):
    n = lax.broadcasted_iota(jnp.int32, (nc, width), 0) * CMP_STRIDE
    s = lax.broadcasted_iota(jnp.int32, (nc, width), 1)
    hit = (n < s * L_SEL + L_SEL) & (n + L_CMP > s * L_SEL) & (s < n_sel_blocks)
    return jnp.where(hit, 1.0, 0.0).astype(BF16)


def _block_scores(imp, q_pos, n_sel_blocks):
    blk = lax.broadcasted_iota(jnp.int32, imp.shape, 1)
    valid = blk * L_SEL <= q_pos
    cur = q_pos // L_SEL
    forced = (blk == 0) | (blk == cur) | (blk == cur - 1)
    score = jnp.where(valid & forced, SEL_FORCE, jnp.where(valid, imp, -1.0))
    return jnp.where(blk < n_sel_blocks, score, -2.0)


def _cmp_prompt_kernel(q_ref, ck_ref, cv_ref, gates_ref, oc_ref, sel_ref):
    i = pl.program_id(0)
    tq = q_ref.shape[0]
    nc = ck_ref.shape[1]
    n_sel_blocks = nc * CMP_STRIDE // L_SEL
    q_pos = i * tq + lax.broadcasted_iota(jnp.int32, (tq, 1), 0)
    n_end = lax.broadcasted_iota(jnp.int32, (1, nc), 1) * CMP_STRIDE + (L_CMP - 1)
    mask = (n_end <= q_pos) & (n_end < nc * CMP_STRIDE)
    cover = _cover_matrix(nc, n_sel_blocks, LANES)
    q = q_ref[...]
    outs = []
    pieces = []
    for h in range(KV_HEADS):
        q4 = jnp.concatenate([q[:, (h * GQA + g) * HEAD_DIM:(h * GQA + g + 1) * HEAD_DIM] for g in range(GQA)],
                             axis=0)
        s4 = _dot_nt(q4, ck_ref[h])
        ps = [_masked_softmax(s4[g * tq:(g + 1) * tq], mask) for g in range(GQA)]
        o4 = _dot(jnp.concatenate([p.astype(BF16) for p in ps], axis=0), cv_ref[h])
        gates = gates_ref[h]
        outs += [o4[g * tq:(g + 1) * tq] * gates[:, 3 * g:3 * g + 1] for g in range(GQA)]
        pieces += list(_split3(ps[0] + ps[1] + ps[2] + ps[3]))
    oc_ref[...] = jnp.concatenate(outs, axis=-1)
    imp_all = _dot(jnp.concatenate(pieces, axis=0), cover)
    for h in range(KV_HEADS):
        imp = sum(imp_all[(3 * h + r) * tq:(3 * h + r + 1) * tq] for r in range(3))
        score_t = _block_scores(imp, q_pos, n_sel_blocks).T
        sel_ref[h] = _top_k_mask_t(score_t, N_SEL, n_sel_blocks).T.astype(sel_ref.dtype)


def _cmp_prompt(qn, ck, cv, gates_h, t):
    nsa = qn.shape[1]
    nc = ck.shape[1]
    tq = Q_TILE
    kv_spec = pl.BlockSpec((KV_HEADS, nc, HEAD_DIM), lambda i: (0, 0, 0))
    return pl.pallas_call(
        _cmp_prompt_kernel,
        grid=(t // tq,),
        in_specs=[pl.BlockSpec((tq, nsa), lambda i: (i, 0)), kv_spec, kv_spec,
                  pl.BlockSpec((KV_HEADS, tq, LANES), lambda i: (0, i, 0))],
        out_specs=[pl.BlockSpec((tq, nsa), lambda i: (i, 0)),
                   pl.BlockSpec((KV_HEADS, tq, LANES), lambda i: (0, i, 0))],
        out_shape=[jax.ShapeDtypeStruct((t, nsa), F32),
                   jax.ShapeDtypeStruct((KV_HEADS, t, LANES), BF16)],
        compiler_params=_cparams(("parallel",)),
        name="cmp_topk_prompt",
    )(qn, ck, cv, gates_h)


def _gqa_slices(q):
    return [q[:, g * HEAD_DIM:(g + 1) * HEAD_DIM] for g in range(GQA)]


def _gated_concat(outs, gates, branch):
    return jnp.concatenate(
        [o * gates[:, 3 * g + branch:3 * g + branch + 1] for g, o in enumerate(outs)], axis=-1)


def _sel_prompt_kernel(q_ref, k_ref, v_ref, sel_ref, gates_ref, o_ref):
    i = pl.program_id(1)
    tq = q_ref.shape[0]
    tk = SEL_KEY_TILE
    bpt = tk // L_SEL
    q4 = jnp.concatenate(_gqa_slices(q_ref[...]), axis=0)
    sel = sel_ref[...]
    q_pos = i * tq + lax.broadcasted_iota(jnp.int32, (tq, 1), 0)
    blk_of_key = (lax.broadcasted_iota(jnp.int32, (LANES, tk), 0)
                  - lax.broadcasted_iota(jnp.int32, (LANES, tk), 1) // L_SEL)
    key_off = lax.broadcasted_iota(jnp.int32, (1, tk), 1)
    n_tiles = (i * tq + tq - 1) // tk + 1

    def body(j, carry):
        k = k_ref[pl.ds(pl.multiple_of(j * tk, tk), tk), :]
        v = v_ref[pl.ds(pl.multiple_of(j * tk, tk), tk), :]
        expand = jnp.where(blk_of_key == j * bpt, 1.0, 0.0).astype(BF16)
        picked = _dot(sel, expand)
        bias = jnp.where((picked > 0.5) & (key_off + j * tk <= q_pos), 0.0, NEG_BIG)
        s4 = _dot_nt(q4, k)
        stats, ps = [], []
        for g in range(GQA):
            m, l, _ = carry[g]
            s = s4[g * tq:(g + 1) * tq] + bias
            m_new = jnp.maximum(m, jnp.max(s, axis=-1, keepdims=True))
            alpha = jnp.exp(m - m_new)
            p = jnp.exp(s - m_new)
            stats.append((m_new, alpha * l + jnp.sum(p, axis=-1, keepdims=True), alpha))
            ps.append(p.astype(BF16))
        pv = _dot(jnp.concatenate(ps, axis=0), v)
        return tuple((m_new, l, alpha * carry[g][2] + pv[g * tq:(g + 1) * tq])
                     for g, (m_new, l, alpha) in enumerate(stats))

    init = tuple((jnp.full((tq, 1), NEG_BIG, F32), jnp.zeros((tq, 1), F32), jnp.zeros((tq, HEAD_DIM), F32))
                 for _ in range(GQA))
    fin = lax.fori_loop(0, n_tiles, body, init)
    o_ref[...] = _gated_concat([acc / l for _, l, acc in fin], gates_ref[...], 1)


def _sel_prompt(qn, k_h, v_h, sel, gates_h, t):
    nsa = qn.shape[1]
    tq = _pick(t, (SEL_Q_TILE, Q_TILE))
    kv_spec = pl.BlockSpec((None, t, HEAD_DIM), lambda h, i: (h, 0, 0))
    return pl.pallas_call(
        _sel_prompt_kernel,
        grid=(KV_HEADS, t // tq),
        in_specs=[pl.BlockSpec((tq, GQA * HEAD_DIM), lambda h, i: (i, h)), kv_spec, kv_spec,
                  pl.BlockSpec((None, tq, LANES), lambda h, i: (h, i, 0)),
                  pl.BlockSpec((None, tq, LANES), lambda h, i: (h, i, 0))],
        out_specs=pl.BlockSpec((tq, GQA * HEAD_DIM), lambda h, i: (i, h)),
        out_shape=jax.ShapeDtypeStruct((t, nsa), F32),
        compiler_params=_cparams(("parallel", "arbitrary")),
        name="sel_prompt",
    )(qn, k_h, v_h, sel, gates_h)


def _win_prompt_kernel(q_ref, k_ref, v_ref, gates_ref, o_ref):
    i = pl.program_id(1)
    tq = q_ref.shape[0]
    span = WINDOW + tq
    start = pl.multiple_of(jnp.maximum(i * tq - WINDOW, 0), tq)
    k = k_ref[pl.ds(start, span), :]
    v = v_ref[pl.ds(start, span), :]
    q_pos = i * tq + lax.broadcasted_iota(jnp.int32, (tq, 1), 0)
    k_pos = start + lax.broadcasted_iota(jnp.int32, (1, span), 1)
    bias = jnp.where((k_pos <= q_pos) & (k_pos >= q_pos - WINDOW), 0.0, NEG_BIG)
    s4 = _dot_nt(jnp.concatenate(_gqa_slices(q_ref[...]), axis=0), k)
    ps, sums = [], []
    for g in range(GQA):
        s = s4[g * tq:(g + 1) * tq] + bias
        p = jnp.exp(s - jnp.max(s, axis=-1, keepdims=True))
        sums.append(jnp.sum(p, axis=-1, keepdims=True))
        ps.append(p.astype(BF16))
    o4 = _dot(jnp.concatenate(ps, axis=0), v)
    o_ref[...] = _gated_concat([o4[g * tq:(g + 1) * tq] / sums[g] for g in range(GQA)], gates_ref[...], 2)


def _win_prompt(qn, k_h, v_h, gates_h, t):
    nsa = qn.shape[1]
    tq = Q_TILE
    kv_spec = pl.BlockSpec((None, t, HEAD_DIM), lambda h, i: (h, 0, 0))
    return pl.pallas_call(
        _win_prompt_kernel,
        grid=(KV_HEADS, t // tq),
        in_specs=[pl.BlockSpec((tq, GQA * HEAD_DIM), lambda h, i: (i, h)), kv_spec, kv_spec,
                  pl.BlockSpec((None, tq, LANES), lambda h, i: (h, i, 0))],
        out_specs=pl.BlockSpec((tq, GQA * HEAD_DIM), lambda h, i: (i, h)),
        out_shape=jax.ShapeDtypeStruct((t, nsa), F32),
        compiler_params=_cparams(("parallel", "arbitrary")),
        name="win_prompt",
    )(qn, k_h, v_h, gates_h)


def _own_head_rows(parts, width):
    row_head = lax.broadcasted_iota(jnp.int32, (ATT_HEADS, 1), 0) // GQA
    out = jnp.zeros((ATT_HEADS, width), F32)
    for h in range(KV_HEADS):
        out = jnp.where(row_head == h, parts[h], out)
    return out


def _cmp_sample_kernel(past, q_ref, ck_ref, cv_ref, g3_ref, oc_ref, sel_ref):
    nc = ck_ref.shape[1]
    width = sel_ref.shape[1]
    n_sel_blocks = past // L_SEL + 1
    q = q_ref[...]
    n_end = lax.broadcasted_iota(jnp.int32, (1, nc), 1) * CMP_STRIDE + (L_CMP - 1)
    mask = n_end <= past
    s = _own_head_rows([_dot_nt(q, ck_ref[h]) for h in range(KV_HEADS)], nc)
    p = _masked_softmax(s, mask)
    pb = p.astype(BF16)
    o = _own_head_rows([_dot(pb, cv_ref[h]) for h in range(KV_HEADS)], HEAD_DIM)
    oc_ref[...] = o * g3_ref[:, 0:1]
    row_head = lax.broadcasted_iota(jnp.int32, (ATT_HEADS, 1), 0) // GQA
    row8 = lax.broadcasted_iota(jnp.int32, (8, 1), 0)
    p_sum = jnp.zeros((8, nc), F32)
    for h in range(KV_HEADS):
        ph = jnp.sum(jnp.where(row_head == h, p, 0.0), axis=0, keepdims=True)
        p_sum = jnp.where(row8 == h, ph, p_sum)
    imp = _dot3(p_sum, _cover_matrix(nc, n_sel_blocks, width))
    q_pos = jnp.full((8, 1), past, jnp.int32)
    sel_ref[...] = _top_k_mask(_block_scores(imp, q_pos, n_sel_blocks), N_SEL)


def _cmp_sample(q3, ck_s, cv_s, g3, past):
    bs = q3.shape[0]
    nc = ck_s.shape[2]
    width = -(-(past // L_SEL + 1) // LANES) * LANES
    kv_spec = pl.BlockSpec((None, KV_HEADS, nc, HEAD_DIM), lambda b: (b, 0, 0, 0))
    return pl.pallas_call(
        functools.partial(_cmp_sample_kernel, past),
        grid=(bs,),
        in_specs=[pl.BlockSpec((None, ATT_HEADS, HEAD_DIM), lambda b: (b, 0, 0)), kv_spec, kv_spec,
                  pl.BlockSpec((None, ATT_HEADS, LANES), lambda b: (b, 0, 0))],
        out_specs=[pl.BlockSpec((None, ATT_HEADS, HEAD_DIM), lambda b: (b, 0, 0)),
                   pl.BlockSpec((None, 8, width), lambda b: (b, 0, 0))],
        out_shape=[jax.ShapeDtypeStruct((bs, ATT_HEADS, HEAD_DIM), F32),
                   jax.ShapeDtypeStruct((bs, 8, width), F32)],
        compiler_params=_cparams(("parallel",)),
        name="cmp_topk_sample",
    )(q3, ck_s, cv_s, g3)


def _diag_heads(x):
    return _own_head_rows([x[:, h * HEAD_DIM:(h + 1) * HEAD_DIM] for h in range(KV_HEADS)], HEAD_DIM)


def _sel_win_sample_kernel(past, pt_ref, *refs):
    n_pg = PAGES_PER_STEP
    page_refs = refs[:n_pg]
    (qbd_ref, selg_ref, selnew_ref, knew_ref, win_ref, wnew_ref, oc_ref, g3_ref,
     o_ref, m_ref, l_ref, acc_ref, kv_ref) = refs[n_pg:]
    j = pl.program_id(1)
    pg = page_refs[0].shape[1]
    qbd = qbd_ref[...]
    bpp = pg // L_SEL

    @pl.when(j == 0)
    def _():
        m_ref[...] = jnp.full(m_ref.shape, NEG_BIG, F32)
        l_ref[...] = jnp.zeros(l_ref.shape, F32)
        acc_ref[...] = jnp.zeros(acc_ref.shape, F32)

    sel = selg_ref[...].astype(BF16)
    n_blk = n_pg * bpp
    expand = jnp.where(lax.broadcasted_iota(jnp.int32, (n_blk, n_pg * pg), 0)
                       == lax.broadcasted_iota(jnp.int32, (n_blk, n_pg * pg), 1) // L_SEL, 1.0, 0.0).astype(BF16)
    picked = _dot(sel, expand)
    for k in range(n_pg):
        kv_ref[:, k * pg:(k + 1) * pg] = page_refs[k][...].astype(BF16)
    m, l, acc = m_ref[...], l_ref[...], acc_ref[...]
    s = _dot(qbd, kv_ref[0:K_COLS, :]) + jnp.where(picked > 0.5, 0.0, NEG_BIG)
    m_new = jnp.maximum(m, jnp.max(s, axis=-1, keepdims=True))
    alpha = jnp.exp(m - m_new)
    p = jnp.exp(s - m_new)
    l = alpha * l + jnp.sum(p, axis=-1, keepdims=True)
    acc = alpha * acc + _dot_nt(p.astype(BF16), kv_ref[K_COLS:, :])
    m = m_new
    m_ref[...], l_ref[...], acc_ref[...] = m, l, acc

    @pl.when(j == pl.num_programs(1) - 1)
    def _():
        qf = qbd.astype(F32)
        kn = knew_ref[...]
        s_new = jnp.sum(qf * kn[:, :K_COLS].astype(BF16).astype(F32), axis=-1, keepdims=True)
        ok_new = selnew_ref[:, 0:1] > 0.5
        s_new = jnp.where(ok_new, s_new, NEG_BIG)
        m2 = jnp.maximum(m, s_new)
        a2 = jnp.exp(m - m2)
        p_new = jnp.where(ok_new, jnp.exp(s_new - m2), 0.0)
        l2 = a2 * l + p_new
        acc2 = a2 * acc + p_new.astype(BF16).astype(F32) * kn[:, K_COLS:].astype(BF16).astype(F32)
        o_s = _diag_heads(acc2 / jnp.maximum(l2, 1e-30))
        w = win_ref[...].astype(BF16)
        w_buf = w.shape[1]
        wn = wnew_ref[...]
        w_pos = past - w_buf + lax.broadcasted_iota(jnp.int32, (1, w_buf), 1)
        ok_w = (w_pos <= past) & (w_pos >= past - WINDOW) & (w_pos >= 0)
        s_w = jnp.where(ok_w, _dot(qbd, w[:K_COLS, :]), -jnp.inf)
        s_wn = jnp.sum(qf * wn[:, :K_COLS].astype(BF16).astype(F32), axis=-1, keepdims=True)
        m_w = jnp.maximum(jnp.max(s_w, axis=-1, keepdims=True), s_wn)
        p_w = jnp.where(ok_w, jnp.exp(s_w - m_w), 0.0)
        p_wn = jnp.exp(s_wn - m_w)
        l_w = jnp.sum(p_w, axis=-1, keepdims=True) + p_wn
        p_w = p_w / l_w
        p_wn = p_wn / l_w
        o_w = _diag_heads(_dot_nt(p_w.astype(BF16), w[K_COLS:, :])
                          + p_wn.astype(BF16).astype(F32) * wn[:, K_COLS:].astype(BF16).astype(F32))
        g3 = g3_ref[...]
        o_ref[...] = oc_ref[...] + g3[:, 1:2] * o_s + g3[:, 2:3] * o_w


def _sel_win_sample(cache_t, page_base, page_table, qbd, sel_groups, sel_new, kvs_new, win_t, win_base,
                    kvw_new, oc, g3, past):
    bs, n_pages = page_table.shape
    pg = cache_t.shape[2]
    n_pg = PAGES_PER_STEP
    bps = n_pg * pg // L_SEL
    w_buf = win_t.shape[2]

    def page_map(b, j, pt, k):
        return (page_base + pt[b, j * n_pg + k], 0, 0)

    per_b = lambda *shape: pl.BlockSpec((None,) + shape, lambda b, j, pt: (b,) + (0,) * len(shape))
    grid_spec = pltpu.PrefetchScalarGridSpec(
        num_scalar_prefetch=1,
        grid=(bs, n_pages // n_pg),
        in_specs=([pl.BlockSpec((None, KV_COLS, pg), functools.partial(page_map, k=k)) for k in range(n_pg)]
                  + [per_b(ATT_HEADS, K_COLS),
                     pl.BlockSpec((None, None, ATT_HEADS, bps), lambda b, j, pt: (b, j, 0, 0)),
                     per_b(ATT_HEADS, LANES), per_b(1, KV_COLS),
                     pl.BlockSpec((None, KV_COLS, w_buf), lambda b, j, pt: (win_base + b, 0, 0)),
                     per_b(1, KV_COLS), per_b(ATT_HEADS, HEAD_DIM), per_b(ATT_HEADS, LANES)]),
        out_specs=per_b(ATT_HEADS, HEAD_DIM),
        scratch_shapes=[pltpu.VMEM((ATT_HEADS, 1), F32), pltpu.VMEM((ATT_HEADS, 1), F32),
                        pltpu.VMEM((ATT_HEADS, K_COLS), F32), pltpu.VMEM((KV_COLS, n_pg * pg), BF16)],
    )
    return pl.pallas_call(
        functools.partial(_sel_win_sample_kernel, past),
        grid_spec=grid_spec,
        out_shape=jax.ShapeDtypeStruct((bs, ATT_HEADS, HEAD_DIM), F32),
        compiler_params=_cparams(("parallel", "arbitrary")),
        name="sel_win_sample",
    )(page_table, *([cache_t] * n_pg), qbd, sel_groups, sel_new, kvs_new, win_t, kvw_new, oc, g3)


def _column_layout(d_model):
    ssm_dim = d_model // 2
    conv_ch = ssm_dim + 2 * SSM_GROUPS * SSM_STATE
    nsa = d_model - ssm_dim
    n_heads = ssm_dim // SSM_HEAD_DIM
    sizes = (ssm_dim, conv_ch, n_heads, nsa, KV_COLS, KV_COLS, KV_COLS, 3 * ATT_HEADS)
    offs = [0]
    for s in sizes:
        offs.append(offs[-1] + s)
    src = dict(zip(("z", "xbc", "dt", "q", "kvc", "kvs", "kvw", "gt"), zip(offs[:-1], sizes)))
    order = ("xbc", "z", "q", "kvc", "kvs", "kvw")
    col, start = {}, {}
    pos = 0
    for name in order:
        o, w = src[name]
        assert pos % w == 0
        col[name] = pos // w
        start[name] = pos
        pos += w
    col["small"] = pos // LANES
    start["small"] = pos
    return src, order, col, start, pos + LANES


def _layer_weights(l, p, d_model):
    src, order, col, start, n_cols = _column_layout(d_model)
    w_in = p["w_in"][l]
    n_heads = src["dt"][1]
    small = jnp.concatenate([w_in[:, src["dt"][0]:src["dt"][0] + n_heads],
                             w_in[:, src["gt"][0]:src["gt"][0] + src["gt"][1]]], axis=1)
    small = jnp.pad(small, ((0, 0), (0, LANES - small.shape[1])))
    w_in_packed = jnp.concatenate([w_in[:, src[n][0]:src[n][0] + src[n][1]] for n in order] + [small],
                                  axis=1).astype(BF16)
    ssm_dim = src["z"][1]
    rep = lambda v: jnp.repeat(v, SSM_HEAD_DIM)[None, :]
    pad_l = lambda v: jnp.pad(v, (0, LANES - v.shape[0]))[None, :]
    tile2 = lambda v: jnp.tile(v, 2)[None, :]
    cw = jnp.transpose(p["cmp_w"][l], (1, 0, 2, 3))
    zero = jnp.zeros_like(cw)
    pairs = jnp.concatenate([jnp.concatenate([cw, zero], axis=-1),
                             jnp.concatenate([zero, cw], axis=-1)], axis=-2)
    w_cat = jnp.transpose(pairs, (1, 0, 2, 3)).reshape(2, L_CMP * LANES, LANES).astype(BF16)
    pe = p["cmp_pe"][l]
    pe_rows = jnp.broadcast_to(pe[:, :, None, :], (L_CMP, 2, KV_HEADS, HEAD_DIM)).reshape(L_CMP, KV_COLS)
    head_expand = (jnp.arange(LANES)[:, None] == jnp.arange(ssm_dim)[None, :] // SSM_HEAD_DIM).astype(BF16)
    return {
        "col": col, "start": start,
        "norm1_g": p["norm1_g"][l][None, :], "w_in": w_in_packed,
        "conv_w": p["conv_w"][l], "conv_b": p["conv_b"][l][None, :],
        "dt_bias": pad_l(p["dt_bias"][l]), "a_log": pad_l(p["a_log"][l]),
        "dt_bias_e": rep(p["dt_bias"][l]), "a_log_e": rep(p["a_log"][l]),
        "dsk": rep(p["d_skip"][l]), "ssm_norm_g": p["ssm_norm_g"][l][None, :],
        "head_expand": head_expand,
        "g_q": tile2(p["q_norm_g"][l]), "g_kc": tile2(p["k_norm_g"][l, 0]),
        "g_ks": tile2(p["k_norm_g"][l, 1]), "g_kw": tile2(p["k_norm_g"][l, 2]),
        "cmp_pe_rows": pe_rows, "cmp_w_cat": w_cat,
        "w_out": p["w_out"][l].astype(BF16), "norm2_g": p["norm2_g"][l][None, :],
        "w_gu": p["w_gu"][l].astype(BF16), "w_down": p["w_down"][l].astype(BF16),
    }


def _pick(n, prefs):
    for c in prefs:
        if n % c == 0:
            return c
    return n


def _dense_tail(x, y_ssd, atts, lw, tm):
    d = x.shape[1]
    h = _out_proj(x, y_ssd, atts, lw["w_out"], tm, _pick(d, (512,)))
    d_ff = lw["w_down"].shape[0]
    act = _ffn_up(h, lw["norm2_g"], lw["w_gu"], tm, _pick(d_ff, (512, 256, 128)))
    return _ffn_down(act, lw["w_down"], h, tm, _pick(d, (512,)))


def kernel(x_prompt, x_sample, cache_cmp_kv, cache_slc_kv, state_win_kv, state_ssm, state_conv, page_table,
           norm1_g, w_in, conv_w, conv_b, dt_bias, a_log, d_skip, ssm_norm_g, q_norm_g, k_norm_g,
           cmp_pe, cmp_w, w_out, norm2_g, w_gu, w_down):
    params = dict(norm1_g=norm1_g, w_in=w_in, conv_w=conv_w, conv_b=conv_b, dt_bias=dt_bias, a_log=a_log,
                  d_skip=d_skip, ssm_norm_g=ssm_norm_g, q_norm_g=q_norm_g, k_norm_g=k_norm_g, cmp_pe=cmp_pe,
                  cmp_w=cmp_w, w_out=w_out, norm2_g=norm2_g, w_gu=w_gu, w_down=w_down)
    bp, t, d = x_prompt.shape
    bs, dec_t, _ = x_sample.shape
    depth = w_in.shape[0]
    n_pool, pg = cache_cmp_kv.shape[1:3]
    n_pages = page_table.shape[1]
    past = n_pages * pg
    w_buf = state_win_kv.shape[2]
    assert bp == 1 and dec_t == 1
    assert t % (PAGES_PER_STEP * pg) == 0 and past % (PAGES_PER_STEP * pg) == 0 and t // L_SEL <= LANES
    ssm_dim = d // 2
    n_heads = ssm_dim // SSM_HEAD_DIM
    kv_shape = (2, KV_HEADS, HEAD_DIM)

    xp = x_prompt[0]
    xs = x_sample[:, 0]
    cmp_pages = cache_cmp_kv.reshape(depth * n_pool, pg, KV_COLS)
    feature_major = lambda a: jnp.transpose(a, (0, 1, 3, 4, 5, 2)).reshape(a.shape[0] * a.shape[1], KV_COLS,
                                                                           a.shape[2])
    slc_pages_t = feature_major(cache_slc_kv)
    win_t = feature_major(state_win_kv)
    cmp_rows = _pick(t, (2 * PAGES_PER_STEP * pg, PAGES_PER_STEP * pg))
    outs = {k: [] for k in ("cmp_p", "cmp_s", "slc_p", "slc_s", "win_p", "win_s", "ssm_p", "ssm_s",
                            "conv_p", "conv_s")}
    tm_p = _pick(t, (512, 256, 128))
    for l in range(depth):
        lw = _layer_weights(l, params, d)
        col, start = lw["col"], lw["start"]
        n_cols = lw["w_in"].shape[1]
        tn_in = _pick(n_cols, (1152, 640, 384, 128))

        u = _norm_matmul(xp, lw["norm1_g"], lw["w_in"], tm_p, tn_in)
        qn, kvs_n, kvw_n, k_s, v_s, k_w, v_w, gates_h = _prep(u, col, lw["g_q"], lw["g_ks"], lw["g_kw"],
                                                               _pick(t, (256, 128)))
        y_ssd, h_fin = _ssd_prompt(u, col, lw)
        kvc = u[:, start["kvc"]:start["kvc"] + KV_COLS]
        xbc_tail = u[t - (CONV_K - 1):, start["xbc"]:start["xbc"] + lw["conv_w"].shape[1]]
        ck, cv = _compress_prompt(jnp.pad(kvc, ((0, CMP_STRIDE), (0, 0))), lw, cmp_rows)
        o_c, sel = _cmp_prompt(qn, ck, cv, gates_h, t)
        o_s = _sel_prompt(qn, k_s, v_s, sel, gates_h, t)
        o_w = _win_prompt(qn, k_w, v_w, gates_h, t)
        xp = _dense_tail(xp, y_ssd, [o_c, o_s, o_w], lw, tm_p)
        outs["cmp_p"].append(kvc.reshape((1, t) + kv_shape))
        outs["slc_p"].append(kvs_n.reshape((1, t) + kv_shape))
        outs["win_p"].append(kvw_n[t - min(WINDOW, t):].reshape((1, min(WINDOW, t)) + kv_shape))
        outs["ssm_p"].append(h_fin.reshape(1, n_heads, SSM_HEAD_DIM, SSM_STATE))
        outs["conv_p"].append(xbc_tail[None])

        us = _norm_matmul(xs, lw["norm1_g"], lw["w_in"], bs, tn_in)
        qn_s, kvs_s, kvw_s, _, _, _, _, gates_s = _prep(us, col, lw["g_q"], lw["g_ks"], lw["g_kw"], bs)
        conv_t = jnp.transpose(state_conv[l], (1, 0, 2))
        y_ssd_s, h_new = _ssd_sample(us, col, conv_t, state_ssm[l].reshape(bs, ssm_dim, SSM_STATE), lw)
        xbc_s = us[:, start["xbc"]:start["xbc"] + lw["conv_w"].shape[1]]
        kvc_s = us[:, start["kvc"]:start["kvc"] + KV_COLS]
        ck_s, cv_s = _compress_sample(cmp_pages, l * n_pool, page_table, lw)
        q3 = qn_s.reshape(bs, ATT_HEADS, HEAD_DIM)
        g3 = jnp.transpose(gates_s[:, :, :3 * GQA].reshape(KV_HEADS, bs, GQA, 3), (1, 0, 2, 3))
        g3 = jnp.pad(g3.reshape(bs, ATT_HEADS, 3), ((0, 0), (0, 0), (0, LANES - 3)))
        oc_s, sel_s = _cmp_sample(q3, ck_s, cv_s, g3, past)
        sel16 = jnp.repeat(sel_s[:, :KV_HEADS], GQA, axis=1)
        bps = PAGES_PER_STEP * pg // L_SEL
        sel_groups = jnp.transpose(sel16[:, :, :past // L_SEL].reshape(bs, ATT_HEADS, past // L_SEL // bps, bps),
                                   (0, 2, 1, 3))
        sel_new = jnp.pad(sel16[:, :, past // L_SEL:past // L_SEL + 1], ((0, 0), (0, 0), (0, LANES - 1)))
        head_of_col = jnp.arange(K_COLS) // HEAD_DIM
        qbd = jnp.where(head_of_col[None, None, :] == (jnp.arange(ATT_HEADS) // GQA)[None, :, None],
                        jnp.tile(q3, (1, 1, KV_HEADS)), jnp.zeros((), BF16))
        o_att_s = _sel_win_sample(slc_pages_t, l * n_pool, page_table, qbd, sel_groups, sel_new,
                                  kvs_s[:, None, :], win_t, l * bs, kvw_s[:, None, :], oc_s, g3, past)
        xs = _dense_tail(xs, y_ssd_s.astype(BF16), [o_att_s.reshape(bs, ATT_HEADS * HEAD_DIM)], lw, bs)
        outs["cmp_s"].append(kvc_s.reshape((bs, 1) + kv_shape))
        outs["slc_s"].append(kvs_s.reshape((bs, 1) + kv_shape))
        outs["win_s"].append(jnp.concatenate([state_win_kv[l], kvw_s.reshape((bs, 1) + kv_shape)],
                                             axis=1)[:, -w_buf:])
        outs["ssm_s"].append(h_new.reshape(bs, n_heads, SSM_HEAD_DIM, SSM_STATE))
        outs["conv_s"].append(jnp.concatenate([state_conv[l], xbc_s[:, None, :]], axis=1)[:, -(CONV_K - 1):])

    st = lambda k: jnp.stack(outs[k])
    return (xp[None], xs[:, None], st("cmp_p"), st("cmp_s"), st("slc_p"), st("slc_s"), st("win_p"),
            st("win_s"), st("ssm_p"), st("ssm_s"), st("conv_p"), st("conv_s"))
```

```python
import functools

import jax
import jax.numpy as jnp
from jax import lax
from jax.experimental import pallas as pl
from jax.experimental.pallas import tpu as pltpu

F32 = jnp.float32
BF16 = jnp.bfloat16

RMS_EPS = 1e-6
SSM_HEAD_DIM = 64
SSM_GROUPS = 4
SSM_STATE = 128
CONV_K = 4
SSD_CHUNK = 128
HEAD_DIM = 64
KV_HEADS = 4
GQA = 4
ATT_HEADS = KV_HEADS * GQA
CMP_STRIDE = 16
L_CMP = 32
L_SEL = 64
N_SEL = 16
WINDOW = 512
SEL_FORCE = 1e9
KV_COLS = 2 * KV_HEADS * HEAD_DIM
K_COLS = KV_HEADS * HEAD_DIM

LANES = 128
Q_TILE = 128
SEL_Q_TILE = 256
SEL_KEY_TILE = 512
PAGES_PER_STEP = 16
VMEM_LIMIT = 56 * 1024 * 1024
NEG_BIG = -1e30
LOG2_E = 1.4426950408889634


def _cparams(sem):
    return pltpu.CompilerParams(dimension_semantics=sem, vmem_limit_bytes=VMEM_LIMIT)


def _sigmoid(x):
    return 1.0 / (1.0 + jnp.exp(-x))


def _silu(x):
    return x * _sigmoid(x)


def _softplus(x):
    return jnp.maximum(x, 0.0) + jnp.log(1.0 + jnp.exp(-jnp.abs(x)))


def _split3(x):
    hi = x.astype(BF16)
    r = x - hi.astype(F32)
    mid = r.astype(BF16)
    lo = (r - mid.astype(F32)).astype(BF16)
    return hi, mid, lo


def _dot(a, b):
    return jnp.dot(a, b, preferred_element_type=F32)


def _dot_nt(a, b):
    return lax.dot_general(a, b, (((1,), (1,)), ((), ())), preferred_element_type=F32)


def _dot_tn(a, b):
    return lax.dot_general(a, b, (((0,), (0,)), ((), ())), preferred_element_type=F32)


def _dot3(x, m):
    hi, mid, lo = _split3(x)
    return _dot(hi, m) + _dot(mid, m) + _dot(lo, m)


def _head_rms(x, g2):
    lane = lax.broadcasted_iota(jnp.int32, (1, LANES), 1)
    lo = lane < HEAD_DIM
    outs = []
    for j in range(x.shape[1] // LANES):
        xt = x[:, j * LANES:(j + 1) * LANES]
        sq = xt * xt
        s_lo = jnp.sum(jnp.where(lo, sq, 0.0), axis=-1, keepdims=True)
        s_hi = jnp.sum(jnp.where(lo, 0.0, sq), axis=-1, keepdims=True)
        ms = jnp.where(lo, s_lo, s_hi) * (1.0 / HEAD_DIM)
        outs.append(xt * lax.rsqrt(ms + RMS_EPS) * g2)
    return outs[0] if len(outs) == 1 else jnp.concatenate(outs, axis=-1)


def _masked_softmax(s, mask):
    s = jnp.where(mask, s, -jnp.inf)
    m = jnp.max(s, axis=-1, keepdims=True)
    m = jnp.where(m == -jnp.inf, 0.0, m)
    e = jnp.where(mask, jnp.exp(s - m), 0.0)
    return e / jnp.maximum(jnp.sum(e, axis=-1, keepdims=True), 1e-30)


def _top_k_mask(score, k):
    lane = lax.broadcasted_iota(jnp.int32, score.shape, 1).astype(F32)
    sel = jnp.zeros(score.shape, F32)
    removed = -3.0
    for _ in range(k):
        m = jnp.max(score, axis=-1, keepdims=True)
        idx = jnp.min(jnp.where(score == m, lane, 1e9), axis=-1, keepdims=True)
        hit = lane == idx
        sel = jnp.where(hit, 1.0, sel)
        score = jnp.where(hit, removed, score)
    return sel


def _top_k_mask_t(score_t, k, n_cand):
    sub = 8
    n_rows, cols = score_t.shape
    row_in_blk = lax.broadcasted_iota(jnp.int32, (sub, cols), 0)
    n_blk = -(-n_cand // sub)
    blocks = [score_t[r * sub:(r + 1) * sub, :] for r in range(n_blk)]
    counts = [jnp.zeros((sub, cols), F32) for _ in range(n_blk)]
    for c in range(n_cand):
        row = score_t[c:c + 1, :]
        for r in range(n_blk):
            gt = jnp.where(row > blocks[r], 1.0, 0.0)
            ge = jnp.where(row >= blocks[r], 1.0, 0.0)
            if r * sub + sub - 1 < c:
                beats = gt
            elif r * sub > c:
                beats = ge
            else:
                beats = jnp.where(row_in_blk + r * sub > c, ge, gt)
            counts[r] = counts[r] + beats
    picked = [jnp.where(cnt < k, 1.0, 0.0) for cnt in counts]
    if n_blk * sub < n_rows:
        picked.append(jnp.zeros((n_rows - n_blk * sub, cols), F32))
    return jnp.concatenate(picked, axis=0)


def _norm_matmul_kernel(x_ref, g_ref, w_ref, o_ref, xn_ref):
    @pl.when(pl.program_id(1) == 0)
    def _():
        x = x_ref[...]
        ms = jnp.mean(x * x, axis=-1, keepdims=True)
        xn_ref[...] = (x * lax.rsqrt(ms + RMS_EPS) * g_ref[...]).astype(BF16)

    o_ref[...] = _dot(xn_ref[...], w_ref[...]).astype(o_ref.dtype)


def _norm_matmul(x, g, w, tm, tn):
    rows, d = x.shape
    n = w.shape[1]
    return pl.pallas_call(
        _norm_matmul_kernel,
        grid=(rows // tm, n // tn),
        in_specs=[pl.BlockSpec((tm, d), lambda i, j: (i, 0)),
                  pl.BlockSpec((1, d), lambda i, j: (0, 0)),
                  pl.BlockSpec((d, tn), lambda i, j: (0, j))],
        out_specs=pl.BlockSpec((tm, tn), lambda i, j: (i, j)),
        out_shape=jax.ShapeDtypeStruct((rows, n), F32),
        scratch_shapes=[pltpu.VMEM((tm, d), BF16)],
        compiler_params=_cparams(("parallel", "arbitrary")),
        name="in_proj",
    )(x, g, w)


def _ffn_up_kernel(x_ref, g_ref, wg_ref, wv_ref, o_ref, xn_ref):
    @pl.when(pl.program_id(1) == 0)
    def _():
        x = x_ref[...]
        ms = jnp.mean(x * x, axis=-1, keepdims=True)
        xn_ref[...] = (x * lax.rsqrt(ms + RMS_EPS) * g_ref[...]).astype(BF16)

    xn = xn_ref[...]
    gate = _dot(xn, wg_ref[...])
    val = _dot(xn, wv_ref[...])
    o_ref[...] = (_silu(gate) * val).astype(o_ref.dtype)


def _ffn_up(x, g, w_gu, tm, tn):
    rows, d = x.shape
    d_ff = w_gu.shape[1] // 2
    nj = d_ff // tn
    return pl.pallas_call(
        _ffn_up_kernel,
        grid=(rows // tm, nj),
        in_specs=[pl.BlockSpec((tm, d), lambda i, j: (i, 0)),
                  pl.BlockSpec((1, d), lambda i, j: (0, 0)),
                  pl.BlockSpec((d, tn), lambda i, j: (0, j)),
                  pl.BlockSpec((d, tn), lambda i, j: (0, j + nj))],
        out_specs=pl.BlockSpec((tm, tn), lambda i, j: (i, j)),
        out_shape=jax.ShapeDtypeStruct((rows, d_ff), BF16),
        scratch_shapes=[pltpu.VMEM((tm, d), BF16)],
        compiler_params=_cparams(("parallel", "arbitrary")),
        name="ffn_up",
    )(x, g, w_gu, w_gu)


def _ffn_down_kernel(a_ref, w_ref, h_ref, o_ref):
    o_ref[...] = h_ref[...] + _dot(a_ref[...], w_ref[...])


def _ffn_down(act, w_down, h, tm, tn):
    rows, d_ff = act.shape
    d = w_down.shape[1]
    return pl.pallas_call(
        _ffn_down_kernel,
        grid=(rows // tm, d // tn),
        in_specs=[pl.BlockSpec((tm, d_ff), lambda i, j: (i, 0)),
                  pl.BlockSpec((d_ff, tn), lambda i, j: (0, j)),
                  pl.BlockSpec((tm, tn), lambda i, j: (i, j))],
        out_specs=pl.BlockSpec((tm, tn), lambda i, j: (i, j)),
        out_shape=jax.ShapeDtypeStruct((rows, d), F32),
        compiler_params=_cparams(("parallel", "arbitrary")),
        name="ffn_down",
    )(act, w_down, h)


def _out_proj_kernel(n_att, x_ref, y_ref, *refs):
    att_refs = refs[:n_att]
    w1_ref, w2_ref, o_ref = refs[n_att:]
    att = att_refs[0][...]
    for r in att_refs[1:]:
        att = att + r[...]
    o_ref[...] = (x_ref[...] + _dot(y_ref[...], w1_ref[...])
                  + _dot(att.astype(BF16), w2_ref[...]))


def _out_proj(x, y_ssd, atts, w_out, tm, tn):
    rows, d = x.shape
    half = y_ssd.shape[1]
    n_att = len(atts)
    return pl.pallas_call(
        functools.partial(_out_proj_kernel, n_att),
        grid=(rows // tm, d // tn),
        in_specs=([pl.BlockSpec((tm, tn), lambda i, j: (i, j)),
                   pl.BlockSpec((tm, half), lambda i, j: (i, 0))]
                  + [pl.BlockSpec((tm, half), lambda i, j: (i, 0)) for _ in atts]
                  + [pl.BlockSpec((half, tn), lambda i, j: (0, j)),
                     pl.BlockSpec((half, tn), lambda i, j: (1, j))]),
        out_specs=pl.BlockSpec((tm, tn), lambda i, j: (i, j)),
        out_shape=jax.ShapeDtypeStruct((rows, d), F32),
        compiler_params=_cparams(("parallel", "arbitrary")),
        name="out_proj",
    )(x, y_ssd, *atts, w_out, w_out)


def _prep_kernel(transposed, q_ref, ks_ref, kw_ref, sm_ref, gq_ref, gks_ref, gkw_ref,
                 qn_ref, kvs_ref, kvw_ref, khs_ref, khw_ref, gates_ref, *t_refs):
    qn = _head_rms(q_ref[...], gq_ref[...]) * (HEAD_DIM ** -0.5)
    qn_ref[...] = qn.astype(BF16)
    if transposed:
        qt_ref, vts_ref, vtw_ref = t_refs
        q2 = qn * LOG2_E
        for h in range(KV_HEADS):
            qt_ref[h] = q2[:, h * GQA * HEAD_DIM:(h + 1) * GQA * HEAD_DIM].T.astype(BF16)
    for src, g_ref, full_ref, kh_ref, vt_ref in ((ks_ref, gks_ref, kvs_ref, khs_ref, vts_ref if transposed else None),
                                                 (kw_ref, gkw_ref, kvw_ref, khw_ref, vtw_ref if transposed else None)):
        kv = src[...]
        kn = _head_rms(kv[:, :K_COLS], g_ref[...])
        v = kv[:, K_COLS:]
        full_ref[:, :K_COLS] = kn
        full_ref[:, K_COLS:] = v
        for h in range(KV_HEADS):
            kh_ref[h] = kn[:, h * HEAD_DIM:(h + 1) * HEAD_DIM].astype(BF16)
        if transposed:
            for pair in range(KV_HEADS // 2):
                vt = v[:, pair * LANES:(pair + 1) * LANES].T.astype(BF16)
                vt_ref[2 * pair] = vt[:HEAD_DIM]
                vt_ref[2 * pair + 1] = vt[HEAD_DIM:]
    sig = _sigmoid(sm_ref[...])
    n_dt = LANES // 8
    for h in range(KV_HEADS):
        gates_ref[h] = pltpu.roll(sig, LANES - n_dt - 3 * GQA * h, axis=1)


def _prep(u, col, g_q, g_ks, g_kw, tm, transposed):
    rows = u.shape[0]
    nsa = ATT_HEADS * HEAD_DIM
    row_spec = lambda w, c: pl.BlockSpec((tm, w), lambda i, c=c: (i, c))
    head_out = jax.ShapeDtypeStruct((KV_HEADS, rows, HEAD_DIM), BF16)
    head_spec = pl.BlockSpec((KV_HEADS, tm, HEAD_DIM), lambda i: (0, i, 0))
    vec = pl.BlockSpec((1, LANES), lambda i: (0, 0))
    out_specs = [pl.BlockSpec((tm, nsa), lambda i: (i, 0)),
                 pl.BlockSpec((tm, KV_COLS), lambda i: (i, 0)),
                 pl.BlockSpec((tm, KV_COLS), lambda i: (i, 0)),
                 head_spec, head_spec,
                 pl.BlockSpec((KV_HEADS, tm, LANES), lambda i: (0, i, 0))]
    out_shape = [jax.ShapeDtypeStruct((rows, nsa), BF16),
                 jax.ShapeDtypeStruct((rows, KV_COLS), F32),
                 jax.ShapeDtypeStruct((rows, KV_COLS), F32),
                 head_out, head_out,
                 jax.ShapeDtypeStruct((KV_HEADS, rows, LANES), F32)]
    if transposed:
        vt_spec = pl.BlockSpec((KV_HEADS, HEAD_DIM, tm), lambda i: (0, 0, i))
        vt_out = jax.ShapeDtypeStruct((KV_HEADS, HEAD_DIM, rows), BF16)
        out_specs += [pl.BlockSpec((KV_HEADS, GQA * HEAD_DIM, tm), lambda i: (0, 0, i)), vt_spec, vt_spec]
        out_shape += [jax.ShapeDtypeStruct((KV_HEADS, GQA * HEAD_DIM, rows), BF16), vt_out, vt_out]
    return pl.pallas_call(
        functools.partial(_prep_kernel, transposed),
        grid=(rows // tm,),
        in_specs=[row_spec(nsa, col["q"]), row_spec(KV_COLS, col["kvs"]),
                  row_spec(KV_COLS, col["kvw"]), row_spec(LANES, col["small"]), vec, vec, vec],
        out_specs=out_specs,
        out_shape=out_shape,
        compiler_params=_cparams(("parallel",)),
        name="head_prep",
    )(u, u, u, u, g_q, g_ks, g_kw)


def _gated_norm(y, xs, z, dsk, gn):
    y = (y + dsk * xs) * _silu(z)
    ms = jnp.mean(y * y, axis=-1, keepdims=True)
    return y * lax.rsqrt(ms + RMS_EPS) * gn


def _ssd_prompt_kernel(xbc_ref, z_ref, sm_ref, cw_ref, cb_ref, dtb_ref, alog_ref, dsk_ref, gn_ref,
                       y_ref, hout_ref, xb_ref, st_ref):
    c = pl.program_id(0)
    q = SSD_CHUNK
    ssm_dim = z_ref.shape[1]
    gw = SSM_STATE
    pad = 8

    @pl.when(c == 0)
    def _():
        xb_ref[0:pad, :] = jnp.zeros((pad, xb_ref.shape[1]), F32)
        st_ref[...] = jnp.zeros(st_ref.shape, F32)

    xb_ref[pad:pad + q, :] = xbc_ref[...]
    conv = cb_ref[...]
    for k in range(CONV_K):
        conv = conv + cw_ref[k:k + 1, :] * xb_ref[pl.ds(pad - (CONV_K - 1) + k, q), :]
    xb_ref[0:pad, :] = xb_ref[q:q + pad, :]
    act = _silu(conv)
    xs = act[:, :ssm_dim]
    bm = act[:, ssm_dim:ssm_dim + SSM_GROUPS * gw].astype(BF16)
    cm = act[:, ssm_dim + SSM_GROUPS * gw:].astype(BF16)

    lane = lax.broadcasted_iota(jnp.int32, (1, LANES), 1)
    n_heads = ssm_dim // SSM_HEAD_DIM
    dt = jnp.where(lane < n_heads, _softplus(sm_ref[...] + dtb_ref[...]), 0.0)
    a = -jnp.exp(alog_ref[...])
    ri = lax.broadcasted_iota(jnp.int32, (q, q), 0)
    ci = lax.broadcasted_iota(jnp.int32, (q, q), 1)
    causal = ri >= ci
    tri = jnp.where(causal, 1.0, 0.0).astype(BF16)
    da_hi, da_mid, da_lo = _split3(dt * a)
    acum = _dot(tri, da_hi) + _dot(tri, da_mid) + _dot(tri, da_lo)
    acum_t = acum.T
    dt_t = dt.T
    e_acum = jnp.exp(acum)
    a_last = acum[q - 1:q, :]
    w_end = dt * jnp.exp(a_last - acum)
    e_last = jnp.exp(a_last)
    lo = lane < SSM_HEAD_DIM
    row_lo = lax.broadcasted_iota(jnp.int32, (LANES, 1), 0) < SSM_HEAD_DIM

    ys = []
    heads_per_group = n_heads // SSM_GROUPS
    for pr in range(n_heads // 2):
        h0, h1 = 2 * pr, 2 * pr + 1
        g = h0 // heads_per_group
        cg = cm[:, g * gw:(g + 1) * gw]
        bg = bm[:, g * gw:(g + 1) * gw]
        cb = _dot_nt(cg, bg)
        x2 = xs[:, pr * LANES:(pr + 1) * LANES]
        x2b = x2.astype(BF16)
        yd = []
        for h in (h0, h1):
            seg = acum[:, h:h + 1] - acum_t[h:h + 1, :]
            dec = jnp.where(causal, jnp.exp(jnp.where(causal, seg, 0.0)), 0.0)
            m = (cb * dec * dt_t[h:h + 1, :]).astype(BF16)
            yd.append(_dot(m, x2b))
        y_diag = jnp.where(lo, yd[0], yd[1])
        sp = st_ref[pr * LANES:(pr + 1) * LANES, :]
        ea = jnp.where(lo, e_acum[:, h0:h0 + 1], e_acum[:, h1:h1 + 1])
        y_off = _dot_nt(cg, sp.astype(BF16)) * ea
        w2 = jnp.where(lo, w_end[:, h0:h0 + 1], w_end[:, h1:h1 + 1])
        xw_t = (x2 * w2).T.astype(BF16)
        cd = jnp.where(row_lo, e_last[:, h0:h0 + 1], e_last[:, h1:h1 + 1])
        st_ref[pr * LANES:(pr + 1) * LANES, :] = sp * cd + _dot(xw_t, bg)
        ys.append(y_diag + y_off)
    y = jnp.concatenate(ys, axis=-1)
    y_ref[...] = _gated_norm(y, xs, z_ref[...], dsk_ref[...], gn_ref[...]).astype(y_ref.dtype)

    @pl.when(c == pl.num_programs(0) - 1)
    def _():
        hout_ref[...] = st_ref[...]


def _ssd_prompt(u, col, lw):
    t = u.shape[0]
    ssm_dim = lw["dsk"].shape[1]
    conv_ch = lw["conv_w"].shape[1]
    q = SSD_CHUNK
    full = lambda r, w: pl.BlockSpec((r, w), lambda c: (0, 0))
    return pl.pallas_call(
        _ssd_prompt_kernel,
        grid=(t // q,),
        in_specs=[pl.BlockSpec((q, conv_ch), lambda c: (c, col["xbc"])),
                  pl.BlockSpec((q, ssm_dim), lambda c: (c, col["z"])),
                  pl.BlockSpec((q, LANES), lambda c: (c, col["small"])),
                  full(CONV_K, conv_ch), full(1, conv_ch), full(1, LANES), full(1, LANES),
                  full(1, ssm_dim), full(1, ssm_dim)],
        out_specs=[pl.BlockSpec((q, ssm_dim), lambda c: (c, 0)),
                   pl.BlockSpec((ssm_dim, SSM_STATE), lambda c: (0, 0))],
        out_shape=[jax.ShapeDtypeStruct((t, ssm_dim), BF16),
                   jax.ShapeDtypeStruct((ssm_dim, SSM_STATE), F32)],
        scratch_shapes=[pltpu.VMEM((q + 8, conv_ch), F32),
                        pltpu.VMEM((ssm_dim, SSM_STATE), F32)],
        compiler_params=_cparams(("arbitrary",)),
        name="ssd_prompt",
    )(u, u, u, lw["conv_w"], lw["conv_b"], lw["dt_bias"], lw["a_log"], lw["dsk"], lw["ssm_norm_g"])


def _ssd_sample_kernel(xbc_ref, z_ref, sm_ref, cbuf_ref, h_ref, cw_ref, cb_ref, dtb_ref, alog_ref,
                       dsk_ref, gn_ref, exp_ref, y_ref, hout_ref):
    b = pl.program_id(0)
    ssm_dim = z_ref.shape[1]
    gw = SSM_STATE
    n_heads = ssm_dim // SSM_HEAD_DIM
    conv = cb_ref[...] + cw_ref[CONV_K - 1:CONV_K, :] * xbc_ref[pl.ds(b, 1), :]
    for k in range(CONV_K - 1):
        conv = conv + cw_ref[k:k + 1, :] * cbuf_ref[k, pl.ds(b, 1), :]
    act = _silu(conv)
    xs = act[:, :ssm_dim]
    lane = lax.broadcasted_iota(jnp.int32, (1, LANES), 1)
    dt_raw = jnp.broadcast_to(jnp.where(lane < n_heads, sm_ref[pl.ds(b, 1), :], 0.0), (8, LANES))
    dt = _softplus(_dot3(dt_raw, exp_ref[...])[0:1, :] + dtb_ref[...])
    dec = jnp.exp(dt * (-jnp.exp(alog_ref[...])))
    dtx = dt * xs
    row8 = lax.broadcasted_iota(jnp.int32, (8, 1), 0)

    def rows8(pieces):
        out = jnp.zeros((8, pieces[0].shape[1]), F32)
        for r, p in enumerate(pieces):
            out = jnp.where(row8 == r, p.astype(F32), out)
        return out.astype(BF16)

    dec8 = rows8(_split3(dec))
    dtx8 = rows8(_split3(dtx))
    ones8 = jnp.where(row8 < 3, 1.0, 0.0).astype(BF16) * jnp.ones((1, gw), BF16)
    gh = (n_heads // SSM_GROUPS) * SSM_HEAD_DIM
    ys = []
    for g in range(SSM_GROUPS):
        bg = act[:, ssm_dim + g * gw:ssm_dim + (g + 1) * gw].astype(BF16)
        cg = act[:, ssm_dim + (SSM_GROUPS + g) * gw:ssm_dim + (SSM_GROUPS + g + 1) * gw].astype(BF16)
        b8 = jnp.where(row8 < 3, 1.0, 0.0).astype(BF16) * bg
        c8 = jnp.where(row8 < 1, 1.0, 0.0).astype(BF16) * cg
        dec_col = _dot_tn(dec8[:, g * gh:(g + 1) * gh], ones8)
        dbx = _dot_tn(dtx8[:, g * gh:(g + 1) * gh], b8)
        s_new = h_ref[g * gh:(g + 1) * gh, :] * dec_col + dbx
        hout_ref[g * gh:(g + 1) * gh, :] = s_new
        ys.append(_dot_nt(c8, s_new.astype(BF16))[0:1, :])
    y = jnp.concatenate(ys, axis=-1)
    y_ref[pl.ds(b, 1), :] = _gated_norm(y, xs, z_ref[pl.ds(b, 1), :], dsk_ref[...], gn_ref[...])


def _ssd_sample(u, col, conv_t, h0, lw):
    bs = u.shape[0]
    ssm_dim = lw["dsk"].shape[1]
    conv_ch = lw["conv_w"].shape[1]
    full = lambda r, w: pl.BlockSpec((r, w), lambda b: (0, 0))
    return pl.pallas_call(
        _ssd_sample_kernel,
        grid=(bs,),
        in_specs=[pl.BlockSpec((bs, conv_ch), lambda b: (0, col["xbc"])),
                  pl.BlockSpec((bs, ssm_dim), lambda b: (0, col["z"])),
                  pl.BlockSpec((bs, LANES), lambda b: (0, col["small"])),
                  pl.BlockSpec((CONV_K - 1, bs, conv_ch), lambda b: (0, 0, 0)),
                  pl.BlockSpec((None, ssm_dim, SSM_STATE), lambda b: (b, 0, 0)),
                  full(CONV_K, conv_ch), full(1, conv_ch), full(1, ssm_dim), full(1, ssm_dim),
                  full(1, ssm_dim), full(1, ssm_dim), full(LANES, ssm_dim)],
        out_specs=[pl.BlockSpec((bs, ssm_dim), lambda b: (0, 0)),
                   pl.BlockSpec((None, ssm_dim, SSM_STATE), lambda b: (b, 0, 0))],
        out_shape=[jax.ShapeDtypeStruct((bs, ssm_dim), F32),
                   jax.ShapeDtypeStruct((bs, ssm_dim, SSM_STATE), F32)],
        compiler_params=_cparams(("arbitrary",)),
        name="ssd_sample",
    )(u, u, u, conv_t, h0, lw["conv_w"], lw["conv_b"], lw["dt_bias_e"], lw["a_log_e"], lw["dsk"],
      lw["ssm_norm_g"], lw["head_expand"])


def _compress_rows(buf_ref, xcat_ref, pe_ref, w_ref, gk_ref):
    n_blocks = xcat_ref.shape[0]
    acc = []
    for p in range(KV_COLS // LANES):
        for l in range(L_CMP):
            xl = buf_ref[p, pl.ds(l, n_blocks, stride=CMP_STRIDE), :] + pe_ref[l:l + 1, p * LANES:(p + 1) * LANES]
            xcat_ref[:, l * LANES:(l + 1) * LANES] = xl.astype(BF16)
        acc.append(_dot(xcat_ref[...], w_ref[p // 2]))
    ck = _head_rms(jnp.concatenate(acc[:2], axis=-1), gk_ref[...])
    cv = jnp.concatenate(acc[2:], axis=-1)
    return ck, cv


def _store_heads(ck, cv, ck_ref, cv_ref):
    for h in range(KV_HEADS):
        ck_ref[h] = ck[:, h * HEAD_DIM:(h + 1) * HEAD_DIM].astype(BF16)
        cv_ref[h] = cv[:, h * HEAD_DIM:(h + 1) * HEAD_DIM].astype(BF16)


def _compress_prompt_kernel(main_ref, nxt_ref, pe_ref, w_ref, gk_ref, ck_ref, cv_ref, buf_ref, xcat_ref):
    rows = main_ref.shape[0]
    for p in range(KV_COLS // LANES):
        buf_ref[p, 0:rows, :] = main_ref[:, p * LANES:(p + 1) * LANES]
        buf_ref[p, rows:rows + CMP_STRIDE, :] = nxt_ref[:, p * LANES:(p + 1) * LANES]
    ck, cv = _compress_rows(buf_ref, xcat_ref, pe_ref, w_ref, gk_ref)
    _store_heads(ck, cv, ck_ref, cv_ref)


def _compress_scratch(rows):
    return [pltpu.VMEM((KV_COLS // LANES, rows + CMP_STRIDE, LANES), F32),
            pltpu.VMEM((rows // CMP_STRIDE, L_CMP * LANES), BF16)]


def _compress_prompt(kvc_pad, lw, rows_per_step):
    t = kvc_pad.shape[0] - CMP_STRIDE
    nb = rows_per_step // CMP_STRIDE
    out = jax.ShapeDtypeStruct((KV_HEADS, t // CMP_STRIDE, HEAD_DIM), BF16)
    out_spec = pl.BlockSpec((KV_HEADS, nb, HEAD_DIM), lambda i: (0, i, 0))
    return pl.pallas_call(
        _compress_prompt_kernel,
        grid=(t // rows_per_step,),
        in_specs=[pl.BlockSpec((rows_per_step, KV_COLS), lambda i: (i, 0)),
                  pl.BlockSpec((CMP_STRIDE, KV_COLS), lambda i: ((i + 1) * nb, 0)),
                  pl.BlockSpec((L_CMP, KV_COLS), lambda i: (0, 0)),
                  pl.BlockSpec((2, L_CMP * LANES, LANES), lambda i: (0, 0, 0)),
                  pl.BlockSpec((1, LANES), lambda i: (0, 0))],
        out_specs=[out_spec, out_spec],
        out_shape=[out, out],
        scratch_shapes=_compress_scratch(rows_per_step),
        compiler_params=_cparams(("parallel",)),
        name="compress_prompt",
    )(kvc_pad, kvc_pad, lw["cmp_pe_rows"], lw["cmp_w_cat"], lw["g_kc"])


def _compress_sample_kernel(n_pg, pt_ref, *refs):
    page_refs = refs[:n_pg]
    nxt_ref, pe_ref, w_ref, gk_ref, ck_ref, cv_ref, buf_ref, xcat_ref = refs[n_pg:]
    pg = page_refs[0].shape[1]
    for p in range(KV_COLS // LANES):
        for k in range(n_pg):
            buf_ref[p, k * pg:(k + 1) * pg, :] = page_refs[k][p * LANES:(p + 1) * LANES, :].T
        buf_ref[p, n_pg * pg:n_pg * pg + CMP_STRIDE, :] = nxt_ref[p * LANES:(p + 1) * LANES, :].T[:CMP_STRIDE]
    ck, cv = _compress_rows(buf_ref, xcat_ref, pe_ref, w_ref, gk_ref)
    _store_heads(ck, cv, ck_ref, cv_ref)


def _compress_sample(cache_t, page_base, page_table, lw):
    bs, n_pages = page_table.shape
    pg = cache_t.shape[2]
    n_pg = _pick(n_pages, (2 * PAGES_PER_STEP, PAGES_PER_STEP))
    nb = n_pg * pg // CMP_STRIDE
    past = n_pages * pg

    def page_map(b, j, pt, k):
        return (page_base + pt[b, j * n_pg + k], 0, 0)

    def next_map(b, j, pt):
        return (page_base + pt[b, jnp.minimum((j + 1) * n_pg, n_pages - 1)], 0, 0)

    out = jax.ShapeDtypeStruct((bs, KV_HEADS, past // CMP_STRIDE, HEAD_DIM), BF16)
    out_spec = pl.BlockSpec((None, KV_HEADS, nb, HEAD_DIM), lambda b, j, pt: (b, 0, j, 0))
    grid_spec = pltpu.PrefetchScalarGridSpec(
        num_scalar_prefetch=1,
        grid=(bs, n_pages // n_pg),
        in_specs=([pl.BlockSpec((None, KV_COLS, pg), functools.partial(page_map, k=k)) for k in range(n_pg)]
                  + [pl.BlockSpec((None, KV_COLS, pg), next_map),
                     pl.BlockSpec((L_CMP, KV_COLS), lambda b, j, pt: (0, 0)),
                     pl.BlockSpec((2, L_CMP * LANES, LANES), lambda b, j, pt: (0, 0, 0)),
                     pl.BlockSpec((1, LANES), lambda b, j, pt: (0, 0))]),
        out_specs=[out_spec, out_spec],
        scratch_shapes=_compress_scratch(n_pg * pg),
    )
    return pl.pallas_call(
        functools.partial(_compress_sample_kernel, n_pg),
        grid_spec=grid_spec,
        out_shape=[out, out],
        compiler_params=_cparams(("parallel", "parallel")),
        name="compress_sample",
    )(page_table, *([cache_t] * (n_pg + 1)), lw["cmp_pe_rows"], lw["cmp_w_cat"], lw["g_kc"])


def _cover_matrix(nc, n_sel_blocks, width):
    n = lax.broadcasted_iota(jnp.int32, (nc, width), 0) * CMP_STRIDE
    s = lax.broadcasted_iota(jnp.int32, (nc, width), 1)
    hit = (n < s * L_SEL + L_SEL) & (n + L_CMP > s * L_SEL) & (s < n_sel_blocks)
    return jnp.where(hit, 1.0, 0.0).astype(BF16)


def _block_scores(imp, q_pos, n_sel_blocks):
    blk = lax.broadcasted_iota(jnp.int32, imp.shape, 1)
    valid = blk * L_SEL <= q_pos
    cur = q_pos // L_SEL
    forced = (blk == 0) | (blk == cur) | (blk == cur - 1)
    score = jnp.where(valid & forced, SEL_FORCE, jnp.where(valid, imp, -1.0))
    return jnp.where(blk < n_sel_blocks, score, -2.0)


def _cmp_prompt_kernel(q_ref, ck_ref, cv_ref, gates_ref, oc_ref, sel_ref):
    i = pl.program_id(0)
    tq = q_ref.shape[0]
    nc = ck_ref.shape[1]
    n_sel_blocks = nc * CMP_STRIDE // L_SEL
    q_pos = i * tq + lax.broadcasted_iota(jnp.int32, (tq, 1), 0)
    n_end = lax.broadcasted_iota(jnp.int32, (1, nc), 1) * CMP_STRIDE + (L_CMP - 1)
    mask = (n_end <= q_pos) & (n_end < nc * CMP_STRIDE)
    cover = _cover_matrix(nc, n_sel_blocks, LANES)
    q = q_ref[...]
    outs = []
    pieces = []
    for h in range(KV_HEADS):
        q4 = jnp.concatenate([q[:, (h * GQA + g) * HEAD_DIM:(h * GQA + g + 1) * HEAD_DIM] for g in range(GQA)],
                             axis=0)
        s4 = _dot_nt(q4, ck_ref[h])
        ps = [_masked_softmax(s4[g * tq:(g + 1) * tq], mask) for g in range(GQA)]
        o4 = _dot(jnp.concatenate([p.astype(BF16) for p in ps], axis=0), cv_ref[h])
        gates = gates_ref[h]
        outs += [o4[g * tq:(g + 1) * tq] * gates[:, 3 * g:3 * g + 1] for g in range(GQA)]
        pieces += list(_split3(ps[0] + ps[1] + ps[2] + ps[3]))
    oc_ref[...] = jnp.concatenate(outs, axis=-1)
    imp_all = _dot(jnp.concatenate(pieces, axis=0), cover)
    for h in range(KV_HEADS):
        imp = sum(imp_all[(3 * h + r) * tq:(3 * h + r + 1) * tq] for r in range(3))
        score_t = _block_scores(imp, q_pos, n_sel_blocks).T
        sel_ref[h] = _top_k_mask_t(score_t, N_SEL, n_sel_blocks).astype(sel_ref.dtype)


def _cmp_prompt(qn, ck, cv, gates_h, t):
    nsa = qn.shape[1]
    nc = ck.shape[1]
    tq = Q_TILE
    kv_spec = pl.BlockSpec((KV_HEADS, nc, HEAD_DIM), lambda i: (0, 0, 0))
    return pl.pallas_call(
        _cmp_prompt_kernel,
        grid=(t // tq,),
        in_specs=[pl.BlockSpec((tq, nsa), lambda i: (i, 0)), kv_spec, kv_spec,
                  pl.BlockSpec((KV_HEADS, tq, LANES), lambda i: (0, i, 0))],
        out_specs=[pl.BlockSpec((tq, nsa), lambda i: (i, 0)),
                   pl.BlockSpec((KV_HEADS, LANES, tq), lambda i: (0, 0, i))],
        out_shape=[jax.ShapeDtypeStruct((t, nsa), F32),
                   jax.ShapeDtypeStruct((KV_HEADS, LANES, t), BF16)],
        compiler_params=_cparams(("parallel",)),
        name="cmp_topk_prompt",
    )(qn, ck, cv, gates_h)


def _tile_softmax_t(s_t, bias_t, tq, stats, p_ref):
    new_stats, alphas = [], []
    for g in range(GQA):
        m, l = stats[g]
        sb = s_t[:, g * tq:(g + 1) * tq] + bias_t
        m_new = jnp.maximum(m, jnp.max(sb, axis=0, keepdims=True))
        alpha = jnp.exp2(m - m_new)
        p = jnp.exp2(sb - m_new)
        p_ref[:, g * tq:(g + 1) * tq] = p.astype(BF16)
        new_stats.append((m_new, alpha * l + jnp.sum(p, axis=0, keepdims=True)))
        alphas.append(alpha)
    return tuple(new_stats), jnp.concatenate(alphas, axis=1)


def _stack_q_t(qt_ref):
    return jnp.concatenate([qt_ref[g * HEAD_DIM:(g + 1) * HEAD_DIM, :] for g in range(GQA)], axis=1)


def _finish_t(acc_t, stats, gates, branch, tq):
    o_t = acc_t * jnp.concatenate([1.0 / l for _, l in stats], axis=1)
    lane = lax.broadcasted_iota(jnp.int32, (1, LANES), 1)
    outs = []
    for pair in range(GQA // 2):
        g0, g1 = 2 * pair, 2 * pair + 1
        two = jnp.concatenate([o_t[:, g0 * tq:(g0 + 1) * tq], o_t[:, g1 * tq:(g1 + 1) * tq]], axis=0)
        gate = jnp.where(lane < HEAD_DIM, gates[:, 3 * g0 + branch:3 * g0 + branch + 1],
                         gates[:, 3 * g1 + branch:3 * g1 + branch + 1])
        outs.append(two.T * gate)
    return jnp.concatenate(outs, axis=1)


def _init_stats(tq):
    return tuple((jnp.full((1, tq), NEG_BIG, F32), jnp.zeros((1, tq), F32)) for _ in range(GQA))


def _sel_prompt_kernel(qt_ref, k_ref, vt_ref, sel_ref, gates_ref, o_ref, p0_ref, p1_ref, acc0_ref, acc1_ref):
    p_refs, acc_refs = (p0_ref, p1_ref), (acc0_ref, acc1_ref)
    i = pl.program_id(1)
    tq = sel_ref.shape[1]
    tk = SEL_KEY_TILE
    bpt = tk // L_SEL
    q4t = _stack_q_t(qt_ref)
    sel_t = sel_ref[...]
    q_pos = i * tq + lax.broadcasted_iota(jnp.int32, (1, tq), 1)
    blk_of_key = (lax.broadcasted_iota(jnp.int32, (tk, LANES), 1)
                  - lax.broadcasted_iota(jnp.int32, (tk, LANES), 0) // L_SEL)
    key_off = lax.broadcasted_iota(jnp.int32, (tk, 1), 0)
    n_tiles = (i * tq + tq - 1) // tk + 1
    last_tile = k_ref.shape[0] // tk - 1
    for acc_ref in acc_refs:
        acc_ref[...] = jnp.zeros(acc_ref.shape, F32)

    def tile(j, stats, p_ref, acc_ref):
        live = (j < n_tiles).astype(jnp.int32)
        jc = jnp.minimum(j, last_tile)
        at = pl.ds(pl.multiple_of(jc * tk, tk), tk)
        expand_t = jnp.where(blk_of_key == jc * bpt, 1.0, 0.0).astype(BF16)
        picked = _dot(expand_t, sel_t)
        q_lim = q_pos * live - (1 - live)
        bias_t = jnp.where((picked > 0.5) & (key_off + jc * tk <= q_lim), 0.0, NEG_BIG)
        stats, alpha = _tile_softmax_t(_dot(k_ref[at, :], q4t), bias_t, tq, stats, p_ref)
        acc_ref[...] = alpha * acc_ref[...] + _dot(vt_ref[:, at], p_ref[...])
        return stats

    def body(jj, carry):
        return tuple(tile(2 * jj + c, carry[c], p_refs[c], acc_refs[c]) for c in range(2))

    chains = lax.fori_loop(0, (n_tiles + 1) // 2, body, (_init_stats(tq), _init_stats(tq)))
    merged, scale = [], [[], []]
    for (m0, l0), (m1, l1) in zip(*chains):
        m = jnp.maximum(m0, m1)
        w0, w1 = jnp.exp2(m0 - m), jnp.exp2(m1 - m)
        merged.append((m, w0 * l0 + w1 * l1))
        scale[0].append(w0)
        scale[1].append(w1)
    acc = sum(jnp.concatenate(scale[c], axis=1) * acc_refs[c][...] for c in range(2))
    o_ref[...] = _finish_t(acc, merged, gates_ref[...], 1, tq)


def _attn_specs(t, tq):
    return [pl.BlockSpec((None, GQA * HEAD_DIM, tq), lambda h, i: (h, 0, i)),
            pl.BlockSpec((None, t, HEAD_DIM), lambda h, i: (h, 0, 0)),
            pl.BlockSpec((None, HEAD_DIM, t), lambda h, i: (h, 0, 0)),
            pl.BlockSpec((None, tq, LANES), lambda h, i: (h, i, 0))]


def _sel_prompt(q_t, k_h, v_t, sel_t, gates_h, t):
    tq = _pick(t, (SEL_Q_TILE, Q_TILE))
    q_spec, k_spec, v_spec, g_spec = _attn_specs(t, tq)
    return pl.pallas_call(
        _sel_prompt_kernel,
        grid=(KV_HEADS, t // tq),
        in_specs=[q_spec, k_spec, v_spec, pl.BlockSpec((None, LANES, tq), lambda h, i: (h, 0, i)), g_spec],
        out_specs=pl.BlockSpec((tq, GQA * HEAD_DIM), lambda h, i: (i, h)),
        out_shape=jax.ShapeDtypeStruct((t, ATT_HEADS * HEAD_DIM), F32),
        scratch_shapes=([pltpu.VMEM((SEL_KEY_TILE, GQA * tq), BF16)] * 2
                        + [pltpu.VMEM((HEAD_DIM, GQA * tq), F32)] * 2),
        compiler_params=_cparams(("parallel", "arbitrary")),
        name="sel_prompt",
    )(q_t, k_h, v_t, sel_t, gates_h)


def _win_prompt_kernel(qt_ref, k_ref, vt_ref, gates_ref, o_ref, p_ref):
    i = pl.program_id(1)
    tq = gates_ref.shape[0]
    span = WINDOW + tq
    at = pl.ds(pl.multiple_of(jnp.maximum(i * tq - WINDOW, 0), tq), span)
    q_pos = i * tq + lax.broadcasted_iota(jnp.int32, (1, tq), 1)
    k_pos = jnp.maximum(i * tq - WINDOW, 0) + lax.broadcasted_iota(jnp.int32, (span, 1), 0)
    bias_t = jnp.where((k_pos <= q_pos) & (k_pos >= q_pos - WINDOW), 0.0, NEG_BIG)
    stats, _ = _tile_softmax_t(_dot(k_ref[at, :], _stack_q_t(qt_ref)), bias_t, tq, _init_stats(tq), p_ref)
    o_ref[...] = _finish_t(_dot(vt_ref[:, at], p_ref[...]), stats, gates_ref[...], 2, tq)


def _win_prompt(q_t, k_h, v_t, gates_h, t):
    tq = _pick(t, (SEL_Q_TILE, Q_TILE))
    return pl.pallas_call(
        _win_prompt_kernel,
        grid=(KV_HEADS, t // tq),
        in_specs=_attn_specs(t, tq),
        out_specs=pl.BlockSpec((tq, GQA * HEAD_DIM), lambda h, i: (i, h)),
        out_shape=jax.ShapeDtypeStruct((t, ATT_HEADS * HEAD_DIM), F32),
        scratch_shapes=[pltpu.VMEM((WINDOW + tq, GQA * tq), BF16)],
        compiler_params=_cparams(("parallel", "arbitrary")),
        name="win_prompt",
    )(q_t, k_h, v_t, gates_h)


def _own_head_rows(parts, width):
    row_head = lax.broadcasted_iota(jnp.int32, (ATT_HEADS, 1), 0) // GQA
    out = jnp.zeros((ATT_HEADS, width), F32)
    for h in range(KV_HEADS):
        out = jnp.where(row_head == h, parts[h], out)
    return out


def _cmp_sample_kernel(past, q_ref, ck_ref, cv_ref, g3_ref, oc_ref, sel_ref):
    nc = ck_ref.shape[1]
    width = sel_ref.shape[1]
    n_sel_blocks = past // L_SEL + 1
    q = q_ref[...]
    n_end = lax.broadcasted_iota(jnp.int32, (1, nc), 1) * CMP_STRIDE + (L_CMP - 1)
    mask = n_end <= past
    s = _own_head_rows([_dot_nt(q, ck_ref[h]) for h in range(KV_HEADS)], nc)
    p = _masked_softmax(s, mask)
    pb = p.astype(BF16)
    o = _own_head_rows([_dot(pb, cv_ref[h]) for h in range(KV_HEADS)], HEAD_DIM)
    oc_ref[...] = o * g3_ref[:, 0:1]
    row_head = lax.broadcasted_iota(jnp.int32, (ATT_HEADS, 1), 0) // GQA
    row8 = lax.broadcasted_iota(jnp.int32, (8, 1), 0)
    p_sum = jnp.zeros((8, nc), F32)
    for h in range(KV_HEADS):
        ph = jnp.sum(jnp.where(row_head == h, p, 0.0), axis=0, keepdims=True)
        p_sum = jnp.where(row8 == h, ph, p_sum)
    imp = _dot3(p_sum, _cover_matrix(nc, n_sel_blocks, width))
    q_pos = jnp.full((8, 1), past, jnp.int32)
    sel_ref[...] = _top_k_mask(_block_scores(imp, q_pos, n_sel_blocks), N_SEL)


def _cmp_sample(q3, ck_s, cv_s, g3, past):
    bs = q3.shape[0]
    nc = ck_s.shape[2]
    width = -(-(past // L_SEL + 1) // LANES) * LANES
    kv_spec = pl.BlockSpec((None, KV_HEADS, nc, HEAD_DIM), lambda b: (b, 0, 0, 0))
    return pl.pallas_call(
        functools.partial(_cmp_sample_kernel, past),
        grid=(bs,),
        in_specs=[pl.BlockSpec((None, ATT_HEADS, HEAD_DIM), lambda b: (b, 0, 0)), kv_spec, kv_spec,
                  pl.BlockSpec((None, ATT_HEADS, LANES), lambda b: (b, 0, 0))],
        out_specs=[pl.BlockSpec((None, ATT_HEADS, HEAD_DIM), lambda b: (b, 0, 0)),
                   pl.BlockSpec((None, 8, width), lambda b: (b, 0, 0))],
        out_shape=[jax.ShapeDtypeStruct((bs, ATT_HEADS, HEAD_DIM), F32),
                   jax.ShapeDtypeStruct((bs, 8, width), F32)],
        compiler_params=_cparams(("parallel",)),
        name="cmp_topk_sample",
    )(q3, ck_s, cv_s, g3)


def _diag_heads(x):
    return _own_head_rows([x[:, h * HEAD_DIM:(h + 1) * HEAD_DIM] for h in range(KV_HEADS)], HEAD_DIM)


def _sel_win_sample_kernel(past, pt_ref, *refs):
    n_pg = PAGES_PER_STEP
    page_refs = refs[:n_pg]
    (qbd_ref, selg_ref, selnew_ref, knew_ref, win_ref, wnew_ref, oc_ref, g3_ref,
     o_ref, m_ref, l_ref, acc_ref, kv_ref) = refs[n_pg:]
    j = pl.program_id(1)
    pg = page_refs[0].shape[1]
    qbd = qbd_ref[...]
    bpp = pg // L_SEL

    @pl.when(j == 0)
    def _():
        m_ref[...] = jnp.full(m_ref.shape, NEG_BIG, F32)
        l_ref[...] = jnp.zeros(l_ref.shape, F32)
        acc_ref[...] = jnp.zeros(acc_ref.shape, F32)

    sel = selg_ref[...].astype(BF16)
    n_blk = n_pg * bpp
    expand = jnp.where(lax.broadcasted_iota(jnp.int32, (n_blk, n_pg * pg), 0)
                       == lax.broadcasted_iota(jnp.int32, (n_blk, n_pg * pg), 1) // L_SEL, 1.0, 0.0).astype(BF16)
    picked = _dot(sel, expand)
    for k in range(n_pg):
        kv_ref[:, k * pg:(k + 1) * pg] = page_refs[k][...].astype(BF16)
    m, l, acc = m_ref[...], l_ref[...], acc_ref[...]
    s = _dot(qbd, kv_ref[0:K_COLS, :]) + jnp.where(picked > 0.5, 0.0, NEG_BIG)
    m_new = jnp.maximum(m, jnp.max(s, axis=-1, keepdims=True))
    alpha = jnp.exp(m - m_new)
    p = jnp.exp(s - m_new)
    l = alpha * l + jnp.sum(p, axis=-1, keepdims=True)
    acc = alpha * acc + _dot_nt(p.astype(BF16), kv_ref[K_COLS:, :])
    m = m_new
    m_ref[...], l_ref[...], acc_ref[...] = m, l, acc

    @pl.when(j == pl.num_programs(1) - 1)
    def _():
        qf = qbd.astype(F32)
        kn = knew_ref[...]
        s_new = jnp.sum(qf * kn[:, :K_COLS].astype(BF16).astype(F32), axis=-1, keepdims=True)
        ok_new = selnew_ref[:, 0:1] > 0.5
        s_new = jnp.where(ok_new, s_new, NEG_BIG)
        m2 = jnp.maximum(m, s_new)
        a2 = jnp.exp(m - m2)
        p_new = jnp.where(ok_new, jnp.exp(s_new - m2), 0.0)
        l2 = a2 * l + p_new
        acc2 = a2 * acc + p_new.astype(BF16).astype(F32) * kn[:, K_COLS:].astype(BF16).astype(F32)
        o_s = _diag_heads(acc2 / jnp.maximum(l2, 1e-30))
        w = win_ref[...].astype(BF16)
        w_buf = w.shape[1]
        wn = wnew_ref[...]
        w_pos = past - w_buf + lax.broadcasted_iota(jnp.int32, (1, w_buf), 1)
        ok_w = (w_pos <= past) & (w_pos >= past - WINDOW) & (w_pos >= 0)
        s_w = jnp.where(ok_w, _dot(qbd, w[:K_COLS, :]), -jnp.inf)
        s_wn = jnp.sum(qf * wn[:, :K_COLS].astype(BF16).astype(F32), axis=-1, keepdims=True)
        m_w = jnp.maximum(jnp.max(s_w, axis=-1, keepdims=True), s_wn)
        p_w = jnp.where(ok_w, jnp.exp(s_w - m_w), 0.0)
        p_wn = jnp.exp(s_wn - m_w)
        l_w = jnp.sum(p_w, axis=-1, keepdims=True) + p_wn
        p_w = p_w / l_w
        p_wn = p_wn / l_w
        o_w = _diag_heads(_dot_nt(p_w.astype(BF16), w[K_COLS:, :])
                          + p_wn.astype(BF16).astype(F32) * wn[:, K_COLS:].astype(BF16).astype(F32))
        g3 = g3_ref[...]
        o_ref[...] = oc_ref[...] + g3[:, 1:2] * o_s + g3[:, 2:3] * o_w


def _sel_win_sample(cache_t, page_base, page_table, qbd, sel_groups, sel_new, kvs_new, win_t, win_base,
                    kvw_new, oc, g3, past):
    bs, n_pages = page_table.shape
    pg = cache_t.shape[2]
    n_pg = PAGES_PER_STEP
    bps = n_pg * pg // L_SEL
    w_buf = win_t.shape[2]

    def page_map(b, j, pt, k):
        return (page_base + pt[b, j * n_pg + k], 0, 0)

    per_b = lambda *shape: pl.BlockSpec((None,) + shape, lambda b, j, pt: (b,) + (0,) * len(shape))
    grid_spec = pltpu.PrefetchScalarGridSpec(
        num_scalar_prefetch=1,
        grid=(bs, n_pages // n_pg),
        in_specs=([pl.BlockSpec((None, KV_COLS, pg), functools.partial(page_map, k=k)) for k in range(n_pg)]
                  + [per_b(ATT_HEADS, K_COLS),
                     pl.BlockSpec((None, None, ATT_HEADS, bps), lambda b, j, pt: (b, j, 0, 0)),
                     per_b(ATT_HEADS, LANES), per_b(1, KV_COLS),
                     pl.BlockSpec((None, KV_COLS, w_buf), lambda b, j, pt: (win_base + b, 0, 0)),
                     per_b(1, KV_COLS), per_b(ATT_HEADS, HEAD_DIM), per_b(ATT_HEADS, LANES)]),
        out_specs=per_b(ATT_HEADS, HEAD_DIM),
        scratch_shapes=[pltpu.VMEM((ATT_HEADS, 1), F32), pltpu.VMEM((ATT_HEADS, 1), F32),
                        pltpu.VMEM((ATT_HEADS, K_COLS), F32), pltpu.VMEM((KV_COLS, n_pg * pg), BF16)],
    )
    return pl.pallas_call(
        functools.partial(_sel_win_sample_kernel, past),
        grid_spec=grid_spec,
        out_shape=jax.ShapeDtypeStruct((bs, ATT_HEADS, HEAD_DIM), F32),
        compiler_params=_cparams(("parallel", "arbitrary")),
        name="sel_win_sample",
    )(page_table, *([cache_t] * n_pg), qbd, sel_groups, sel_new, kvs_new, win_t, kvw_new, oc, g3)


def _column_layout(d_model):
    ssm_dim = d_model // 2
    conv_ch = ssm_dim + 2 * SSM_GROUPS * SSM_STATE
    nsa = d_model - ssm_dim
    n_heads = ssm_dim // SSM_HEAD_DIM
    sizes = (ssm_dim, conv_ch, n_heads, nsa, KV_COLS, KV_COLS, KV_COLS, 3 * ATT_HEADS)
    offs = [0]
    for s in sizes:
        offs.append(offs[-1] + s)
    src = dict(zip(("z", "xbc", "dt", "q", "kvc", "kvs", "kvw", "gt"), zip(offs[:-1], sizes)))
    order = ("xbc", "z", "q", "kvc", "kvs", "kvw")
    col, start = {}, {}
    pos = 0
    for name in order:
        o, w = src[name]
        assert pos % w == 0
        col[name] = pos // w
        start[name] = pos
        pos += w
    col["small"] = pos // LANES
    start["small"] = pos
    return src, order, col, start, pos + LANES


def _layer_weights(l, p, d_model):
    src, order, col, start, n_cols = _column_layout(d_model)
    w_in = p["w_in"][l]
    n_heads = src["dt"][1]
    small = jnp.concatenate([w_in[:, src["dt"][0]:src["dt"][0] + n_heads],
                             w_in[:, src["gt"][0]:src["gt"][0] + src["gt"][1]]], axis=1)
    small = jnp.pad(small, ((0, 0), (0, LANES - small.shape[1])))
    w_in_packed = jnp.concatenate([w_in[:, src[n][0]:src[n][0] + src[n][1]] for n in order] + [small],
                                  axis=1).astype(BF16)
    ssm_dim = src["z"][1]
    rep = lambda v: jnp.repeat(v, SSM_HEAD_DIM)[None, :]
    pad_l = lambda v: jnp.pad(v, (0, LANES - v.shape[0]))[None, :]
    tile2 = lambda v: jnp.tile(v, 2)[None, :]
    cw = jnp.transpose(p["cmp_w"][l], (1, 0, 2, 3))
    zero = jnp.zeros_like(cw)
    pairs = jnp.concatenate([jnp.concatenate([cw, zero], axis=-1),
                             jnp.concatenate([zero, cw], axis=-1)], axis=-2)
    w_cat = jnp.transpose(pairs, (1, 0, 2, 3)).reshape(2, L_CMP * LANES, LANES).astype(BF16)
    pe = p["cmp_pe"][l]
    pe_rows = jnp.broadcast_to(pe[:, :, None, :], (L_CMP, 2, KV_HEADS, HEAD_DIM)).reshape(L_CMP, KV_COLS)
    head_expand = (jnp.arange(LANES)[:, None] == jnp.arange(ssm_dim)[None, :] // SSM_HEAD_DIM).astype(BF16)
    return {
        "col": col, "start": start,
        "norm1_g": p["norm1_g"][l][None, :], "w_in": w_in_packed,
        "conv_w": p["conv_w"][l], "conv_b": p["conv_b"][l][None, :],
        "dt_bias": pad_l(p["dt_bias"][l]), "a_log": pad_l(p["a_log"][l]),
        "dt_bias_e": rep(p["dt_bias"][l]), "a_log_e": rep(p["a_log"][l]),
        "dsk": rep(p["d_skip"][l]), "ssm_norm_g": p["ssm_norm_g"][l][None, :],
        "head_expand": head_expand,
        "g_q": tile2(p["q_norm_g"][l]), "g_kc": tile2(p["k_norm_g"][l, 0]),
        "g_ks": tile2(p["k_norm_g"][l, 1]), "g_kw": tile2(p["k_norm_g"][l, 2]),
        "cmp_pe_rows": pe_rows, "cmp_w_cat": w_cat,
        "w_out": p["w_out"][l].astype(BF16), "norm2_g": p["norm2_g"][l][None, :],
        "w_gu": p["w_gu"][l].astype(BF16), "w_down": p["w_down"][l].astype(BF16),
    }


def _pick(n, prefs):
    for c in prefs:
        if n % c == 0:
            return c
    return n


def _dense_tail(x, y_ssd, atts, lw, tm):
    d = x.shape[1]
    h = _out_proj(x, y_ssd, atts, lw["w_out"], tm, _pick(d, (512,)))
    d_ff = lw["w_down"].shape[0]
    act = _ffn_up(h, lw["norm2_g"], lw["w_gu"], tm, _pick(d_ff, (512, 256, 128)))
    return _ffn_down(act, lw["w_down"], h, tm, _pick(d, (512,)))


def kernel(x_prompt, x_sample, cache_cmp_kv, cache_slc_kv, state_win_kv, state_ssm, state_conv, page_table,
           norm1_g, w_in, conv_w, conv_b, dt_bias, a_log, d_skip, ssm_norm_g, q_norm_g, k_norm_g,
           cmp_pe, cmp_w, w_out, norm2_g, w_gu, w_down):
    params = dict(norm1_g=norm1_g, w_in=w_in, conv_w=conv_w, conv_b=conv_b, dt_bias=dt_bias, a_log=a_log,
                  d_skip=d_skip, ssm_norm_g=ssm_norm_g, q_norm_g=q_norm_g, k_norm_g=k_norm_g, cmp_pe=cmp_pe,
                  cmp_w=cmp_w, w_out=w_out, norm2_g=norm2_g, w_gu=w_gu, w_down=w_down)
    bp, t, d = x_prompt.shape
    bs, dec_t, _ = x_sample.shape
    depth = w_in.shape[0]
    n_pool, pg = cache_cmp_kv.shape[1:3]
    n_pages = page_table.shape[1]
    past = n_pages * pg
    w_buf = state_win_kv.shape[2]
    assert bp == 1 and dec_t == 1
    assert t % (PAGES_PER_STEP * pg) == 0 and past % (PAGES_PER_STEP * pg) == 0 and t // L_SEL <= LANES
    ssm_dim = d // 2
    n_heads = ssm_dim // SSM_HEAD_DIM
    kv_shape = (2, KV_HEADS, HEAD_DIM)

    xp = x_prompt[0]
    xs = x_sample[:, 0]
    feature_major = lambda a: jnp.transpose(a, (0, 1, 3, 4, 5, 2)).reshape(a.shape[0] * a.shape[1], KV_COLS,
                                                                           a.shape[2])
    cmp_pages_t = feature_major(cache_cmp_kv)
    slc_pages_t = feature_major(cache_slc_kv)
    win_t = feature_major(state_win_kv)
    cmp_rows = _pick(t, (2 * PAGES_PER_STEP * pg, PAGES_PER_STEP * pg))
    outs = {k: [] for k in ("cmp_p", "cmp_s", "slc_p", "slc_s", "win_p", "win_s", "ssm_p", "ssm_s",
                            "conv_p", "conv_s")}
    tm_p = _pick(t, (512, 256, 128))
    for l in range(depth):
        lw = _layer_weights(l, params, d)
        col, start = lw["col"], lw["start"]
        n_cols = lw["w_in"].shape[1]
        tn_in = _pick(n_cols, (1152, 640, 384, 128))

        u = _norm_matmul(xp, lw["norm1_g"], lw["w_in"], tm_p, tn_in)
        qn, kvs_n, kvw_n, k_s, k_w, gates_h, q_t, vt_s, vt_w = _prep(u, col, lw["g_q"], lw["g_ks"], lw["g_kw"],
                                                                     _pick(t, (256, 128)), True)
        y_ssd, h_fin = _ssd_prompt(u, col, lw)
        kvc = u[:, start["kvc"]:start["kvc"] + KV_COLS]
        xbc_tail = u[t - (CONV_K - 1):, start["xbc"]:start["xbc"] + lw["conv_w"].shape[1]]
        ck, cv = _compress_prompt(jnp.pad(kvc, ((0, CMP_STRIDE), (0, 0))), lw, cmp_rows)
        o_c, sel_t = _cmp_prompt(qn, ck, cv, gates_h, t)
        o_s = _sel_prompt(q_t, k_s, vt_s, sel_t, gates_h, t)
        o_w = _win_prompt(q_t, k_w, vt_w, gates_h, t)
        xp = _dense_tail(xp, y_ssd, [o_c, o_s, o_w], lw, tm_p)
        outs["cmp_p"].append(kvc.reshape((1, t) + kv_shape))
        outs["slc_p"].append(kvs_n.reshape((1, t) + kv_shape))
        outs["win_p"].append(kvw_n[t - min(WINDOW, t):].reshape((1, min(WINDOW, t)) + kv_shape))
        outs["ssm_p"].append(h_fin.reshape(1, n_heads, SSM_HEAD_DIM, SSM_STATE))
        outs["conv_p"].append(xbc_tail[None])

        us = _norm_matmul(xs, lw["norm1_g"], lw["w_in"], bs, tn_in)
        qn_s, kvs_s, kvw_s, _, _, gates_s = _prep(us, col, lw["g_q"], lw["g_ks"], lw["g_kw"], bs, False)
        conv_t = jnp.transpose(state_conv[l], (1, 0, 2))
        y_ssd_s, h_new = _ssd_sample(us, col, conv_t, state_ssm[l].reshape(bs, ssm_dim, SSM_STATE), lw)
        xbc_s = us[:, start["xbc"]:start["xbc"] + lw["conv_w"].shape[1]]
        kvc_s = us[:, start["kvc"]:start["kvc"] + KV_COLS]
        ck_s, cv_s = _compress_sample(cmp_pages_t, l * n_pool, page_table, lw)
        q3 = qn_s.reshape(bs, ATT_HEADS, HEAD_DIM)
        g3 = jnp.transpose(gates_s[:, :, :3 * GQA].reshape(KV_HEADS, bs, GQA, 3), (1, 0, 2, 3))
        g3 = jnp.pad(g3.reshape(bs, ATT_HEADS, 3), ((0, 0), (0, 0), (0, LANES - 3)))
        oc_s, sel_s = _cmp_sample(q3, ck_s, cv_s, g3, past)
        sel16 = jnp.repeat(sel_s[:, :KV_HEADS], GQA, axis=1)
        bps = PAGES_PER_STEP * pg // L_SEL
        sel_groups = jnp.transpose(sel16[:, :, :past // L_SEL].reshape(bs, ATT_HEADS, past // L_SEL // bps, bps),
                                   (0, 2, 1, 3))
        sel_new = jnp.pad(sel16[:, :, past // L_SEL:past // L_SEL + 1], ((0, 0), (0, 0), (0, LANES - 1)))
        head_of_col = jnp.arange(K_COLS) // HEAD_DIM
        qbd = jnp.where(head_of_col[None, None, :] == (jnp.arange(ATT_HEADS) // GQA)[None, :, None],
                        jnp.tile(q3, (1, 1, KV_HEADS)), jnp.zeros((), BF16))
        o_att_s = _sel_win_sample(slc_pages_t, l * n_pool, page_table, qbd, sel_groups, sel_new,
                                  kvs_s[:, None, :], win_t, l * bs, kvw_s[:, None, :], oc_s, g3, past)
        xs = _dense_tail(xs, y_ssd_s.astype(BF16), [o_att_s.reshape(bs, ATT_HEADS * HEAD_DIM)], lw, bs)
        outs["cmp_s"].append(kvc_s.reshape((bs, 1) + kv_shape))
        outs["slc_s"].append(kvs_s.reshape((bs, 1) + kv_shape))
        outs["win_s"].append(jnp.concatenate([state_win_kv[l], kvw_s.reshape((bs, 1) + kv_shape)],
                                             axis=1)[:, -w_buf:])
        outs["ssm_s"].append(h_new.reshape(bs, n_heads, SSM_HEAD_DIM, SSM_STATE))
        outs["conv_s"].append(jnp.concatenate([state_conv[l], xbc_s[:, None, :]], axis=1)[:, -(CONV_K - 1):])

    st = lambda k: jnp.stack(outs[k])
    return (xp[None], xs[:, None], st("cmp_p"), st("cmp_s"), st("slc_p"), st("slc_s"), st("win_p"),
            st("win_s"), st("ssm_p"), st("ssm_s"), st("conv_p"), st("conv_s"))
```

```python
import functools

import jax
import jax.numpy as jnp
from jax import lax
from jax.experimental import pallas as pl
from jax.experimental.pallas import tpu as pltpu

F32 = jnp.float32
BF16 = jnp.bfloat16

RMS_EPS = 1e-6
SSM_HEAD_DIM = 64
SSM_GROUPS = 4
SSM_STATE = 128
CONV_K = 4
SSD_CHUNK = 128
HEAD_DIM = 64
KV_HEADS = 4
GQA = 4
ATT_HEADS = KV_HEADS * GQA
CMP_STRIDE = 16
L_CMP = 32
L_SEL = 64
N_SEL = 16
WINDOW = 512
SEL_FORCE = 1e9
KV_COLS = 2 * KV_HEADS * HEAD_DIM
K_COLS = KV_HEADS * HEAD_DIM
V_ROWS = HEAD_DIM + 16

LANES = 128
Q_TILE = 128
SEL_Q_TILE = 256
SEL_KEY_TILE = 512
PAGES_PER_STEP = 16
VMEM_LIMIT = 56 * 1024 * 1024
NEG_BIG = -1e30
LOG2_E = 1.4426950408889634


def _cparams(sem):
    return pltpu.CompilerParams(dimension_semantics=sem, vmem_limit_bytes=VMEM_LIMIT)


def _sigmoid(x):
    return 1.0 / (1.0 + jnp.exp(-x))


def _silu(x):
    return x * _sigmoid(x)


def _softplus(x):
    return jnp.maximum(x, 0.0) + jnp.log(1.0 + jnp.exp(-jnp.abs(x)))


def _split3(x):
    hi = x.astype(BF16)
    r = x - hi.astype(F32)
    mid = r.astype(BF16)
    lo = (r - mid.astype(F32)).astype(BF16)
    return hi, mid, lo


def _dot(a, b):
    return jnp.dot(a, b, preferred_element_type=F32)


def _dot_nt(a, b):
    return lax.dot_general(a, b, (((1,), (1,)), ((), ())), preferred_element_type=F32)


def _dot_tn(a, b):
    return lax.dot_general(a, b, (((0,), (0,)), ((), ())), preferred_element_type=F32)


def _dot3(x, m):
    hi, mid, lo = _split3(x)
    return _dot(hi, m) + _dot(mid, m) + _dot(lo, m)


def _head_rms(x, g2):
    lane = lax.broadcasted_iota(jnp.int32, (1, LANES), 1)
    lo = lane < HEAD_DIM
    outs = []
    for j in range(x.shape[1] // LANES):
        xt = x[:, j * LANES:(j + 1) * LANES]
        sq = xt * xt
        s_lo = jnp.sum(jnp.where(lo, sq, 0.0), axis=-1, keepdims=True)
        s_hi = jnp.sum(jnp.where(lo, 0.0, sq), axis=-1, keepdims=True)
        ms = jnp.where(lo, s_lo, s_hi) * (1.0 / HEAD_DIM)
        outs.append(xt * lax.rsqrt(ms + RMS_EPS) * g2)
    return outs[0] if len(outs) == 1 else jnp.concatenate(outs, axis=-1)


def _masked_softmax(s, mask):
    s = jnp.where(mask, s, -jnp.inf)
    m = jnp.max(s, axis=-1, keepdims=True)
    m = jnp.where(m == -jnp.inf, 0.0, m)
    e = jnp.where(mask, jnp.exp(s - m), 0.0)
    return e / jnp.maximum(jnp.sum(e, axis=-1, keepdims=True), 1e-30)


def _top_k_mask_t(score_t, k, n_cand):
    sub = 8
    n_rows, cols = score_t.shape
    row_in_blk = lax.broadcasted_iota(jnp.int32, (sub, cols), 0)
    n_blk = -(-n_cand // sub)
    blocks = [score_t[r * sub:(r + 1) * sub, :] for r in range(n_blk)]
    counts = [jnp.zeros((sub, cols), F32) for _ in range(n_blk)]
    for c in range(n_cand):
        row = score_t[c:c + 1, :]
        for r in range(n_blk):
            gt = jnp.where(row > blocks[r], 1.0, 0.0)
            ge = jnp.where(row >= blocks[r], 1.0, 0.0)
            if r * sub + sub - 1 < c:
                beats = gt
            elif r * sub > c:
                beats = ge
            else:
                beats = jnp.where(row_in_blk + r * sub > c, ge, gt)
            counts[r] = counts[r] + beats
    picked = [jnp.where(cnt < k, 1.0, 0.0) for cnt in counts]
    if n_blk * sub < n_rows:
        picked.append(jnp.zeros((n_rows - n_blk * sub, cols), F32))
    return jnp.concatenate(picked, axis=0)


def _norm_matmul_kernel(x_ref, g_ref, w_ref, o_ref, xn_ref):
    @pl.when(pl.program_id(1) == 0)
    def _():
        x = x_ref[...]
        ms = jnp.mean(x * x, axis=-1, keepdims=True)
        xn_ref[...] = (x * lax.rsqrt(ms + RMS_EPS) * g_ref[...]).astype(BF16)

    o_ref[...] = _dot(xn_ref[...], w_ref[...]).astype(o_ref.dtype)


def _norm_matmul(x, g, w, layer, tm, tn):
    rows, d = x.shape
    n = w.shape[2]
    return pl.pallas_call(
        _norm_matmul_kernel,
        grid=(rows // tm, n // tn),
        in_specs=[pl.BlockSpec((tm, d), lambda i, j: (i, 0)),
                  pl.BlockSpec((1, d), lambda i, j: (0, 0)),
                  pl.BlockSpec((None, d, tn), lambda i, j: (layer, 0, j))],
        out_specs=pl.BlockSpec((tm, tn), lambda i, j: (i, j)),
        out_shape=jax.ShapeDtypeStruct((rows, n), F32),
        scratch_shapes=[pltpu.VMEM((tm, d), BF16)],
        compiler_params=_cparams(("parallel", "arbitrary")),
        name="in_proj",
    )(x, g, w)


def _ffn_up_kernel(x_ref, g_ref, wg_ref, wv_ref, o_ref, xn_ref):
    @pl.when(pl.program_id(1) == 0)
    def _():
        x = x_ref[...]
        ms = jnp.mean(x * x, axis=-1, keepdims=True)
        xn_ref[...] = (x * lax.rsqrt(ms + RMS_EPS) * g_ref[...]).astype(BF16)

    xn = xn_ref[...]
    gate = _dot(xn, wg_ref[...])
    val = _dot(xn, wv_ref[...])
    o_ref[...] = (_silu(gate) * val).astype(o_ref.dtype)


def _ffn_up(x, g, w_gu, layer, tm, tn):
    rows, d = x.shape
    d_ff = w_gu.shape[2] // 2
    nj = d_ff // tn
    return pl.pallas_call(
        _ffn_up_kernel,
        grid=(rows // tm, nj),
        in_specs=[pl.BlockSpec((tm, d), lambda i, j: (i, 0)),
                  pl.BlockSpec((1, d), lambda i, j: (0, 0)),
                  pl.BlockSpec((None, d, tn), lambda i, j: (layer, 0, j)),
                  pl.BlockSpec((None, d, tn), lambda i, j: (layer, 0, j + nj))],
        out_specs=pl.BlockSpec((tm, tn), lambda i, j: (i, j)),
        out_shape=jax.ShapeDtypeStruct((rows, d_ff), BF16),
        scratch_shapes=[pltpu.VMEM((tm, d), BF16)],
        compiler_params=_cparams(("parallel", "arbitrary")),
        name="ffn_up",
    )(x, g, w_gu, w_gu)


def _ffn_down_kernel(a_ref, w_ref, h_ref, o_ref):
    o_ref[...] = h_ref[...] + _dot(a_ref[...], w_ref[...])


def _ffn_down(act, w_down, layer, h, tm, tn):
    rows, d_ff = act.shape
    d = w_down.shape[2]
    return pl.pallas_call(
        _ffn_down_kernel,
        grid=(rows // tm, d // tn),
        in_specs=[pl.BlockSpec((tm, d_ff), lambda i, j: (i, 0)),
                  pl.BlockSpec((None, d_ff, tn), lambda i, j: (layer, 0, j)),
                  pl.BlockSpec((tm, tn), lambda i, j: (i, j))],
        out_specs=pl.BlockSpec((tm, tn), lambda i, j: (i, j)),
        out_shape=jax.ShapeDtypeStruct((rows, d), F32),
        compiler_params=_cparams(("parallel", "arbitrary")),
        name="ffn_down",
    )(act, w_down, h)


def _out_proj_kernel(n_att, x_ref, y_ref, *refs):
    att_refs = refs[:n_att]
    w1_ref, w2_ref, o_ref = refs[n_att:]
    att = att_refs[0][...]
    for r in att_refs[1:]:
        att = att + r[...]
    o_ref[...] = (x_ref[...] + _dot(y_ref[...], w1_ref[...])
                  + _dot(att.astype(BF16), w2_ref[...]))


def _out_proj(x, y_ssd, atts, w_out, layer, tm, tn):
    rows, d = x.shape
    half = y_ssd.shape[1]
    n_att = len(atts)
    return pl.pallas_call(
        functools.partial(_out_proj_kernel, n_att),
        grid=(rows // tm, d // tn),
        in_specs=([pl.BlockSpec((tm, tn), lambda i, j: (i, j)),
                   pl.BlockSpec((tm, half), lambda i, j: (i, 0))]
                  + [pl.BlockSpec((tm, half), lambda i, j: (i, 0)) for _ in atts]
                  + [pl.BlockSpec((None, half, tn), lambda i, j: (layer, 0, j)),
                     pl.BlockSpec((None, half, tn), lambda i, j: (layer, 1, j))]),
        out_specs=pl.BlockSpec((tm, tn), lambda i, j: (i, j)),
        out_shape=jax.ShapeDtypeStruct((rows, d), F32),
        compiler_params=_cparams(("parallel", "arbitrary")),
        name="out_proj",
    )(x, y_ssd, *atts, w_out, w_out)


def _prep_kernel(transposed, q_ref, ks_ref, kw_ref, sm_ref, gq_ref, gks_ref, gkw_ref,
                 qn_ref, kvs_ref, kvw_ref, khs_ref, khw_ref, gates_ref, *t_refs):
    qn = _head_rms(q_ref[...], gq_ref[...]) * (HEAD_DIM ** -0.5)
    qn_ref[...] = qn.astype(BF16)
    if transposed:
        qt_ref, vts_ref, vtw_ref = t_refs
        q2 = qn * LOG2_E
        for h in range(KV_HEADS):
            qt_ref[h] = q2[:, h * GQA * HEAD_DIM:(h + 1) * GQA * HEAD_DIM].T.astype(BF16)
    for src, g_ref, full_ref, kh_ref, vt_ref in ((ks_ref, gks_ref, kvs_ref, khs_ref, vts_ref if transposed else None),
                                                 (kw_ref, gkw_ref, kvw_ref, khw_ref, vtw_ref if transposed else None)):
        kv = src[...]
        kn = _head_rms(kv[:, :K_COLS], g_ref[...])
        v = kv[:, K_COLS:]
        full_ref[:, :K_COLS] = kn
        full_ref[:, K_COLS:] = v
        for h in range(KV_HEADS):
            kh_ref[h] = kn[:, h * HEAD_DIM:(h + 1) * HEAD_DIM].astype(BF16)
        if transposed:
            for pair in range(KV_HEADS // 2):
                vt = v[:, pair * LANES:(pair + 1) * LANES].T.astype(BF16)
                ones = jnp.ones((V_ROWS - HEAD_DIM, vt.shape[1]), BF16)
                vt_ref[2 * pair] = jnp.concatenate([vt[:HEAD_DIM], ones], axis=0)
                vt_ref[2 * pair + 1] = jnp.concatenate([vt[HEAD_DIM:], ones], axis=0)
    sig = _sigmoid(sm_ref[...])
    n_dt = LANES // 8
    for h in range(KV_HEADS):
        gates_ref[h] = pltpu.roll(sig, LANES - n_dt - 3 * GQA * h, axis=1)


def _prep(u, col, g_q, g_ks, g_kw, tm, transposed):
    rows = u.shape[0]
    nsa = ATT_HEADS * HEAD_DIM
    row_spec = lambda w, c: pl.BlockSpec((tm, w), lambda i, c=c: (i, c))
    head_out = jax.ShapeDtypeStruct((KV_HEADS, rows, HEAD_DIM), BF16)
    head_spec = pl.BlockSpec((KV_HEADS, tm, HEAD_DIM), lambda i: (0, i, 0))
    vec = pl.BlockSpec((1, LANES), lambda i: (0, 0))
    out_specs = [pl.BlockSpec((tm, nsa), lambda i: (i, 0)),
                 pl.BlockSpec((tm, KV_COLS), lambda i: (i, 0)),
                 pl.BlockSpec((tm, KV_COLS), lambda i: (i, 0)),
                 head_spec, head_spec,
                 pl.BlockSpec((KV_HEADS, tm, LANES), lambda i: (0, i, 0))]
    out_shape = [jax.ShapeDtypeStruct((rows, nsa), BF16),
                 jax.ShapeDtypeStruct((rows, KV_COLS), F32),
                 jax.ShapeDtypeStruct((rows, KV_COLS), F32),
                 head_out, head_out,
                 jax.ShapeDtypeStruct((KV_HEADS, rows, LANES), F32)]
    if transposed:
        vt_spec = pl.BlockSpec((KV_HEADS, V_ROWS, tm), lambda i: (0, 0, i))
        vt_out = jax.ShapeDtypeStruct((KV_HEADS, V_ROWS, rows), BF16)
        out_specs += [pl.BlockSpec((KV_HEADS, GQA * HEAD_DIM, tm), lambda i: (0, 0, i)), vt_spec, vt_spec]
        out_shape += [jax.ShapeDtypeStruct((KV_HEADS, GQA * HEAD_DIM, rows), BF16), vt_out, vt_out]
    return pl.pallas_call(
        functools.partial(_prep_kernel, transposed),
        grid=(rows // tm,),
        in_specs=[row_spec(nsa, col["q"]), row_spec(KV_COLS, col["kvs"]),
                  row_spec(KV_COLS, col["kvw"]), row_spec(LANES, col["small"]), vec, vec, vec],
        out_specs=out_specs,
        out_shape=out_shape,
        compiler_params=_cparams(("parallel",)),
        name="head_prep",
    )(u, u, u, u, g_q, g_ks, g_kw)


def _gated_norm(y, xs, z, dsk, gn):
    y = (y + dsk * xs) * _silu(z)
    ms = jnp.mean(y * y, axis=-1, keepdims=True)
    return y * lax.rsqrt(ms + RMS_EPS) * gn


def _ssd_prompt_kernel(xbc_ref, z_ref, sm_ref, cw_ref, cb_ref, dtb_ref, alog_ref, dsk_ref, gn_ref,
                       y_ref, hout_ref, xb_ref, st_ref):
    c = pl.program_id(0)
    q = SSD_CHUNK
    ssm_dim = z_ref.shape[1]
    gw = SSM_STATE
    pad = 8

    @pl.when(c == 0)
    def _():
        xb_ref[0:pad, :] = jnp.zeros((pad, xb_ref.shape[1]), F32)
        st_ref[...] = jnp.zeros(st_ref.shape, F32)

    xb_ref[pad:pad + q, :] = xbc_ref[...]
    conv = cb_ref[...]
    for k in range(CONV_K):
        conv = conv + cw_ref[k:k + 1, :] * xb_ref[pl.ds(pad - (CONV_K - 1) + k, q), :]
    xb_ref[0:pad, :] = xb_ref[q:q + pad, :]
    act = _silu(conv)
    xs = act[:, :ssm_dim]
    bm = act[:, ssm_dim:ssm_dim + SSM_GROUPS * gw].astype(BF16)
    cm = act[:, ssm_dim + SSM_GROUPS * gw:].astype(BF16)

    lane = lax.broadcasted_iota(jnp.int32, (1, LANES), 1)
    n_heads = ssm_dim // SSM_HEAD_DIM
    dt = jnp.where(lane < n_heads, _softplus(sm_ref[...] + dtb_ref[...]), 0.0)
    a = -jnp.exp(alog_ref[...])
    ri = lax.broadcasted_iota(jnp.int32, (q, q), 0)
    ci = lax.broadcasted_iota(jnp.int32, (q, q), 1)
    causal = ri >= ci
    tri = jnp.where(causal, 1.0, 0.0).astype(BF16)
    da_hi, da_mid, da_lo = _split3(dt * a)
    acum = _dot(tri, da_hi) + _dot(tri, da_mid) + _dot(tri, da_lo)
    acum_t = acum.T
    dt_t = dt.T
    e_acum = jnp.exp(acum)
    a_last = acum[q - 1:q, :]
    w_end = dt * jnp.exp(a_last - acum)
    e_last = jnp.exp(a_last)
    lo = lane < SSM_HEAD_DIM
    row_lo = lax.broadcasted_iota(jnp.int32, (LANES, 1), 0) < SSM_HEAD_DIM

    ys = []
    heads_per_group = n_heads // SSM_GROUPS
    for pr in range(n_heads // 2):
        h0, h1 = 2 * pr, 2 * pr + 1
        g = h0 // heads_per_group
        cg = cm[:, g * gw:(g + 1) * gw]
        bg = bm[:, g * gw:(g + 1) * gw]
        cb = _dot_nt(cg, bg)
        x2 = xs[:, pr * LANES:(pr + 1) * LANES]
        x2b = x2.astype(BF16)
        yd = []
        for h in (h0, h1):
            seg = acum[:, h:h + 1] - acum_t[h:h + 1, :]
            dec = jnp.where(causal, jnp.exp(jnp.where(causal, seg, 0.0)), 0.0)
            m = (cb * dec * dt_t[h:h + 1, :]).astype(BF16)
            yd.append(_dot(m, x2b))
        y_diag = jnp.where(lo, yd[0], yd[1])
        sp = st_ref[pr * LANES:(pr + 1) * LANES, :]
        ea = jnp.where(lo, e_acum[:, h0:h0 + 1], e_acum[:, h1:h1 + 1])
        y_off = _dot_nt(cg, sp.astype(BF16)) * ea
        w2 = jnp.where(lo, w_end[:, h0:h0 + 1], w_end[:, h1:h1 + 1])
        xw_t = (x2 * w2).T.astype(BF16)
        cd = jnp.where(row_lo, e_last[:, h0:h0 + 1], e_last[:, h1:h1 + 1])
        st_ref[pr * LANES:(pr + 1) * LANES, :] = sp * cd + _dot(xw_t, bg)
        ys.append(y_diag + y_off)
    y = jnp.concatenate(ys, axis=-1)
    y_ref[...] = _gated_norm(y, xs, z_ref[...], dsk_ref[...], gn_ref[...]).astype(y_ref.dtype)

    @pl.when(c == pl.num_programs(0) - 1)
    def _():
        hout_ref[...] = st_ref[...]


def _ssd_prompt(u, col, lw):
    t = u.shape[0]
    ssm_dim = lw["dsk"].shape[1]
    conv_ch = lw["conv_w"].shape[1]
    q = SSD_CHUNK
    full = lambda r, w: pl.BlockSpec((r, w), lambda c: (0, 0))
    return pl.pallas_call(
        _ssd_prompt_kernel,
        grid=(t // q,),
        in_specs=[pl.BlockSpec((q, conv_ch), lambda c: (c, col["xbc"])),
                  pl.BlockSpec((q, ssm_dim), lambda c: (c, col["z"])),
                  pl.BlockSpec((q, LANES), lambda c: (c, col["small"])),
                  full(CONV_K, conv_ch), full(1, conv_ch), full(1, LANES), full(1, LANES),
                  full(1, ssm_dim), full(1, ssm_dim)],
        out_specs=[pl.BlockSpec((q, ssm_dim), lambda c: (c, 0)),
                   pl.BlockSpec((ssm_dim, SSM_STATE), lambda c: (0, 0))],
        out_shape=[jax.ShapeDtypeStruct((t, ssm_dim), BF16),
                   jax.ShapeDtypeStruct((ssm_dim, SSM_STATE), F32)],
        scratch_shapes=[pltpu.VMEM((q + 8, conv_ch), F32),
                        pltpu.VMEM((ssm_dim, SSM_STATE), F32)],
        compiler_params=_cparams(("arbitrary",)),
        name="ssd_prompt",
    )(u, u, u, lw["conv_w"], lw["conv_b"], lw["dt_bias"], lw["a_log"], lw["dsk"], lw["ssm_norm_g"])


def _ssd_sample_kernel(xbc_ref, z_ref, sm_ref, cbuf_ref, h_ref, cw_ref, cb_ref, dtb_ref, alog_ref,
                       dsk_ref, gn_ref, exp_ref, y_ref, hout_ref):
    b = pl.program_id(0)
    ssm_dim = z_ref.shape[1]
    gw = SSM_STATE
    n_heads = ssm_dim // SSM_HEAD_DIM
    conv = cb_ref[...] + cw_ref[CONV_K - 1:CONV_K, :] * xbc_ref[pl.ds(b, 1), :]
    for k in range(CONV_K - 1):
        conv = conv + cw_ref[k:k + 1, :] * cbuf_ref[k, pl.ds(b, 1), :]
    act = _silu(conv)
    xs = act[:, :ssm_dim]
    lane = lax.broadcasted_iota(jnp.int32, (1, LANES), 1)
    dt_raw = jnp.broadcast_to(jnp.where(lane < n_heads, sm_ref[pl.ds(b, 1), :], 0.0), (8, LANES))
    dt = _softplus(_dot3(dt_raw, exp_ref[...])[0:1, :] + dtb_ref[...])
    dec = jnp.exp(dt * (-jnp.exp(alog_ref[...])))
    dtx = dt * xs
    row8 = lax.broadcasted_iota(jnp.int32, (8, 1), 0)

    def rows8(pieces):
        out = jnp.zeros((8, pieces[0].shape[1]), F32)
        for r, p in enumerate(pieces):
            out = jnp.where(row8 == r, p.astype(F32), out)
        return out.astype(BF16)

    dec8 = rows8(_split3(dec))
    dtx8 = rows8(_split3(dtx))
    ones8 = jnp.where(row8 < 3, 1.0, 0.0).astype(BF16) * jnp.ones((1, gw), BF16)
    gh = (n_heads // SSM_GROUPS) * SSM_HEAD_DIM
    ys = []
    for g in range(SSM_GROUPS):
        bg = act[:, ssm_dim + g * gw:ssm_dim + (g + 1) * gw].astype(BF16)
        cg = act[:, ssm_dim + (SSM_GROUPS + g) * gw:ssm_dim + (SSM_GROUPS + g + 1) * gw].astype(BF16)
        b8 = jnp.where(row8 < 3, 1.0, 0.0).astype(BF16) * bg
        c8 = jnp.where(row8 < 1, 1.0, 0.0).astype(BF16) * cg
        dec_col = _dot_tn(dec8[:, g * gh:(g + 1) * gh], ones8)
        dbx = _dot_tn(dtx8[:, g * gh:(g + 1) * gh], b8)
        s_new = h_ref[g * gh:(g + 1) * gh, :] * dec_col + dbx
        hout_ref[g * gh:(g + 1) * gh, :] = s_new
        ys.append(_dot_nt(c8, s_new.astype(BF16))[0:1, :])
    y = jnp.concatenate(ys, axis=-1)
    y_ref[pl.ds(b, 1), :] = _gated_norm(y, xs, z_ref[pl.ds(b, 1), :], dsk_ref[...], gn_ref[...])


def _ssd_sample(u, col, conv_t, h_all, h_base, lw):
    bs = u.shape[0]
    ssm_dim = lw["dsk"].shape[1]
    conv_ch = lw["conv_w"].shape[1]
    full = lambda r, w: pl.BlockSpec((r, w), lambda b: (0, 0))
    return pl.pallas_call(
        _ssd_sample_kernel,
        grid=(bs,),
        in_specs=[pl.BlockSpec((bs, conv_ch), lambda b: (0, col["xbc"])),
                  pl.BlockSpec((bs, ssm_dim), lambda b: (0, col["z"])),
                  pl.BlockSpec((bs, LANES), lambda b: (0, col["small"])),
                  pl.BlockSpec((CONV_K - 1, bs, conv_ch), lambda b: (0, 0, 0)),
                  pl.BlockSpec((None, ssm_dim, SSM_STATE), lambda b: (h_base + b, 0, 0)),
                  full(CONV_K, conv_ch), full(1, conv_ch), full(1, ssm_dim), full(1, ssm_dim),
                  full(1, ssm_dim), full(1, ssm_dim), full(LANES, ssm_dim)],
        out_specs=[pl.BlockSpec((bs, ssm_dim), lambda b: (0, 0)),
                   pl.BlockSpec((None, ssm_dim, SSM_STATE), lambda b: (b, 0, 0))],
        out_shape=[jax.ShapeDtypeStruct((bs, ssm_dim), F32),
                   jax.ShapeDtypeStruct((bs, ssm_dim, SSM_STATE), F32)],
        compiler_params=_cparams(("arbitrary",)),
        name="ssd_sample",
    )(u, u, u, conv_t, h_all, lw["conv_w"], lw["conv_b"], lw["dt_bias_e"], lw["a_log_e"], lw["dsk"],
      lw["ssm_norm_g"], lw["head_expand"])


CMP_EXTRA_CHUNKS = 8


def _compress_rows(buf_ref, xcat_ref, res_ref, pe_ref, w_ref, gk_ref):
    n_blocks = res_ref.shape[0] - CMP_EXTRA_CHUNKS
    rows = res_ref.shape[0]
    pe = pe_ref[...]
    pe_hi = pe.astype(BF16).astype(F32)
    pe_lo = pe - pe_hi
    half = L_CMP // 2
    acc = []
    for p in range(KV_COLS // LANES):
        tile = slice(p * LANES, (p + 1) * LANES)
        base = (n_blocks + 1) * CMP_STRIDE
        extra = [None, pe_hi[:half, tile], pe_lo[:half, tile], pe_hi[half:, tile], pe_lo[half:, tile], None, None]
        for c, piece in enumerate(extra):
            dst = buf_ref.at[p, base + c * CMP_STRIDE:base + (c + 1) * CMP_STRIDE, :]
            dst[...] = jnp.zeros((CMP_STRIDE, LANES), F32) if piece is None else piece
        for phase in range(CMP_STRIDE):
            xcat_ref[:, phase * LANES:(phase + 1) * LANES] = (
                buf_ref[p, pl.ds(phase, rows, stride=CMP_STRIDE), :].astype(BF16))
        res_ref[...] = _dot(xcat_ref[...], w_ref[p // 2])
        pe_term = (res_ref[n_blocks + 2:n_blocks + 3, :LANES] + res_ref[n_blocks + 3:n_blocks + 4, :LANES]
                   + res_ref[n_blocks + 4:n_blocks + 5, LANES:] + res_ref[n_blocks + 5:n_blocks + 6, LANES:])
        acc.append(res_ref[0:n_blocks, :LANES] + res_ref[1:n_blocks + 1, LANES:] + pe_term)
    ck = _head_rms(jnp.concatenate(acc[:2], axis=-1), gk_ref[...])
    cv = jnp.concatenate(acc[2:], axis=-1)
    return ck, cv


def _store_heads(ck, cv, ck_ref, cv_ref):
    for h in range(KV_HEADS):
        ck_ref[h] = ck[:, h * HEAD_DIM:(h + 1) * HEAD_DIM].astype(BF16)
        cv_ref[h] = cv[:, h * HEAD_DIM:(h + 1) * HEAD_DIM].astype(BF16)


def _compress_prompt_kernel(main_ref, nxt_ref, pe_ref, w_ref, gk_ref, ck_ref, cv_ref, buf_ref, xcat_ref, res_ref):
    rows = main_ref.shape[0]
    for p in range(KV_COLS // LANES):
        buf_ref[p, 0:rows, :] = main_ref[:, p * LANES:(p + 1) * LANES]
        buf_ref[p, rows:rows + CMP_STRIDE, :] = nxt_ref[:, p * LANES:(p + 1) * LANES]
    ck, cv = _compress_rows(buf_ref, xcat_ref, res_ref, pe_ref, w_ref, gk_ref)
    _store_heads(ck, cv, ck_ref, cv_ref)


def _compress_scratch(rows):
    chunks = rows // CMP_STRIDE + CMP_EXTRA_CHUNKS
    return [pltpu.VMEM((KV_COLS // LANES, chunks * CMP_STRIDE, LANES), F32),
            pltpu.VMEM((chunks, CMP_STRIDE * LANES), BF16),
            pltpu.VMEM((chunks, 2 * LANES), F32)]


def _compress_prompt(kvc_pad, lw, rows_per_step):
    t = kvc_pad.shape[0] - CMP_STRIDE
    nb = rows_per_step // CMP_STRIDE
    out = jax.ShapeDtypeStruct((KV_HEADS, t // CMP_STRIDE, HEAD_DIM), BF16)
    out_spec = pl.BlockSpec((KV_HEADS, nb, HEAD_DIM), lambda i: (0, i, 0))
    return pl.pallas_call(
        _compress_prompt_kernel,
        grid=(t // rows_per_step,),
        in_specs=[pl.BlockSpec((rows_per_step, KV_COLS), lambda i: (i, 0)),
                  pl.BlockSpec((CMP_STRIDE, KV_COLS), lambda i: ((i + 1) * nb, 0)),
                  pl.BlockSpec((L_CMP, KV_COLS), lambda i: (0, 0)),
                  pl.BlockSpec((2, CMP_STRIDE * LANES, 2 * LANES), lambda i: (0, 0, 0)),
                  pl.BlockSpec((1, LANES), lambda i: (0, 0))],
        out_specs=[out_spec, out_spec],
        out_shape=[out, out],
        scratch_shapes=_compress_scratch(rows_per_step),
        compiler_params=_cparams(("parallel",)),
        name="compress_prompt",
    )(kvc_pad, kvc_pad, lw["cmp_pe_rows"], lw["cmp_w_cat"], lw["g_kc"])


def _compress_sample_kernel(n_pg, pt_ref, *refs):
    page_refs = refs[:n_pg]
    nxt_ref, pe_ref, w_ref, gk_ref, ck_ref, cv_ref, buf_ref, xcat_ref, res_ref = refs[n_pg:]
    pg = page_refs[0].shape[1]
    for p in range(KV_COLS // LANES):
        for k in range(n_pg):
            buf_ref[p, k * pg:(k + 1) * pg, :] = page_refs[k][p * LANES:(p + 1) * LANES, :].T
        buf_ref[p, n_pg * pg:n_pg * pg + CMP_STRIDE, :] = nxt_ref[p * LANES:(p + 1) * LANES, :].T[:CMP_STRIDE]
    ck, cv = _compress_rows(buf_ref, xcat_ref, res_ref, pe_ref, w_ref, gk_ref)
    _store_heads(ck, cv, ck_ref, cv_ref)


def _compress_sample(cache_t, page_base, page_table, lw):
    bs, n_pages = page_table.shape
    pg = cache_t.shape[2]
    n_pg = _pick(n_pages, (2 * PAGES_PER_STEP, PAGES_PER_STEP))
    nb = n_pg * pg // CMP_STRIDE
    past = n_pages * pg

    def page_map(b, j, pt, k):
        return (page_base + pt[b, j * n_pg + k], 0, 0)

    def next_map(b, j, pt):
        return (page_base + pt[b, jnp.minimum((j + 1) * n_pg, n_pages - 1)], 0, 0)

    out = jax.ShapeDtypeStruct((bs, KV_HEADS, past // CMP_STRIDE, HEAD_DIM), BF16)
    out_spec = pl.BlockSpec((None, KV_HEADS, nb, HEAD_DIM), lambda b, j, pt: (b, 0, j, 0))
    grid_spec = pltpu.PrefetchScalarGridSpec(
        num_scalar_prefetch=1,
        grid=(bs, n_pages // n_pg),
        in_specs=([pl.BlockSpec((None, KV_COLS, pg), functools.partial(page_map, k=k)) for k in range(n_pg)]
                  + [pl.BlockSpec((None, KV_COLS, pg), next_map),
                     pl.BlockSpec((L_CMP, KV_COLS), lambda b, j, pt: (0, 0)),
                     pl.BlockSpec((2, CMP_STRIDE * LANES, 2 * LANES), lambda b, j, pt: (0, 0, 0)),
                     pl.BlockSpec((1, LANES), lambda b, j, pt: (0, 0))]),
        out_specs=[out_spec, out_spec],
        scratch_shapes=_compress_scratch(n_pg * pg),
    )
    return pl.pallas_call(
        functools.partial(_compress_sample_kernel, n_pg),
        grid_spec=grid_spec,
        out_shape=[out, out],
        compiler_params=_cparams(("parallel", "parallel")),
        name="compress_sample",
    )(page_table, *([cache_t] * (n_pg + 1)), lw["cmp_pe_rows"], lw["cmp_w_cat"], lw["g_kc"])


def _cover_matrix(nc, n_sel_blocks, width):
    n = lax.broadcasted_iota(jnp.int32, (nc, width), 0) * CMP_STRIDE
    s = lax.broadcasted_iota(jnp.int32, (nc, width), 1)
    hit = (n < s * L_SEL + L_SEL) & (n + L_CMP > s * L_SEL) & (s < n_sel_blocks)
    return jnp.where(hit, 1.0, 0.0).astype(BF16)


def _block_scores(imp, q_pos, n_sel_blocks):
    blk = lax.broadcasted_iota(jnp.int32, imp.shape, 1)
    valid = blk * L_SEL <= q_pos
    cur = q_pos // L_SEL
    forced = (blk == 0) | (blk == cur) | (blk == cur - 1)
    score = jnp.where(valid & forced, SEL_FORCE, jnp.where(valid, imp, -1.0))
    return jnp.where(blk < n_sel_blocks, score, -2.0)


def _cmp_prompt_kernel(q_ref, ck_ref, cv_ref, gates_ref, oc_ref, sel_ref):
    i = pl.program_id(0)
    tq = q_ref.shape[0]
    nc = ck_ref.shape[1]
    n_sel_blocks = nc * CMP_STRIDE // L_SEL
    q_pos = i * tq + lax.broadcasted_iota(jnp.int32, (tq, 1), 0)
    n_end = lax.broadcasted_iota(jnp.int32, (1, nc), 1) * CMP_STRIDE + (L_CMP - 1)
    mask = (n_end <= q_pos) & (n_end < nc * CMP_STRIDE)
    cover = _cover_matrix(nc, n_sel_blocks, LANES)
    q = q_ref[...]
    outs = []
    pieces = []
    for h in range(KV_HEADS):
        q4 = jnp.concatenate([q[:, (h * GQA + g) * HEAD_DIM:(h * GQA + g + 1) * HEAD_DIM] for g in range(GQA)],
                             axis=0)
        s4 = _dot_nt(q4, ck_ref[h])
        ps = [_masked_softmax(s4[g * tq:(g + 1) * tq], mask) for g in range(GQA)]
        o4 = _dot(jnp.concatenate([p.astype(BF16) for p in ps], axis=0), cv_ref[h])
        gates = gates_ref[h]
        outs += [o4[g * tq:(g + 1) * tq] * gates[:, 3 * g:3 * g + 1] for g in range(GQA)]
        pieces += list(_split3(ps[0] + ps[1] + ps[2] + ps[3]))
    oc_ref[...] = jnp.concatenate(outs, axis=-1)
    imp_all = _dot(jnp.concatenate(pieces, axis=0), cover)
    for h in range(KV_HEADS):
        imp = sum(imp_all[(3 * h + r) * tq:(3 * h + r + 1) * tq] for r in range(3))
        score_t = _block_scores(imp, q_pos, n_sel_blocks).T
        sel_ref[h] = _top_k_mask_t(score_t, N_SEL, n_sel_blocks).astype(sel_ref.dtype)


def _cmp_prompt(qn, ck, cv, gates_h, t):
    nsa = qn.shape[1]
    nc = ck.shape[1]
    tq = Q_TILE
    kv_spec = pl.BlockSpec((KV_HEADS, nc, HEAD_DIM), lambda i: (0, 0, 0))
    return pl.pallas_call(
        _cmp_prompt_kernel,
        grid=(t // tq,),
        in_specs=[pl.BlockSpec((tq, nsa), lambda i: (i, 0)), kv_spec, kv_spec,
                  pl.BlockSpec((KV_HEADS, tq, LANES), lambda i: (0, i, 0))],
        out_specs=[pl.BlockSpec((tq, nsa), lambda i: (i, 0)),
                   pl.BlockSpec((KV_HEADS, LANES, tq), lambda i: (0, 0, i))],
        out_shape=[jax.ShapeDtypeStruct((t, nsa), F32),
                   jax.ShapeDtypeStruct((KV_HEADS, LANES, t), BF16)],
        compiler_params=_cparams(("parallel",)),
        name="cmp_topk_prompt",
    )(qn, ck, cv, gates_h)


def _tile_softmax_t(s_t, bias_t, tq, maxes, p_ref):
    new_maxes, alphas = [], []
    for g in range(GQA):
        sb = s_t[:, g * tq:(g + 1) * tq] + bias_t
        m_new = jnp.maximum(maxes[g], jnp.max(sb, axis=0, keepdims=True))
        alphas.append(jnp.exp2(maxes[g] - m_new))
        p_ref[:, g * tq:(g + 1) * tq] = jnp.exp2(sb - m_new).astype(BF16)
        new_maxes.append(m_new)
    return tuple(new_maxes), jnp.concatenate(alphas, axis=1)


def _stack_q_t(qt_ref):
    return jnp.concatenate([qt_ref[g * HEAD_DIM:(g + 1) * HEAD_DIM, :] for g in range(GQA)], axis=1)


def _finish_t(acc_t, gates, branch, tq):
    o_t = acc_t[:HEAD_DIM] * (1.0 / acc_t[HEAD_DIM:HEAD_DIM + 1])
    lane = lax.broadcasted_iota(jnp.int32, (1, LANES), 1)
    outs = []
    for pair in range(GQA // 2):
        g0, g1 = 2 * pair, 2 * pair + 1
        two = jnp.concatenate([o_t[:, g0 * tq:(g0 + 1) * tq], o_t[:, g1 * tq:(g1 + 1) * tq]], axis=0)
        gate = jnp.where(lane < HEAD_DIM, gates[:, 3 * g0 + branch:3 * g0 + branch + 1],
                         gates[:, 3 * g1 + branch:3 * g1 + branch + 1])
        outs.append(two.T * gate)
    return jnp.concatenate(outs, axis=1)


def _init_maxes(tq):
    return tuple(jnp.full((1, tq), NEG_BIG, F32) for _ in range(GQA))


def _sel_prompt_kernel(qt_ref, k_ref, vt_ref, sel_ref, gates_ref, o_ref, p0_ref, p1_ref, acc0_ref, acc1_ref):
    p_refs, acc_refs = (p0_ref, p1_ref), (acc0_ref, acc1_ref)
    i = pl.program_id(1)
    tq = sel_ref.shape[1]
    tk = SEL_KEY_TILE
    bpt = tk // L_SEL
    q4t = _stack_q_t(qt_ref)
    sel_t = sel_ref[...]
    q_pos = i * tq + lax.broadcasted_iota(jnp.int32, (1, tq), 1)
    blk_of_key = (lax.broadcasted_iota(jnp.int32, (tk, LANES), 1)
                  - lax.broadcasted_iota(jnp.int32, (tk, LANES), 0) // L_SEL)
    key_off = lax.broadcasted_iota(jnp.int32, (tk, 1), 0)
    n_tiles = (i * tq + tq - 1) // tk + 1
    last_tile = k_ref.shape[0] // tk - 1
    for acc_ref in acc_refs:
        acc_ref[...] = jnp.zeros(acc_ref.shape, F32)

    def tile(j, maxes, p_ref, acc_ref):
        live = (j < n_tiles).astype(jnp.int32)
        jc = jnp.minimum(j, last_tile)
        at = pl.ds(pl.multiple_of(jc * tk, tk), tk)
        expand_t = jnp.where(blk_of_key == jc * bpt, 1.0, 0.0).astype(BF16)
        picked = _dot(expand_t, sel_t)
        q_lim = q_pos * live - (1 - live)
        bias_t = jnp.where((picked > 0.5) & (key_off + jc * tk <= q_lim), 0.0, NEG_BIG)
        maxes, alpha = _tile_softmax_t(_dot(k_ref[at, :], q4t), bias_t, tq, maxes, p_ref)
        acc_ref[...] = alpha * acc_ref[...] + _dot(vt_ref[:, at], p_ref[...])
        return maxes

    def body(jj, carry):
        return tuple(tile(2 * jj + c, carry[c], p_refs[c], acc_refs[c]) for c in range(2))

    chains = lax.fori_loop(0, (n_tiles + 1) // 2, body, (_init_maxes(tq), _init_maxes(tq)))
    scale = [[], []]
    for m0, m1 in zip(*chains):
        m = jnp.maximum(m0, m1)
        scale[0].append(jnp.exp2(m0 - m))
        scale[1].append(jnp.exp2(m1 - m))
    acc = sum(jnp.concatenate(scale[c], axis=1) * acc_refs[c][...] for c in range(2))
    o_ref[...] = _finish_t(acc, gates_ref[...], 1, tq)


def _attn_specs(t, tq):
    return [pl.BlockSpec((None, GQA * HEAD_DIM, tq), lambda h, i: (h, 0, i)),
            pl.BlockSpec((None, t, HEAD_DIM), lambda h, i: (h, 0, 0)),
            pl.BlockSpec((None, V_ROWS, t), lambda h, i: (h, 0, 0)),
            pl.BlockSpec((None, tq, LANES), lambda h, i: (h, i, 0))]


def _sel_prompt(q_t, k_h, v_t, sel_t, gates_h, t):
    tq = _pick(t, (SEL_Q_TILE, Q_TILE))
    q_spec, k_spec, v_spec, g_spec = _attn_specs(t, tq)
    return pl.pallas_call(
        _sel_prompt_kernel,
        grid=(KV_HEADS, t // tq),
        in_specs=[q_spec, k_spec, v_spec, pl.BlockSpec((None, LANES, tq), lambda h, i: (h, 0, i)), g_spec],
        out_specs=pl.BlockSpec((tq, GQA * HEAD_DIM), lambda h, i: (i, h)),
        out_shape=jax.ShapeDtypeStruct((t, ATT_HEADS * HEAD_DIM), F32),
        scratch_shapes=([pltpu.VMEM((SEL_KEY_TILE, GQA * tq), BF16)] * 2
                        + [pltpu.VMEM((V_ROWS, GQA * tq), F32)] * 2),
        compiler_params=_cparams(("parallel", "arbitrary")),
        name="sel_prompt",
    )(q_t, k_h, v_t, sel_t, gates_h)


def _win_prompt_kernel(qt_ref, k_ref, vt_ref, gates_ref, o_ref, p_ref):
    i = pl.program_id(1)
    tq = gates_ref.shape[0]
    span = WINDOW + tq
    at = pl.ds(pl.multiple_of(jnp.maximum(i * tq - WINDOW, 0), tq), span)
    q_pos = i * tq + lax.broadcasted_iota(jnp.int32, (1, tq), 1)
    k_pos = jnp.maximum(i * tq - WINDOW, 0) + lax.broadcasted_iota(jnp.int32, (span, 1), 0)
    bias_t = jnp.where((k_pos <= q_pos) & (k_pos >= q_pos - WINDOW), 0.0, NEG_BIG)
    _tile_softmax_t(_dot(k_ref[at, :], _stack_q_t(qt_ref)), bias_t, tq, _init_maxes(tq), p_ref)
    o_ref[...] = _finish_t(_dot(vt_ref[:, at], p_ref[...]), gates_ref[...], 2, tq)


def _win_prompt(q_t, k_h, v_t, gates_h, t):
    tq = _pick(t, (SEL_Q_TILE, Q_TILE))
    return pl.pallas_call(
        _win_prompt_kernel,
        grid=(KV_HEADS, t // tq),
        in_specs=_attn_specs(t, tq),
        out_specs=pl.BlockSpec((tq, GQA * HEAD_DIM), lambda h, i: (i, h)),
        out_shape=jax.ShapeDtypeStruct((t, ATT_HEADS * HEAD_DIM), F32),
        scratch_shapes=[pltpu.VMEM((WINDOW + tq, GQA * tq), BF16)],
        compiler_params=_cparams(("parallel", "arbitrary")),
        name="win_prompt",
    )(q_t, k_h, v_t, gates_h)


def _own_head_rows(parts, width):
    row_head = lax.broadcasted_iota(jnp.int32, (ATT_HEADS, 1), 0) // GQA
    out = jnp.zeros((ATT_HEADS, width), F32)
    for h in range(KV_HEADS):
        out = jnp.where(row_head == h, parts[h], out)
    return out


def _cmp_sample_kernel(past, q_ref, ck_ref, cv_ref, g3_ref, oc_ref, sel_ref, score_ref):
    b = pl.program_id(0)
    nc = ck_ref.shape[1]
    width = sel_ref.shape[1]
    n_sel_blocks = past // L_SEL + 1
    q = q_ref[...]
    n_end = lax.broadcasted_iota(jnp.int32, (1, nc), 1) * CMP_STRIDE + (L_CMP - 1)
    mask = n_end <= past
    s = _own_head_rows([_dot_nt(q, ck_ref[h]) for h in range(KV_HEADS)], nc)
    p = _masked_softmax(s, mask)
    pb = p.astype(BF16)
    o = _own_head_rows([_dot(pb, cv_ref[h]) for h in range(KV_HEADS)], HEAD_DIM)
    oc_ref[...] = o * g3_ref[:, 0:1]
    row_head = lax.broadcasted_iota(jnp.int32, (ATT_HEADS, 1), 0) // GQA
    row8 = lax.broadcasted_iota(jnp.int32, (8, 1), 0)
    p_sum = jnp.zeros((8, nc), F32)
    for h in range(KV_HEADS):
        ph = jnp.sum(jnp.where(row_head == h, p, 0.0), axis=0, keepdims=True)
        p_sum = jnp.where(row8 == h, ph, p_sum)
    imp = _dot3(p_sum, _cover_matrix(nc, n_sel_blocks, width))
    q_pos = jnp.full((8, 1), past, jnp.int32)
    score_ref[pl.ds(pl.multiple_of(b * 8, 8), 8), :] = _block_scores(imp, q_pos, n_sel_blocks)

    @pl.when(b == pl.num_programs(0) - 1)
    def _():
        sel_ref[...] = _top_k_mask_t(score_ref[...].T, N_SEL, n_sel_blocks).T


def _cmp_sample(q3, ck_s, cv_s, g3, past):
    bs = q3.shape[0]
    nc = ck_s.shape[2]
    width = -(-(past // L_SEL + 1) // LANES) * LANES
    kv_spec = pl.BlockSpec((None, KV_HEADS, nc, HEAD_DIM), lambda b: (b, 0, 0, 0))
    oc, sel = pl.pallas_call(
        functools.partial(_cmp_sample_kernel, past),
        grid=(bs,),
        in_specs=[pl.BlockSpec((None, ATT_HEADS, HEAD_DIM), lambda b: (b, 0, 0)), kv_spec, kv_spec,
                  pl.BlockSpec((None, ATT_HEADS, LANES), lambda b: (b, 0, 0))],
        out_specs=[pl.BlockSpec((None, ATT_HEADS, HEAD_DIM), lambda b: (b, 0, 0)),
                   pl.BlockSpec((bs * 8, width), lambda b: (0, 0))],
        out_shape=[jax.ShapeDtypeStruct((bs, ATT_HEADS, HEAD_DIM), F32),
                   jax.ShapeDtypeStruct((bs * 8, width), F32)],
        scratch_shapes=[pltpu.VMEM((bs * 8, width), F32)],
        compiler_params=_cparams(("arbitrary",)),
        name="cmp_topk_sample",
    )(q3, ck_s, cv_s, g3)
    return oc, sel.reshape(bs, 8, width)


def _diag_heads(x):
    return _own_head_rows([x[:, h * HEAD_DIM:(h + 1) * HEAD_DIM] for h in range(KV_HEADS)], HEAD_DIM)


def _sel_win_sample_kernel(past, pt_ref, *refs):
    n_pg = PAGES_PER_STEP
    page_refs = refs[:n_pg]
    (qbd_ref, selg_ref, selnew_ref, knew_ref, win_ref, wnew_ref, oc_ref, g3_ref,
     o_ref, m_ref, l_ref, acc_ref, kv_ref) = refs[n_pg:]
    j = pl.program_id(1)
    pg = page_refs[0].shape[1]
    qbd = qbd_ref[...]
    bpp = pg // L_SEL

    @pl.when(j == 0)
    def _():
        m_ref[...] = jnp.full(m_ref.shape, NEG_BIG, F32)
        l_ref[...] = jnp.zeros(l_ref.shape, F32)
        acc_ref[...] = jnp.zeros(acc_ref.shape, F32)

    sel = selg_ref[...].astype(BF16)
    n_blk = n_pg * bpp
    expand = jnp.where(lax.broadcasted_iota(jnp.int32, (n_blk, n_pg * pg), 0)
                       == lax.broadcasted_iota(jnp.int32, (n_blk, n_pg * pg), 1) // L_SEL, 1.0, 0.0).astype(BF16)
    picked = _dot(sel, expand)
    for k in range(n_pg):
        kv_ref[:, k * pg:(k + 1) * pg] = page_refs[k][...].astype(BF16)
    m, l, acc = m_ref[...], l_ref[...], acc_ref[...]
    s = _dot(qbd, kv_ref[0:K_COLS, :]) + jnp.where(picked > 0.5, 0.0, NEG_BIG)
    m_new = jnp.maximum(m, jnp.max(s, axis=-1, keepdims=True))
    alpha = jnp.exp(m - m_new)
    p = jnp.exp(s - m_new)
    l = alpha * l + jnp.sum(p, axis=-1, keepdims=True)
    acc = alpha * acc + _dot_nt(p.astype(BF16), kv_ref[K_COLS:, :])
    m = m_new
    m_ref[...], l_ref[...], acc_ref[...] = m, l, acc

    @pl.when(j == pl.num_programs(1) - 1)
    def _():
        qf = qbd.astype(F32)
        kn = knew_ref[...]
        s_new = jnp.sum(qf * kn[:, :K_COLS].astype(BF16).astype(F32), axis=-1, keepdims=True)
        ok_new = selnew_ref[:, 0:1] > 0.5
        s_new = jnp.where(ok_new, s_new, NEG_BIG)
        m2 = jnp.maximum(m, s_new)
        a2 = jnp.exp(m - m2)
        p_new = jnp.where(ok_new, jnp.exp(s_new - m2), 0.0)
        l2 = a2 * l + p_new
        acc2 = a2 * acc + p_new.astype(BF16).astype(F32) * kn[:, K_COLS:].astype(BF16).astype(F32)
        o_s = _diag_heads(acc2 / jnp.maximum(l2, 1e-30))
        w = win_ref[...].astype(BF16)
        w_buf = w.shape[1]
        wn = wnew_ref[...]
        w_pos = past - w_buf + lax.broadcasted_iota(jnp.int32, (1, w_buf), 1)
        ok_w = (w_pos <= past) & (w_pos >= past - WINDOW) & (w_pos >= 0)
        s_w = jnp.where(ok_w, _dot(qbd, w[:K_COLS, :]), -jnp.inf)
        s_wn = jnp.sum(qf * wn[:, :K_COLS].astype(BF16).astype(F32), axis=-1, keepdims=True)
        m_w = jnp.maximum(jnp.max(s_w, axis=-1, keepdims=True), s_wn)
        p_w = jnp.where(ok_w, jnp.exp(s_w - m_w), 0.0)
        p_wn = jnp.exp(s_wn - m_w)
        l_w = jnp.sum(p_w, axis=-1, keepdims=True) + p_wn
        p_w = p_w / l_w
        p_wn = p_wn / l_w
        o_w = _diag_heads(_dot_nt(p_w.astype(BF16), w[K_COLS:, :])
                          + p_wn.astype(BF16).astype(F32) * wn[:, K_COLS:].astype(BF16).astype(F32))
        g3 = g3_ref[...]
        o_ref[...] = oc_ref[...] + g3[:, 1:2] * o_s + g3[:, 2:3] * o_w


def _sel_win_sample(cache_t, page_base, page_table, qbd, sel_groups, sel_new, kvs_new, win_t, win_base,
                    kvw_new, oc, g3, past):
    bs, n_pages = page_table.shape
    pg = cache_t.shape[2]
    n_pg = PAGES_PER_STEP
    bps = n_pg * pg // L_SEL
    w_buf = win_t.shape[2]

    def page_map(b, j, pt, k):
        return (page_base + pt[b, j * n_pg + k], 0, 0)

    per_b = lambda *shape: pl.BlockSpec((None,) + shape, lambda b, j, pt: (b,) + (0,) * len(shape))
    grid_spec = pltpu.PrefetchScalarGridSpec(
        num_scalar_prefetch=1,
        grid=(bs, n_pages // n_pg),
        in_specs=([pl.BlockSpec((None, KV_COLS, pg), functools.partial(page_map, k=k)) for k in range(n_pg)]
                  + [per_b(ATT_HEADS, K_COLS),
                     pl.BlockSpec((None, None, ATT_HEADS, bps), lambda b, j, pt: (b, j, 0, 0)),
                     per_b(ATT_HEADS, LANES), per_b(1, KV_COLS),
                     pl.BlockSpec((None, KV_COLS, w_buf), lambda b, j, pt: (win_base + b, 0, 0)),
                     per_b(1, KV_COLS), per_b(ATT_HEADS, HEAD_DIM), per_b(ATT_HEADS, LANES)]),
        out_specs=per_b(ATT_HEADS, HEAD_DIM),
        scratch_shapes=[pltpu.VMEM((ATT_HEADS, 1), F32), pltpu.VMEM((ATT_HEADS, 1), F32),
                        pltpu.VMEM((ATT_HEADS, K_COLS), F32), pltpu.VMEM((KV_COLS, n_pg * pg), BF16)],
    )
    return pl.pallas_call(
        functools.partial(_sel_win_sample_kernel, past),
        grid_spec=grid_spec,
        out_shape=jax.ShapeDtypeStruct((bs, ATT_HEADS, HEAD_DIM), F32),
        compiler_params=_cparams(("parallel", "arbitrary")),
        name="sel_win_sample",
    )(page_table, *([cache_t] * n_pg), qbd, sel_groups, sel_new, kvs_new, win_t, kvw_new, oc, g3)


def _column_layout(d_model):
    ssm_dim = d_model // 2
    conv_ch = ssm_dim + 2 * SSM_GROUPS * SSM_STATE
    nsa = d_model - ssm_dim
    n_heads = ssm_dim // SSM_HEAD_DIM
    sizes = (ssm_dim, conv_ch, n_heads, nsa, KV_COLS, KV_COLS, KV_COLS, 3 * ATT_HEADS)
    offs = [0]
    for s in sizes:
        offs.append(offs[-1] + s)
    src = dict(zip(("z", "xbc", "dt", "q", "kvc", "kvs", "kvw", "gt"), zip(offs[:-1], sizes)))
    order = ("xbc", "z", "q", "kvc", "kvs", "kvw")
    col, start = {}, {}
    pos = 0
    for name in order:
        o, w = src[name]
        assert pos % w == 0
        col[name] = pos // w
        start[name] = pos
        pos += w
    col["small"] = pos // LANES
    start["small"] = pos
    return src, order, col, start, pos + LANES


def _dense_weights(p, d_model):
    src, order, _, _, _ = _column_layout(d_model)
    w_in = p["w_in"]
    n_heads = src["dt"][1]
    small = jnp.concatenate([w_in[:, :, src["dt"][0]:src["dt"][0] + n_heads],
                             w_in[:, :, src["gt"][0]:src["gt"][0] + src["gt"][1]]], axis=2)
    small = jnp.pad(small, ((0, 0), (0, 0), (0, LANES - small.shape[2])))
    w_in_packed = jnp.concatenate([w_in[:, :, src[n][0]:src[n][0] + src[n][1]] for n in order] + [small],
                                  axis=2).astype(BF16)
    return {"w_in": w_in_packed, "w_out": p["w_out"].astype(BF16), "w_gu": p["w_gu"].astype(BF16),
            "w_down": p["w_down"].astype(BF16)}


def _layer_weights(l, p, d_model):
    src, order, col, start, n_cols = _column_layout(d_model)
    ssm_dim = src["z"][1]
    rep = lambda v: jnp.repeat(v, SSM_HEAD_DIM)[None, :]
    pad_l = lambda v: jnp.pad(v, (0, LANES - v.shape[0]))[None, :]
    tile2 = lambda v: jnp.tile(v, 2)[None, :]
    cw = jnp.transpose(p["cmp_w"][l], (1, 0, 2, 3))
    zero = jnp.zeros_like(cw)
    pairs = jnp.concatenate([jnp.concatenate([cw, zero], axis=-1),
                             jnp.concatenate([zero, cw], axis=-1)], axis=-2)
    halves = jnp.transpose(pairs, (1, 0, 2, 3)).reshape(2, 2, CMP_STRIDE * LANES, LANES)
    w_cat = jnp.concatenate([halves[:, 0], halves[:, 1]], axis=-1).astype(BF16)
    pe = p["cmp_pe"][l]
    pe_rows = jnp.broadcast_to(pe[:, :, None, :], (L_CMP, 2, KV_HEADS, HEAD_DIM)).reshape(L_CMP, KV_COLS)
    head_expand = (jnp.arange(LANES)[:, None] == jnp.arange(ssm_dim)[None, :] // SSM_HEAD_DIM).astype(BF16)
    return {
        "col": col, "start": start,
        "norm1_g": p["norm1_g"][l][None, :],
        "conv_w": p["conv_w"][l], "conv_b": p["conv_b"][l][None, :],
        "dt_bias": pad_l(p["dt_bias"][l]), "a_log": pad_l(p["a_log"][l]),
        "dt_bias_e": rep(p["dt_bias"][l]), "a_log_e": rep(p["a_log"][l]),
        "dsk": rep(p["d_skip"][l]), "ssm_norm_g": p["ssm_norm_g"][l][None, :],
        "head_expand": head_expand,
        "g_q": tile2(p["q_norm_g"][l]), "g_kc": tile2(p["k_norm_g"][l, 0]),
        "g_ks": tile2(p["k_norm_g"][l, 1]), "g_kw": tile2(p["k_norm_g"][l, 2]),
        "cmp_pe_rows": pe_rows, "cmp_w_cat": w_cat,
        "norm2_g": p["norm2_g"][l][None, :],
    }


def _pick(n, prefs):
    for c in prefs:
        if n % c == 0:
            return c
    return n


def _dense_tail(x, y_ssd, atts, lw, big, layer, tm):
    d = x.shape[1]
    h = _out_proj(x, y_ssd, atts, big["w_out"], layer, tm, _pick(d, (512,)))
    d_ff = big["w_down"].shape[1]
    act = _ffn_up(h, lw["norm2_g"], big["w_gu"], layer, tm, _pick(d_ff, (512, 256, 128)))
    return _ffn_down(act, big["w_down"], layer, h, tm, _pick(d, (512,)))


def kernel(x_prompt, x_sample, cache_cmp_kv, cache_slc_kv, state_win_kv, state_ssm, state_conv, page_table,
           norm1_g, w_in, conv_w, conv_b, dt_bias, a_log, d_skip, ssm_norm_g, q_norm_g, k_norm_g,
           cmp_pe, cmp_w, w_out, norm2_g, w_gu, w_down):
    params = dict(norm1_g=norm1_g, w_in=w_in, conv_w=conv_w, conv_b=conv_b, dt_bias=dt_bias, a_log=a_log,
                  d_skip=d_skip, ssm_norm_g=ssm_norm_g, q_norm_g=q_norm_g, k_norm_g=k_norm_g, cmp_pe=cmp_pe,
                  cmp_w=cmp_w, w_out=w_out, norm2_g=norm2_g, w_gu=w_gu, w_down=w_down)
    bp, t, d = x_prompt.shape
    bs, dec_t, _ = x_sample.shape
    depth = w_in.shape[0]
    n_pool, pg = cache_cmp_kv.shape[1:3]
    n_pages = page_table.shape[1]
    past = n_pages * pg
    w_buf = state_win_kv.shape[2]
    assert bp == 1 and dec_t == 1
    assert t % (PAGES_PER_STEP * pg) == 0 and past % (PAGES_PER_STEP * pg) == 0 and t // L_SEL <= LANES
    ssm_dim = d // 2
    n_heads = ssm_dim // SSM_HEAD_DIM
    kv_shape = (2, KV_HEADS, HEAD_DIM)

    xp = x_prompt[0]
    xs = x_sample[:, 0]
    feature_major = lambda a: jnp.transpose(a, (0, 1, 3, 4, 5, 2)).reshape(a.shape[0] * a.shape[1], KV_COLS,
                                                                           a.shape[2])
    cmp_pages_t = feature_major(cache_cmp_kv)
    slc_pages_t = feature_major(cache_slc_kv)
    win_t = feature_major(state_win_kv)
    cmp_rows = _pick(t, (2 * PAGES_PER_STEP * pg, PAGES_PER_STEP * pg))
    outs = {k: [] for k in ("cmp_p", "cmp_s", "slc_p", "slc_s", "win_p", "win_s", "ssm_p", "ssm_s",
                            "conv_p", "conv_s")}
    tm_p = _pick(t, (512, 256, 128))
    big = _dense_weights(params, d)
    ssm_all = state_ssm.reshape(depth * bs, ssm_dim, SSM_STATE)
    tn_in = _pick(big["w_in"].shape[2], (1152, 640, 384, 128))
    for l in range(depth):
        lw = _layer_weights(l, params, d)
        col, start = lw["col"], lw["start"]

        u = _norm_matmul(xp, lw["norm1_g"], big["w_in"], l, tm_p, tn_in)
        qn, kvs_n, kvw_n, k_s, k_w, gates_h, q_t, vt_s, vt_w = _prep(u, col, lw["g_q"], lw["g_ks"], lw["g_kw"],
                                                                     _pick(t, (256, 128)), True)
        y_ssd, h_fin = _ssd_prompt(u, col, lw)
        kvc = u[:, start["kvc"]:start["kvc"] + KV_COLS]
        xbc_tail = u[t - (CONV_K - 1):, start["xbc"]:start["xbc"] + lw["conv_w"].shape[1]]
        ck, cv = _compress_prompt(jnp.pad(kvc, ((0, CMP_STRIDE), (0, 0))), lw, cmp_rows)
        o_c, sel_t = _cmp_prompt(qn, ck, cv, gates_h, t)
        o_s = _sel_prompt(q_t, k_s, vt_s, sel_t, gates_h, t)
        o_w = _win_prompt(q_t, k_w, vt_w, gates_h, t)
        xp = _dense_tail(xp, y_ssd, [o_c, o_s, o_w], lw, big, l, tm_p)
        outs["cmp_p"].append(kvc.reshape((1, t) + kv_shape))
        outs["slc_p"].append(kvs_n.reshape((1, t) + kv_shape))
        outs["win_p"].append(kvw_n[t - min(WINDOW, t):].reshape((1, min(WINDOW, t)) + kv_shape))
        outs["ssm_p"].append(h_fin.reshape(1, n_heads, SSM_HEAD_DIM, SSM_STATE))
        outs["conv_p"].append(xbc_tail[None])

        us = _norm_matmul(xs, lw["norm1_g"], big["w_in"], l, bs, tn_in)
        qn_s, kvs_s, kvw_s, _, _, gates_s = _prep(us, col, lw["g_q"], lw["g_ks"], lw["g_kw"], bs, False)
        conv_t = jnp.transpose(state_conv[l], (1, 0, 2))
        y_ssd_s, h_new = _ssd_sample(us, col, conv_t, ssm_all, l * bs, lw)
        xbc_s = us[:, start["xbc"]:start["xbc"] + lw["conv_w"].shape[1]]
        kvc_s = us[:, start["kvc"]:start["kvc"] + KV_COLS]
        ck_s, cv_s = _compress_sample(cmp_pages_t, l * n_pool, page_table, lw)
        q3 = qn_s.reshape(bs, ATT_HEADS, HEAD_DIM)
        g3 = jnp.transpose(gates_s[:, :, :3 * GQA].reshape(KV_HEADS, bs, GQA, 3), (1, 0, 2, 3))
        g3 = jnp.pad(g3.reshape(bs, ATT_HEADS, 3), ((0, 0), (0, 0), (0, LANES - 3)))
        oc_s, sel_s = _cmp_sample(q3, ck_s, cv_s, g3, past)
        sel16 = jnp.repeat(sel_s[:, :KV_HEADS], GQA, axis=1)
        bps = PAGES_PER_STEP * pg // L_SEL
        sel_groups = jnp.transpose(sel16[:, :, :past // L_SEL].reshape(bs, ATT_HEADS, past // L_SEL // bps, bps),
                                   (0, 2, 1, 3))
        sel_new = jnp.pad(sel16[:, :, past // L_SEL:past // L_SEL + 1], ((0, 0), (0, 0), (0, LANES - 1)))
        head_of_col = jnp.arange(K_COLS) // HEAD_DIM
        qbd = jnp.where(head_of_col[None, None, :] == (jnp.arange(ATT_HEADS) // GQA)[None, :, None],
                        jnp.tile(q3, (1, 1, KV_HEADS)), jnp.zeros((), BF16))
        o_att_s = _sel_win_sample(slc_pages_t, l * n_pool, page_table, qbd, sel_groups, sel_new,
                                  kvs_s[:, None, :], win_t, l * bs, kvw_s[:, None, :], oc_s, g3, past)
        xs = _dense_tail(xs, y_ssd_s.astype(BF16), [o_att_s.reshape(bs, ATT_HEADS * HEAD_DIM)], lw, big, l, bs)
        outs["cmp_s"].append(kvc_s.reshape((bs, 1) + kv_shape))
        outs["slc_s"].append(kvs_s.reshape((bs, 1) + kv_shape))
        outs["win_s"].append(kvw_s.reshape((bs, 1) + kv_shape))
        outs["ssm_s"].append(h_new.reshape(bs, n_heads, SSM_HEAD_DIM, SSM_STATE))
        outs["conv_s"].append(jnp.concatenate([state_conv[l], xbc_s[:, None, :]], axis=1)[:, -(CONV_K - 1):])

    st = lambda k: jnp.stack(outs[k])
    win_s = jnp.concatenate([state_win_kv, st("win_s")], axis=2)[:, :, -w_buf:]
    return (xp[None], xs[:, None], st("cmp_p"), st("cmp_s"), st("slc_p"), st("slc_s"), st("win_p"),
            win_s, st("ssm_p"), st("ssm_s"), st("conv_p"), st("conv_s"))
```

```python
import functools

import jax
import jax.numpy as jnp
from jax import lax
from jax.experimental import pallas as pl
from jax.experimental.pallas import tpu as pltpu

F32 = jnp.float32
BF16 = jnp.bfloat16

RMS_EPS = 1e-6
SSM_HEAD_DIM = 64
SSM_GROUPS = 4
SSM_STATE = 128
CONV_K = 4
SSD_CHUNK = 128
HEAD_DIM = 64
KV_HEADS = 4
GQA = 4
ATT_HEADS = KV_HEADS * GQA
CMP_STRIDE = 16
L_CMP = 32
L_SEL = 64
N_SEL = 16
WINDOW = 512
SEL_FORCE = 1e9
KV_COLS = 2 * KV_HEADS * HEAD_DIM
K_COLS = KV_HEADS * HEAD_DIM
V_ROWS = HEAD_DIM + 16

LANES = 128
Q_TILE = 128
SEL_Q_TILE = 256
SEL_KEY_TILE = 512
PAGES_PER_STEP = 16
VMEM_LIMIT = 56 * 1024 * 1024
NEG_BIG = -1e30
LOG2_E = 1.4426950408889634


def _cparams(sem):
    return pltpu.CompilerParams(dimension_semantics=sem, vmem_limit_bytes=VMEM_LIMIT)


def _sigmoid(x):
    return 1.0 / (1.0 + jnp.exp(-x))


def _silu(x):
    return x * _sigmoid(x)


def _softplus(x):
    return jnp.maximum(x, 0.0) + jnp.log(1.0 + jnp.exp(-jnp.abs(x)))


def _split3(x):
    hi = x.astype(BF16)
    r = x - hi.astype(F32)
    mid = r.astype(BF16)
    lo = (r - mid.astype(F32)).astype(BF16)
    return hi, mid, lo


def _dot(a, b):
    return jnp.dot(a, b, preferred_element_type=F32)


def _dot_nt(a, b):
    return lax.dot_general(a, b, (((1,), (1,)), ((), ())), preferred_element_type=F32)


def _dot_tn(a, b):
    return lax.dot_general(a, b, (((0,), (0,)), ((), ())), preferred_element_type=F32)


def _dot3(x, m):
    hi, mid, lo = _split3(x)
    return _dot(hi, m) + _dot(mid, m) + _dot(lo, m)


def _head_rms(x, g2):
    lane = lax.broadcasted_iota(jnp.int32, (1, LANES), 1)
    lo = lane < HEAD_DIM
    outs = []
    for j in range(x.shape[1] // LANES):
        xt = x[:, j * LANES:(j + 1) * LANES]
        sq = xt * xt
        s_lo = jnp.sum(jnp.where(lo, sq, 0.0), axis=-1, keepdims=True)
        s_hi = jnp.sum(jnp.where(lo, 0.0, sq), axis=-1, keepdims=True)
        ms = jnp.where(lo, s_lo, s_hi) * (1.0 / HEAD_DIM)
        outs.append(xt * lax.rsqrt(ms + RMS_EPS) * g2)
    return outs[0] if len(outs) == 1 else jnp.concatenate(outs, axis=-1)


def _masked_softmax(s, mask):
    s = jnp.where(mask, s, -jnp.inf)
    m = jnp.max(s, axis=-1, keepdims=True)
    m = jnp.where(m == -jnp.inf, 0.0, m)
    e = jnp.where(mask, jnp.exp(s - m), 0.0)
    return e / jnp.maximum(jnp.sum(e, axis=-1, keepdims=True), 1e-30)


RANK_GROUP = 32


def _top_k_mask_t(score_t, cnt_ref, k, n_cand, live_groups=None):
    sub = 8
    n_rows, cols = score_t.shape
    row_in_blk = lax.broadcasted_iota(jnp.int32, (sub, cols), 0)
    n_blk = -(-n_cand // sub)
    n_grp = -(-n_cand // RANK_GROUP)
    blk_per_grp = RANK_GROUP // sub
    cnt_ref[...] = jnp.zeros(cnt_ref.shape, F32)

    def count_tile(rival_grp, cand_grp):
        for r in range(cand_grp * blk_per_grp, min((cand_grp + 1) * blk_per_grp, n_blk)):
            block = score_t[r * sub:(r + 1) * sub, :]
            cnt = cnt_ref[r * sub:(r + 1) * sub, :]
            for c in range(rival_grp * RANK_GROUP, min((rival_grp + 1) * RANK_GROUP, n_cand)):
                row = score_t[c:c + 1, :]
                gt = jnp.where(row > block, 1.0, 0.0)
                ge = jnp.where(row >= block, 1.0, 0.0)
                if r * sub + sub - 1 < c:
                    cnt = cnt + gt
                elif r * sub > c:
                    cnt = cnt + ge
                else:
                    cnt = cnt + jnp.where(row_in_blk + r * sub > c, ge, gt)
            cnt_ref[r * sub:(r + 1) * sub, :] = cnt

    for rival_grp in range(n_grp):
        for cand_grp in range(n_grp):
            if live_groups is None:
                count_tile(rival_grp, cand_grp)
            else:
                pl.when(max(rival_grp, cand_grp) < live_groups)(functools.partial(count_tile, rival_grp, cand_grp))
    row = lax.broadcasted_iota(jnp.int32, (n_rows, cols), 0)
    return jnp.where(row < n_blk * sub, jnp.where(cnt_ref[...] < k, 1.0, 0.0), 0.0)


def _norm_matmul_kernel(x_ref, g_ref, w_ref, o_ref, xn_ref):
    @pl.when(pl.program_id(1) == 0)
    def _():
        x = x_ref[...]
        ms = jnp.mean(x * x, axis=-1, keepdims=True)
        xn_ref[...] = (x * lax.rsqrt(ms + RMS_EPS) * g_ref[...]).astype(BF16)

    o_ref[...] = _dot(xn_ref[...], w_ref[...]).astype(o_ref.dtype)


def _norm_matmul(x, g, w, layer, tm, tn):
    rows, d = x.shape
    n = w.shape[2]
    return pl.pallas_call(
        _norm_matmul_kernel,
        grid=(rows // tm, n // tn),
        in_specs=[pl.BlockSpec((tm, d), lambda i, j: (i, 0)),
                  pl.BlockSpec((1, d), lambda i, j: (0, 0)),
                  pl.BlockSpec((None, d, tn), lambda i, j: (layer, 0, j))],
        out_specs=pl.BlockSpec((tm, tn), lambda i, j: (i, j)),
        out_shape=jax.ShapeDtypeStruct((rows, n), F32),
        scratch_shapes=[pltpu.VMEM((tm, d), BF16)],
        compiler_params=_cparams(("parallel", "arbitrary")),
        name="in_proj",
    )(x, g, w)


def _ffn_up_kernel(x_ref, g_ref, wg_ref, wv_ref, o_ref, xn_ref):
    @pl.when(pl.program_id(1) == 0)
    def _():
        x = x_ref[...]
        ms = jnp.mean(x * x, axis=-1, keepdims=True)
        xn_ref[...] = (x * lax.rsqrt(ms + RMS_EPS) * g_ref[...]).astype(BF16)

    xn = xn_ref[...]
    gate = _dot(xn, wg_ref[...])
    val = _dot(xn, wv_ref[...])
    o_ref[...] = (_silu(gate) * val).astype(o_ref.dtype)


def _ffn_up(x, g, w_gu, layer, tm, tn):
    rows, d = x.shape
    d_ff = w_gu.shape[2] // 2
    nj = d_ff // tn
    return pl.pallas_call(
        _ffn_up_kernel,
        grid=(rows // tm, nj),
        in_specs=[pl.BlockSpec((tm, d), lambda i, j: (i, 0)),
                  pl.BlockSpec((1, d), lambda i, j: (0, 0)),
                  pl.BlockSpec((None, d, tn), lambda i, j: (layer, 0, j)),
                  pl.BlockSpec((None, d, tn), lambda i, j: (layer, 0, j + nj))],
        out_specs=pl.BlockSpec((tm, tn), lambda i, j: (i, j)),
        out_shape=jax.ShapeDtypeStruct((rows, d_ff), BF16),
        scratch_shapes=[pltpu.VMEM((tm, d), BF16)],
        compiler_params=_cparams(("parallel", "arbitrary")),
        name="ffn_up",
    )(x, g, w_gu, w_gu)


def _ffn_down_kernel(a_ref, w_ref, h_ref, o_ref):
    o_ref[...] = h_ref[...] + _dot(a_ref[...], w_ref[...])


def _ffn_down(act, w_down, layer, h, tm, tn):
    rows, d_ff = act.shape
    d = w_down.shape[2]
    return pl.pallas_call(
        _ffn_down_kernel,
        grid=(rows // tm, d // tn),
        in_specs=[pl.BlockSpec((tm, d_ff), lambda i, j: (i, 0)),
                  pl.BlockSpec((None, d_ff, tn), lambda i, j: (layer, 0, j)),
                  pl.BlockSpec((tm, tn), lambda i, j: (i, j))],
        out_specs=pl.BlockSpec((tm, tn), lambda i, j: (i, j)),
        out_shape=jax.ShapeDtypeStruct((rows, d), F32),
        compiler_params=_cparams(("parallel", "arbitrary")),
        name="ffn_down",
    )(act, w_down, h)


def _out_proj_kernel(n_att, x_ref, y_ref, *refs):
    att_refs = refs[:n_att]
    w1_ref, w2_ref, o_ref = refs[n_att:]
    att = att_refs[0][...]
    for r in att_refs[1:]:
        att = att + r[...]
    o_ref[...] = (x_ref[...] + _dot(y_ref[...], w1_ref[...])
                  + _dot(att.astype(BF16), w2_ref[...]))


def _out_proj(x, y_ssd, atts, w_out, layer, tm, tn):
    rows, d = x.shape
    half = y_ssd.shape[1]
    n_att = len(atts)
    return pl.pallas_call(
        functools.partial(_out_proj_kernel, n_att),
        grid=(rows // tm, d // tn),
        in_specs=([pl.BlockSpec((tm, tn), lambda i, j: (i, j)),
                   pl.BlockSpec((tm, half), lambda i, j: (i, 0))]
                  + [pl.BlockSpec((tm, half), lambda i, j: (i, 0)) for _ in atts]
                  + [pl.BlockSpec((None, half, tn), lambda i, j: (layer, 0, j)),
                     pl.BlockSpec((None, half, tn), lambda i, j: (layer, 1, j))]),
        out_specs=pl.BlockSpec((tm, tn), lambda i, j: (i, j)),
        out_shape=jax.ShapeDtypeStruct((rows, d), F32),
        compiler_params=_cparams(("parallel", "arbitrary")),
        name="out_proj",
    )(x, y_ssd, *atts, w_out, w_out)


def _prep_kernel(transposed, q_ref, ks_ref, kw_ref, sm_ref, gq_ref, gks_ref, gkw_ref,
                 qn_ref, kvs_ref, kvw_ref, khs_ref, khw_ref, gates_ref, *t_refs):
    qn = _head_rms(q_ref[...], gq_ref[...]) * (HEAD_DIM ** -0.5)
    qn_ref[...] = qn.astype(BF16)
    if transposed:
        qt_ref, vts_ref, vtw_ref = t_refs
        q2 = qn * LOG2_E
        for h in range(KV_HEADS):
            qt_ref[h] = q2[:, h * GQA * HEAD_DIM:(h + 1) * GQA * HEAD_DIM].T.astype(BF16)
    for src, g_ref, full_ref, kh_ref, vt_ref in ((ks_ref, gks_ref, kvs_ref, khs_ref, vts_ref if transposed else None),
                                                 (kw_ref, gkw_ref, kvw_ref, khw_ref, vtw_ref if transposed else None)):
        kv = src[...]
        kn = _head_rms(kv[:, :K_COLS], g_ref[...])
        v = kv[:, K_COLS:]
        full_ref[:, :K_COLS] = kn
        full_ref[:, K_COLS:] = v
        for h in range(KV_HEADS):
            kh_ref[h] = kn[:, h * HEAD_DIM:(h + 1) * HEAD_DIM].astype(BF16)
        if transposed:
            for pair in range(KV_HEADS // 2):
                vt = v[:, pair * LANES:(pair + 1) * LANES].T.astype(BF16)
                ones = jnp.ones((V_ROWS - HEAD_DIM, vt.shape[1]), BF16)
                vt_ref[2 * pair] = jnp.concatenate([vt[:HEAD_DIM], ones], axis=0)
                vt_ref[2 * pair + 1] = jnp.concatenate([vt[HEAD_DIM:], ones], axis=0)
    sig = _sigmoid(sm_ref[...])
    n_dt = LANES // 8
    for h in range(KV_HEADS):
        gates_ref[h] = pltpu.roll(sig, LANES - n_dt - 3 * GQA * h, axis=1)


def _prep(u, col, g_q, g_ks, g_kw, tm, transposed):
    rows = u.shape[0]
    nsa = ATT_HEADS * HEAD_DIM
    row_spec = lambda w, c: pl.BlockSpec((tm, w), lambda i, c=c: (i, c))
    head_out = jax.ShapeDtypeStruct((KV_HEADS, rows, HEAD_DIM), BF16)
    head_spec = pl.BlockSpec((KV_HEADS, tm, HEAD_DIM), lambda i: (0, i, 0))
    vec = pl.BlockSpec((1, LANES), lambda i: (0, 0))
    out_specs = [pl.BlockSpec((tm, nsa), lambda i: (i, 0)),
                 pl.BlockSpec((tm, KV_COLS), lambda i: (i, 0)),
                 pl.BlockSpec((tm, KV_COLS), lambda i: (i, 0)),
                 head_spec, head_spec,
                 pl.BlockSpec((KV_HEADS, tm, LANES), lambda i: (0, i, 0))]
    out_shape = [jax.ShapeDtypeStruct((rows, nsa), BF16),
                 jax.ShapeDtypeStruct((rows, KV_COLS), F32),
                 jax.ShapeDtypeStruct((rows, KV_COLS), F32),
                 head_out, head_out,
                 jax.ShapeDtypeStruct((KV_HEADS, rows, LANES), F32)]
    if transposed:
        vt_spec = pl.BlockSpec((KV_HEADS, V_ROWS, tm), lambda i: (0, 0, i))
        vt_out = jax.ShapeDtypeStruct((KV_HEADS, V_ROWS, rows), BF16)
        out_specs += [pl.BlockSpec((KV_HEADS, GQA * HEAD_DIM, tm), lambda i: (0, 0, i)), vt_spec, vt_spec]
        out_shape += [jax.ShapeDtypeStruct((KV_HEADS, GQA * HEAD_DIM, rows), BF16), vt_out, vt_out]
    return pl.pallas_call(
        functools.partial(_prep_kernel, transposed),
        grid=(rows // tm,),
        in_specs=[row_spec(nsa, col["q"]), row_spec(KV_COLS, col["kvs"]),
                  row_spec(KV_COLS, col["kvw"]), row_spec(LANES, col["small"]), vec, vec, vec],
        out_specs=out_specs,
        out_shape=out_shape,
        compiler_params=_cparams(("parallel",)),
        name="head_prep",
    )(u, u, u, u, g_q, g_ks, g_kw)


def _gated_norm(y, xs, z, dsk, gn):
    y = (y + dsk * xs) * _silu(z)
    ms = jnp.mean(y * y, axis=-1, keepdims=True)
    return y * lax.rsqrt(ms + RMS_EPS) * gn


def _ssd_prompt_kernel(xbc_ref, z_ref, sm_ref, cw_ref, cb_ref, dtb_ref, alog_ref, dsk_ref, gn_ref,
                       y_ref, hout_ref, xb_ref, st_ref):
    c = pl.program_id(0)
    q = SSD_CHUNK
    ssm_dim = z_ref.shape[1]
    gw = SSM_STATE
    pad = 8

    @pl.when(c == 0)
    def _():
        xb_ref[0:pad, :] = jnp.zeros((pad, xb_ref.shape[1]), F32)
        st_ref[...] = jnp.zeros(st_ref.shape, F32)

    xb_ref[pad:pad + q, :] = xbc_ref[...]
    conv = cb_ref[...]
    for k in range(CONV_K):
        conv = conv + cw_ref[k:k + 1, :] * xb_ref[pl.ds(pad - (CONV_K - 1) + k, q), :]
    xb_ref[0:pad, :] = xb_ref[q:q + pad, :]
    act = _silu(conv)
    xs = act[:, :ssm_dim]
    bm = act[:, ssm_dim:ssm_dim + SSM_GROUPS * gw].astype(BF16)
    cm = act[:, ssm_dim + SSM_GROUPS * gw:].astype(BF16)

    lane = lax.broadcasted_iota(jnp.int32, (1, LANES), 1)
    n_heads = ssm_dim // SSM_HEAD_DIM
    dt = jnp.where(lane < n_heads, _softplus(sm_ref[...] + dtb_ref[...]), 0.0)
    a = -jnp.exp(alog_ref[...])
    ri = lax.broadcasted_iota(jnp.int32, (q, q), 0)
    ci = lax.broadcasted_iota(jnp.int32, (q, q), 1)
    causal = ri >= ci
    tri = jnp.where(causal, 1.0, 0.0).astype(BF16)
    da_hi, da_mid, da_lo = _split3(dt * a)
    acum = _dot(tri, da_hi) + _dot(tri, da_mid) + _dot(tri, da_lo)
    acum_t = acum.T
    dt_t = dt.T
    e_acum = jnp.exp(acum)
    a_last = acum[q - 1:q, :]
    w_end = dt * jnp.exp(a_last - acum)
    e_last = jnp.exp(a_last)
    lo = lane < SSM_HEAD_DIM
    row_lo = lax.broadcasted_iota(jnp.int32, (LANES, 1), 0) < SSM_HEAD_DIM

    ys = []
    heads_per_group = n_heads // SSM_GROUPS
    for pr in range(n_heads // 2):
        h0, h1 = 2 * pr, 2 * pr + 1
        g = h0 // heads_per_group
        cg = cm[:, g * gw:(g + 1) * gw]
        bg = bm[:, g * gw:(g + 1) * gw]
        cb = _dot_nt(cg, bg)
        x2 = xs[:, pr * LANES:(pr + 1) * LANES]
        x2b = x2.astype(BF16)
        yd = []
        for h in (h0, h1):
            seg = acum[:, h:h + 1] - acum_t[h:h + 1, :]
            dec = jnp.where(causal, jnp.exp(jnp.where(causal, seg, 0.0)), 0.0)
            m = (cb * dec * dt_t[h:h + 1, :]).astype(BF16)
            yd.append(_dot(m, x2b))
        y_diag = jnp.where(lo, yd[0], yd[1])
        sp = st_ref[pr * LANES:(pr + 1) * LANES, :]
        ea = jnp.where(lo, e_acum[:, h0:h0 + 1], e_acum[:, h1:h1 + 1])
        y_off = _dot_nt(cg, sp.astype(BF16)) * ea
        w2 = jnp.where(lo, w_end[:, h0:h0 + 1], w_end[:, h1:h1 + 1])
        xw_t = (x2 * w2).T.astype(BF16)
        cd = jnp.where(row_lo, e_last[:, h0:h0 + 1], e_last[:, h1:h1 + 1])
        st_ref[pr * LANES:(pr + 1) * LANES, :] = sp * cd + _dot(xw_t, bg)
        ys.append(y_diag + y_off)
    y = jnp.concatenate(ys, axis=-1)
    y_ref[...] = _gated_norm(y, xs, z_ref[...], dsk_ref[...], gn_ref[...]).astype(y_ref.dtype)

    @pl.when(c == pl.num_programs(0) - 1)
    def _():
        hout_ref[...] = st_ref[...]


def _ssd_prompt(u, col, lw):
    t = u.shape[0]
    ssm_dim = lw["dsk"].shape[1]
    conv_ch = lw["conv_w"].shape[1]
    q = SSD_CHUNK
    full = lambda r, w: pl.BlockSpec((r, w), lambda c: (0, 0))
    return pl.pallas_call(
        _ssd_prompt_kernel,
        grid=(t // q,),
        in_specs=[pl.BlockSpec((q, conv_ch), lambda c: (c, col["xbc"])),
                  pl.BlockSpec((q, ssm_dim), lambda c: (c, col["z"])),
                  pl.BlockSpec((q, LANES), lambda c: (c, col["small"])),
                  full(CONV_K, conv_ch), full(1, conv_ch), full(1, LANES), full(1, LANES),
                  full(1, ssm_dim), full(1, ssm_dim)],
        out_specs=[pl.BlockSpec((q, ssm_dim), lambda c: (c, 0)),
                   pl.BlockSpec((ssm_dim, SSM_STATE), lambda c: (0, 0))],
        out_shape=[jax.ShapeDtypeStruct((t, ssm_dim), BF16),
                   jax.ShapeDtypeStruct((ssm_dim, SSM_STATE), F32)],
        scratch_shapes=[pltpu.VMEM((q + 8, conv_ch), F32),
                        pltpu.VMEM((ssm_dim, SSM_STATE), F32)],
        compiler_params=_cparams(("arbitrary",)),
        name="ssd_prompt",
    )(u, u, u, lw["conv_w"], lw["conv_b"], lw["dt_bias"], lw["a_log"], lw["dsk"], lw["ssm_norm_g"])


def _ssd_sample_kernel(xbc_ref, z_ref, sm_ref, cbuf_ref, h_ref, cw_ref, cb_ref, dtb_ref, alog_ref,
                       dsk_ref, gn_ref, exp_ref, y_ref, hout_ref):
    b = pl.program_id(0)
    ssm_dim = z_ref.shape[1]
    gw = SSM_STATE
    n_heads = ssm_dim // SSM_HEAD_DIM
    conv = cb_ref[...] + cw_ref[CONV_K - 1:CONV_K, :] * xbc_ref[pl.ds(b, 1), :]
    for k in range(CONV_K - 1):
        conv = conv + cw_ref[k:k + 1, :] * cbuf_ref[k, pl.ds(b, 1), :]
    act = _silu(conv)
    xs = act[:, :ssm_dim]
    lane = lax.broadcasted_iota(jnp.int32, (1, LANES), 1)
    dt_raw = jnp.broadcast_to(jnp.where(lane < n_heads, sm_ref[pl.ds(b, 1), :], 0.0), (8, LANES))
    dt = _softplus(_dot3(dt_raw, exp_ref[...])[0:1, :] + dtb_ref[...])
    dec = jnp.exp(dt * (-jnp.exp(alog_ref[...])))
    dtx = dt * xs
    row8 = lax.broadcasted_iota(jnp.int32, (8, 1), 0)

    def rows8(pieces):
        out = jnp.zeros((8, pieces[0].shape[1]), F32)
        for r, p in enumerate(pieces):
            out = jnp.where(row8 == r, p.astype(F32), out)
        return out.astype(BF16)

    dec8 = rows8(_split3(dec))
    dtx8 = rows8(_split3(dtx))
    ones8 = jnp.where(row8 < 3, 1.0, 0.0).astype(BF16) * jnp.ones((1, gw), BF16)
    gh = (n_heads // SSM_GROUPS) * SSM_HEAD_DIM
    ys = []
    for g in range(SSM_GROUPS):
        bg = act[:, ssm_dim + g * gw:ssm_dim + (g + 1) * gw].astype(BF16)
        cg = act[:, ssm_dim + (SSM_GROUPS + g) * gw:ssm_dim + (SSM_GROUPS + g + 1) * gw].astype(BF16)
        b8 = jnp.where(row8 < 3, 1.0, 0.0).astype(BF16) * bg
        c8 = jnp.where(row8 < 1, 1.0, 0.0).astype(BF16) * cg
        dec_col = _dot_tn(dec8[:, g * gh:(g + 1) * gh], ones8)
        dbx = _dot_tn(dtx8[:, g * gh:(g + 1) * gh], b8)
        s_new = h_ref[g * gh:(g + 1) * gh, :] * dec_col + dbx
        hout_ref[g * gh:(g + 1) * gh, :] = s_new
        ys.append(_dot_nt(c8, s_new.astype(BF16))[0:1, :])
    y = jnp.concatenate(ys, axis=-1)
    y_ref[pl.ds(b, 1), :] = _gated_norm(y, xs, z_ref[pl.ds(b, 1), :], dsk_ref[...], gn_ref[...])


def _ssd_sample(u, col, conv_t, h_all, h_base, lw):
    bs = u.shape[0]
    ssm_dim = lw["dsk"].shape[1]
    conv_ch = lw["conv_w"].shape[1]
    full = lambda r, w: pl.BlockSpec((r, w), lambda b: (0, 0))
    return pl.pallas_call(
        _ssd_sample_kernel,
        grid=(bs,),
        in_specs=[pl.BlockSpec((bs, conv_ch), lambda b: (0, col["xbc"])),
                  pl.BlockSpec((bs, ssm_dim), lambda b: (0, col["z"])),
                  pl.BlockSpec((bs, LANES), lambda b: (0, col["small"])),
                  pl.BlockSpec((CONV_K - 1, bs, conv_ch), lambda b: (0, 0, 0)),
                  pl.BlockSpec((None, ssm_dim, SSM_STATE), lambda b: (h_base + b, 0, 0)),
                  full(CONV_K, conv_ch), full(1, conv_ch), full(1, ssm_dim), full(1, ssm_dim),
                  full(1, ssm_dim), full(1, ssm_dim), full(LANES, ssm_dim)],
        out_specs=[pl.BlockSpec((bs, ssm_dim), lambda b: (0, 0)),
                   pl.BlockSpec((None, ssm_dim, SSM_STATE), lambda b: (b, 0, 0))],
        out_shape=[jax.ShapeDtypeStruct((bs, ssm_dim), F32),
                   jax.ShapeDtypeStruct((bs, ssm_dim, SSM_STATE), F32)],
        compiler_params=_cparams(("arbitrary",)),
        name="ssd_sample",
    )(u, u, u, conv_t, h_all, lw["conv_w"], lw["conv_b"], lw["dt_bias_e"], lw["a_log_e"], lw["dsk"],
      lw["ssm_norm_g"], lw["head_expand"])


CMP_EXTRA_CHUNKS = 8


def _compress_rows(buf_ref, xcat_ref, res_ref, pe_ref, w_ref, gk_ref):
    n_blocks = res_ref.shape[0] - CMP_EXTRA_CHUNKS
    rows = res_ref.shape[0]
    pe = pe_ref[...]
    pe_hi = pe.astype(BF16).astype(F32)
    pe_lo = pe - pe_hi
    half = L_CMP // 2
    acc = []
    for p in range(KV_COLS // LANES):
        tile = slice(p * LANES, (p + 1) * LANES)
        base = (n_blocks + 1) * CMP_STRIDE
        extra = [None, pe_hi[:half, tile], pe_lo[:half, tile], pe_hi[half:, tile], pe_lo[half:, tile], None, None]
        for c, piece in enumerate(extra):
            dst = buf_ref.at[p, base + c * CMP_STRIDE:base + (c + 1) * CMP_STRIDE, :]
            dst[...] = jnp.zeros((CMP_STRIDE, LANES), F32) if piece is None else piece
        for phase in range(CMP_STRIDE):
            xcat_ref[:, phase * LANES:(phase + 1) * LANES] = (
                buf_ref[p, pl.ds(phase, rows, stride=CMP_STRIDE), :].astype(BF16))
        res_ref[...] = _dot(xcat_ref[...], w_ref[p // 2])
        pe_term = (res_ref[n_blocks + 2:n_blocks + 3, :LANES] + res_ref[n_blocks + 3:n_blocks + 4, :LANES]
                   + res_ref[n_blocks + 4:n_blocks + 5, LANES:] + res_ref[n_blocks + 5:n_blocks + 6, LANES:])
        acc.append(res_ref[0:n_blocks, :LANES] + res_ref[1:n_blocks + 1, LANES:] + pe_term)
    ck = _head_rms(jnp.concatenate(acc[:2], axis=-1), gk_ref[...])
    cv = jnp.concatenate(acc[2:], axis=-1)
    return ck, cv


def _store_heads(ck, cv, ck_ref, cv_ref):
    for h in range(KV_HEADS):
        ck_ref[h] = ck[:, h * HEAD_DIM:(h + 1) * HEAD_DIM].astype(BF16)
        cv_ref[h] = cv[:, h * HEAD_DIM:(h + 1) * HEAD_DIM].astype(BF16)


def _compress_prompt_kernel(main_ref, nxt_ref, pe_ref, w_ref, gk_ref, ck_ref, cv_ref, buf_ref, xcat_ref, res_ref):
    rows = main_ref.shape[0]
    for p in range(KV_COLS // LANES):
        buf_ref[p, 0:rows, :] = main_ref[:, p * LANES:(p + 1) * LANES]
        buf_ref[p, rows:rows + CMP_STRIDE, :] = nxt_ref[:, p * LANES:(p + 1) * LANES]
    ck, cv = _compress_rows(buf_ref, xcat_ref, res_ref, pe_ref, w_ref, gk_ref)
    _store_heads(ck, cv, ck_ref, cv_ref)


def _compress_scratch(rows):
    chunks = rows // CMP_STRIDE + CMP_EXTRA_CHUNKS
    return [pltpu.VMEM((KV_COLS // LANES, chunks * CMP_STRIDE, LANES), F32),
            pltpu.VMEM((chunks, CMP_STRIDE * LANES), BF16),
            pltpu.VMEM((chunks, 2 * LANES), F32)]


def _compress_prompt(kvc_pad, lw, rows_per_step):
    t = kvc_pad.shape[0] - CMP_STRIDE
    nb = rows_per_step // CMP_STRIDE
    out = jax.ShapeDtypeStruct((KV_HEADS, t // CMP_STRIDE, HEAD_DIM), BF16)
    out_spec = pl.BlockSpec((KV_HEADS, nb, HEAD_DIM), lambda i: (0, i, 0))
    return pl.pallas_call(
        _compress_prompt_kernel,
        grid=(t // rows_per_step,),
        in_specs=[pl.BlockSpec((rows_per_step, KV_COLS), lambda i: (i, 0)),
                  pl.BlockSpec((CMP_STRIDE, KV_COLS), lambda i: ((i + 1) * nb, 0)),
                  pl.BlockSpec((L_CMP, KV_COLS), lambda i: (0, 0)),
                  pl.BlockSpec((2, CMP_STRIDE * LANES, 2 * LANES), lambda i: (0, 0, 0)),
                  pl.BlockSpec((1, LANES), lambda i: (0, 0))],
        out_specs=[out_spec, out_spec],
        out_shape=[out, out],
        scratch_shapes=_compress_scratch(rows_per_step),
        compiler_params=_cparams(("parallel",)),
        name="compress_prompt",
    )(kvc_pad, kvc_pad, lw["cmp_pe_rows"], lw["cmp_w_cat"], lw["g_kc"])


def _compress_sample_kernel(n_pg, pt_ref, *refs):
    page_refs = refs[:n_pg]
    nxt_ref, pe_ref, w_ref, gk_ref, ck_ref, cv_ref, buf_ref, xcat_ref, res_ref = refs[n_pg:]
    pg = page_refs[0].shape[1]
    for p in range(KV_COLS // LANES):
        for k in range(n_pg):
            buf_ref[p, k * pg:(k + 1) * pg, :] = page_refs[k][p * LANES:(p + 1) * LANES, :].T
        buf_ref[p, n_pg * pg:n_pg * pg + CMP_STRIDE, :] = nxt_ref[p * LANES:(p + 1) * LANES, :].T[:CMP_STRIDE]
    ck, cv = _compress_rows(buf_ref, xcat_ref, res_ref, pe_ref, w_ref, gk_ref)
    _store_heads(ck, cv, ck_ref, cv_ref)


def _compress_sample(cache_t, page_base, page_table, lw):
    bs, n_pages = page_table.shape
    pg = cache_t.shape[2]
    n_pg = _pick(n_pages, (2 * PAGES_PER_STEP, PAGES_PER_STEP))
    nb = n_pg * pg // CMP_STRIDE
    past = n_pages * pg

    def page_map(b, j, pt, k):
        return (page_base + pt[b, j * n_pg + k], 0, 0)

    def next_map(b, j, pt):
        return (page_base + pt[b, jnp.minimum((j + 1) * n_pg, n_pages - 1)], 0, 0)

    out = jax.ShapeDtypeStruct((bs, KV_HEADS, past // CMP_STRIDE, HEAD_DIM), BF16)
    out_spec = pl.BlockSpec((None, KV_HEADS, nb, HEAD_DIM), lambda b, j, pt: (b, 0, j, 0))
    grid_spec = pltpu.PrefetchScalarGridSpec(
        num_scalar_prefetch=1,
        grid=(bs, n_pages // n_pg),
        in_specs=([pl.BlockSpec((None, KV_COLS, pg), functools.partial(page_map, k=k)) for k in range(n_pg)]
                  + [pl.BlockSpec((None, KV_COLS, pg), next_map),
                     pl.BlockSpec((L_CMP, KV_COLS), lambda b, j, pt: (0, 0)),
                     pl.BlockSpec((2, CMP_STRIDE * LANES, 2 * LANES), lambda b, j, pt: (0, 0, 0)),
                     pl.BlockSpec((1, LANES), lambda b, j, pt: (0, 0))]),
        out_specs=[out_spec, out_spec],
        scratch_shapes=_compress_scratch(n_pg * pg),
    )
    return pl.pallas_call(
        functools.partial(_compress_sample_kernel, n_pg),
        grid_spec=grid_spec,
        out_shape=[out, out],
        compiler_params=_cparams(("parallel", "parallel")),
        name="compress_sample",
    )(page_table, *([cache_t] * (n_pg + 1)), lw["cmp_pe_rows"], lw["cmp_w_cat"], lw["g_kc"])


def _cover_matrix(nc, n_sel_blocks, width):
    n = lax.broadcasted_iota(jnp.int32, (nc, width), 0) * CMP_STRIDE
    s = lax.broadcasted_iota(jnp.int32, (nc, width), 1)
    hit = (n < s * L_SEL + L_SEL) & (n + L_CMP > s * L_SEL) & (s < n_sel_blocks)
    return jnp.where(hit, 1.0, 0.0).astype(BF16)


def _block_scores(imp, q_pos, n_sel_blocks):
    blk = lax.broadcasted_iota(jnp.int32, imp.shape, 1)
    valid = blk * L_SEL <= q_pos
    cur = q_pos // L_SEL
    forced = (blk == 0) | (blk == cur) | (blk == cur - 1)
    score = jnp.where(valid & forced, SEL_FORCE, jnp.where(valid, imp, -1.0))
    return jnp.where(blk < n_sel_blocks, score, -2.0)


def _cmp_prompt_kernel(q_ref, ck_ref, cv_ref, gates_ref, oc_ref, sel_ref, cnt_ref):
    i = pl.program_id(0)
    tq = q_ref.shape[0]
    nc = ck_ref.shape[1]
    n_sel_blocks = nc * CMP_STRIDE // L_SEL
    q_pos = i * tq + lax.broadcasted_iota(jnp.int32, (tq, 1), 0)
    n_end = lax.broadcasted_iota(jnp.int32, (1, nc), 1) * CMP_STRIDE + (L_CMP - 1)
    mask = (n_end <= q_pos) & (n_end < nc * CMP_STRIDE)
    cover = _cover_matrix(nc, n_sel_blocks, LANES)
    q = q_ref[...]
    outs = []
    pieces = []
    for h in range(KV_HEADS):
        q4 = jnp.concatenate([q[:, (h * GQA + g) * HEAD_DIM:(h * GQA + g + 1) * HEAD_DIM] for g in range(GQA)],
                             axis=0)
        s4 = _dot_nt(q4, ck_ref[h])
        ps = [_masked_softmax(s4[g * tq:(g + 1) * tq], mask) for g in range(GQA)]
        o4 = _dot(jnp.concatenate([p.astype(BF16) for p in ps], axis=0), cv_ref[h])
        gates = gates_ref[h]
        outs += [o4[g * tq:(g + 1) * tq] * gates[:, 3 * g:3 * g + 1] for g in range(GQA)]
        pieces += list(_split3(ps[0] + ps[1] + ps[2] + ps[3]))
    oc_ref[...] = jnp.concatenate(outs, axis=-1)
    imp_all = _dot(jnp.concatenate(pieces, axis=0), cover)
    for h in range(KV_HEADS):
        imp = sum(imp_all[(3 * h + r) * tq:(3 * h + r + 1) * tq] for r in range(3))
        score_t = _block_scores(imp, q_pos, n_sel_blocks).T
        live_groups = (i * tq + tq - 1) // (L_SEL * RANK_GROUP) + 1
        sel_ref[h] = _top_k_mask_t(score_t, cnt_ref, N_SEL, n_sel_blocks, live_groups)


def _cmp_prompt(qn, ck, cv, gates_h, t):
    nsa = qn.shape[1]
    nc = ck.shape[1]
    tq = Q_TILE
    kv_spec = pl.BlockSpec((KV_HEADS, nc, HEAD_DIM), lambda i: (0, 0, 0))
    return pl.pallas_call(
        _cmp_prompt_kernel,
        grid=(t // tq,),
        in_specs=[pl.BlockSpec((tq, nsa), lambda i: (i, 0)), kv_spec, kv_spec,
                  pl.BlockSpec((KV_HEADS, tq, LANES), lambda i: (0, i, 0))],
        out_specs=[pl.BlockSpec((tq, nsa), lambda i: (i, 0)),
                   pl.BlockSpec((KV_HEADS, LANES, tq), lambda i: (0, 0, i))],
        out_shape=[jax.ShapeDtypeStruct((t, nsa), F32),
                   jax.ShapeDtypeStruct((KV_HEADS, LANES, t), F32)],
        scratch_shapes=[pltpu.VMEM((LANES, tq), F32)],
        compiler_params=_cparams(("parallel",)),
        name="cmp_topk_prompt",
    )(qn, ck, cv, gates_h)


def _tile_softmax_t(s_t, bias_t, tq, maxes, p_ref):
    new_maxes, alphas = [], []
    for g in range(GQA):
        parts = []
        for c in range(tq // LANES):
            at = slice(g * tq + c * LANES, g * tq + (c + 1) * LANES)
            sb = s_t[:, at] + bias_t[:, c * LANES:(c + 1) * LANES]
            m_old = maxes[g][:, c * LANES:(c + 1) * LANES]
            m_new = jnp.maximum(m_old, jnp.max(sb, axis=0, keepdims=True))
            alphas.append(jnp.exp2(m_old - m_new))
            p_ref[:, at] = jnp.exp2(sb - m_new).astype(BF16)
            parts.append(m_new)
        new_maxes.append(jnp.concatenate(parts, axis=1))
    return tuple(new_maxes), jnp.concatenate(alphas, axis=1)


def _stack_q_t(qt_ref):
    return jnp.concatenate([qt_ref[g * HEAD_DIM:(g + 1) * HEAD_DIM, :] for g in range(GQA)], axis=1)


def _finish_t(acc_t, gates, branch, tq):
    o_t = acc_t[:HEAD_DIM] * (1.0 / acc_t[HEAD_DIM:HEAD_DIM + 1])
    lane = lax.broadcasted_iota(jnp.int32, (1, LANES), 1)
    outs = []
    for pair in range(GQA // 2):
        g0, g1 = 2 * pair, 2 * pair + 1
        two = jnp.concatenate([o_t[:, g0 * tq:(g0 + 1) * tq], o_t[:, g1 * tq:(g1 + 1) * tq]], axis=0)
        gate = jnp.where(lane < HEAD_DIM, gates[:, 3 * g0 + branch:3 * g0 + branch + 1],
                         gates[:, 3 * g1 + branch:3 * g1 + branch + 1])
        outs.append(two.T * gate)
    return jnp.concatenate(outs, axis=1)


def _init_maxes(tq):
    return tuple(jnp.full((1, tq), NEG_BIG, F32) for _ in range(GQA))


def _sel_prompt_kernel(qt_ref, k_ref, vt_ref, sel_ref, gates_ref, o_ref, p0_ref, p1_ref, acc0_ref, acc1_ref):
    p_refs, acc_refs = (p0_ref, p1_ref), (acc0_ref, acc1_ref)
    i = pl.program_id(1)
    tq = sel_ref.shape[1]
    tk = SEL_KEY_TILE
    bpt = tk // L_SEL
    q4t = _stack_q_t(qt_ref)
    q_pos = i * tq + lax.broadcasted_iota(jnp.int32, (1, tq), 1)
    key_off = lax.broadcasted_iota(jnp.int32, (L_SEL, 1), 0)
    n_tiles = (i * tq + tq - 1) // tk + 1
    last_tile = k_ref.shape[0] // tk - 1
    for acc_ref in acc_refs:
        acc_ref[...] = jnp.zeros(acc_ref.shape, F32)

    def tile(j, maxes, p_ref, acc_ref):
        live = (j < n_tiles).astype(jnp.int32)
        jc = jnp.minimum(j, last_tile)
        at = pl.ds(pl.multiple_of(jc * tk, tk), tk)
        q_lim = q_pos * live - (1 - live)
        rows = []
        for blk in range(bpt):
            picked = jnp.where(sel_ref[pl.ds(jc * bpt + blk, 1), :] > 0.5, 0.0, NEG_BIG)
            visible = key_off + (jc * tk + blk * L_SEL) <= q_lim
            rows.append(jnp.where(visible, picked, NEG_BIG))
        bias_t = jnp.concatenate(rows, axis=0)
        maxes, alpha = _tile_softmax_t(_dot(k_ref[at, :], q4t), bias_t, tq, maxes, p_ref)
        acc_ref[...] = alpha * acc_ref[...] + _dot(vt_ref[:, at], p_ref[...])
        return maxes

    def body(jj, carry):
        return tuple(tile(2 * jj + c, carry[c], p_refs[c], acc_refs[c]) for c in range(2))

    chains = lax.fori_loop(0, (n_tiles + 1) // 2, body, (_init_maxes(tq), _init_maxes(tq)))
    scale = [[], []]
    for m0, m1 in zip(*chains):
        m = jnp.maximum(m0, m1)
        scale[0].append(jnp.exp2(m0 - m))
        scale[1].append(jnp.exp2(m1 - m))
    acc = sum(jnp.concatenate(scale[c], axis=1) * acc_refs[c][...] for c in range(2))
    o_ref[...] = _finish_t(acc, gates_ref[...], 1, tq)


def _attn_specs(t, tq):
    return [pl.BlockSpec((None, GQA * HEAD_DIM, tq), lambda h, i: (h, 0, i)),
            pl.BlockSpec((None, t, HEAD_DIM), lambda h, i: (h, 0, 0)),
            pl.BlockSpec((None, V_ROWS, t), lambda h, i: (h, 0, 0)),
            pl.BlockSpec((None, tq, LANES), lambda h, i: (h, i, 0))]


def _sel_prompt(q_t, k_h, v_t, sel_t, gates_h, t):
    tq = _pick(t, (SEL_Q_TILE, Q_TILE))
    q_spec, k_spec, v_spec, g_spec = _attn_specs(t, tq)
    return pl.pallas_call(
        _sel_prompt_kernel,
        grid=(KV_HEADS, t // tq),
        in_specs=[q_spec, k_spec, v_spec, pl.BlockSpec((None, LANES, tq), lambda h, i: (h, 0, i)), g_spec],
        out_specs=pl.BlockSpec((tq, GQA * HEAD_DIM), lambda h, i: (i, h)),
        out_shape=jax.ShapeDtypeStruct((t, ATT_HEADS * HEAD_DIM), F32),
        scratch_shapes=([pltpu.VMEM((SEL_KEY_TILE, GQA * tq), BF16)] * 2
                        + [pltpu.VMEM((V_ROWS, GQA * tq), F32)] * 2),
        compiler_params=_cparams(("parallel", "arbitrary")),
        name="sel_prompt",
    )(q_t, k_h, v_t, sel_t, gates_h)


def _win_prompt_kernel(qt_ref, k_ref, vt_ref, gates_ref, o_ref, p_ref):
    i = pl.program_id(1)
    tq = gates_ref.shape[0]
    span = WINDOW + tq
    at = pl.ds(pl.multiple_of(jnp.maximum(i * tq - WINDOW, 0), tq), span)
    q_pos = i * tq + lax.broadcasted_iota(jnp.int32, (1, tq), 1)
    k_pos = jnp.maximum(i * tq - WINDOW, 0) + lax.broadcasted_iota(jnp.int32, (span, 1), 0)
    bias_t = jnp.where((k_pos <= q_pos) & (k_pos >= q_pos - WINDOW), 0.0, NEG_BIG)
    _tile_softmax_t(_dot(k_ref[at, :], _stack_q_t(qt_ref)), bias_t, tq, _init_maxes(tq), p_ref)
    o_ref[...] = _finish_t(_dot(vt_ref[:, at], p_ref[...]), gates_ref[...], 2, tq)


def _win_prompt(q_t, k_h, v_t, gates_h, t):
    tq = _pick(t, (SEL_Q_TILE, Q_TILE))
    return pl.pallas_call(
        _win_prompt_kernel,
        grid=(KV_HEADS, t // tq),
        in_specs=_attn_specs(t, tq),
        out_specs=pl.BlockSpec((tq, GQA * HEAD_DIM), lambda h, i: (i, h)),
        out_shape=jax.ShapeDtypeStruct((t, ATT_HEADS * HEAD_DIM), F32),
        scratch_shapes=[pltpu.VMEM((WINDOW + tq, GQA * tq), BF16)],
        compiler_params=_cparams(("parallel", "arbitrary")),
        name="win_prompt",
    )(q_t, k_h, v_t, gates_h)


def _own_head_rows(parts, width):
    row_head = lax.broadcasted_iota(jnp.int32, (ATT_HEADS, 1), 0) // GQA
    out = jnp.zeros((ATT_HEADS, width), F32)
    for h in range(KV_HEADS):
        out = jnp.where(row_head == h, parts[h], out)
    return out


def _cmp_sample_kernel(past, q_ref, ck_ref, cv_ref, g3_ref, oc_ref, sel_ref, score_ref, cnt_ref):
    b = pl.program_id(0)
    nc = ck_ref.shape[1]
    width = sel_ref.shape[1]
    n_sel_blocks = past // L_SEL + 1
    q = q_ref[...]
    n_end = lax.broadcasted_iota(jnp.int32, (1, nc), 1) * CMP_STRIDE + (L_CMP - 1)
    mask = n_end <= past
    s = _own_head_rows([_dot_nt(q, ck_ref[h]) for h in range(KV_HEADS)], nc)
    p = _masked_softmax(s, mask)
    pb = p.astype(BF16)
    o = _own_head_rows([_dot(pb, cv_ref[h]) for h in range(KV_HEADS)], HEAD_DIM)
    oc_ref[...] = o * g3_ref[:, 0:1]
    row_head = lax.broadcasted_iota(jnp.int32, (ATT_HEADS, 1), 0) // GQA
    row8 = lax.broadcasted_iota(jnp.int32, (8, 1), 0)
    p_sum = jnp.zeros((8, nc), F32)
    for h in range(KV_HEADS):
        ph = jnp.sum(jnp.where(row_head == h, p, 0.0), axis=0, keepdims=True)
        p_sum = jnp.where(row8 == h, ph, p_sum)
    imp = _dot3(p_sum, _cover_matrix(nc, n_sel_blocks, width))
    q_pos = jnp.full((8, 1), past, jnp.int32)
    score_ref[pl.ds(pl.multiple_of(b * 8, 8), 8), :] = _block_scores(imp, q_pos, n_sel_blocks)

    @pl.when(b == pl.num_programs(0) - 1)
    def _():
        sel_ref[...] = _top_k_mask_t(score_ref[...].T, cnt_ref, N_SEL, n_sel_blocks).T


def _cmp_sample(q3, ck_s, cv_s, g3, past):
    bs = q3.shape[0]
    nc = ck_s.shape[2]
    width = -(-(past // L_SEL + 1) // LANES) * LANES
    kv_spec = pl.BlockSpec((None, KV_HEADS, nc, HEAD_DIM), lambda b: (b, 0, 0, 0))
    oc, sel = pl.pallas_call(
        functools.partial(_cmp_sample_kernel, past),
        grid=(bs,),
        in_specs=[pl.BlockSpec((None, ATT_HEADS, HEAD_DIM), lambda b: (b, 0, 0)), kv_spec, kv_spec,
                  pl.BlockSpec((None, ATT_HEADS, LANES), lambda b: (b, 0, 0))],
        out_specs=[pl.BlockSpec((None, ATT_HEADS, HEAD_DIM), lambda b: (b, 0, 0)),
                   pl.BlockSpec((bs * 8, width), lambda b: (0, 0))],
        out_shape=[jax.ShapeDtypeStruct((bs, ATT_HEADS, HEAD_DIM), F32),
                   jax.ShapeDtypeStruct((bs * 8, width), F32)],
        scratch_shapes=[pltpu.VMEM((bs * 8, width), F32), pltpu.VMEM((width, bs * 8), F32)],
        compiler_params=_cparams(("arbitrary",)),
        name="cmp_topk_sample",
    )(q3, ck_s, cv_s, g3)
    return oc, sel.reshape(bs, 8, width)


def _diag_heads(x):
    return _own_head_rows([x[:, h * HEAD_DIM:(h + 1) * HEAD_DIM] for h in range(KV_HEADS)], HEAD_DIM)


def _sel_win_sample_kernel(past, n_pg, pt_ref, *refs):
    page_refs = refs[:n_pg]
    (qbd_ref, selg_ref, selnew_ref, knew_ref, win_ref, wnew_ref, oc_ref, g3_ref,
     o_ref, m_ref, l_ref, acc_ref, kv_ref) = refs[n_pg:]
    j = pl.program_id(1)
    pg = page_refs[0].shape[1]
    qbd = qbd_ref[...]
    bpp = pg // L_SEL

    @pl.when(j == 0)
    def _():
        m_ref[...] = jnp.full(m_ref.shape, NEG_BIG, F32)
        l_ref[...] = jnp.zeros(l_ref.shape, F32)
        acc_ref[...] = jnp.zeros(acc_ref.shape, F32)

    sel = selg_ref[...].astype(BF16)
    n_blk = n_pg * bpp
    expand = jnp.where(lax.broadcasted_iota(jnp.int32, (n_blk, n_pg * pg), 0)
                       == lax.broadcasted_iota(jnp.int32, (n_blk, n_pg * pg), 1) // L_SEL, 1.0, 0.0).astype(BF16)
    picked = _dot(sel, expand)
    for k in range(n_pg):
        kv_ref[:, k * pg:(k + 1) * pg] = page_refs[k][...].astype(BF16)
    m, l, acc = m_ref[...], l_ref[...], acc_ref[...]
    s = _dot(qbd, kv_ref[0:K_COLS, :]) + jnp.where(picked > 0.5, 0.0, NEG_BIG)
    m_new = jnp.maximum(m, jnp.max(s, axis=-1, keepdims=True))
    alpha = jnp.exp(m - m_new)
    p = jnp.exp(s - m_new)
    l = alpha * l + jnp.sum(p, axis=-1, keepdims=True)
    acc = alpha * acc + _dot_nt(p.astype(BF16), kv_ref[K_COLS:, :])
    m = m_new
    m_ref[...], l_ref[...], acc_ref[...] = m, l, acc

    @pl.when(j == pl.num_programs(1) - 1)
    def _():
        qf = qbd.astype(F32)
        kn = knew_ref[...]
        s_new = jnp.sum(qf * kn[:, :K_COLS].astype(BF16).astype(F32), axis=-1, keepdims=True)
        ok_new = selnew_ref[:, 0:1] > 0.5
        s_new = jnp.where(ok_new, s_new, NEG_BIG)
        m2 = jnp.maximum(m, s_new)
        a2 = jnp.exp(m - m2)
        p_new = jnp.where(ok_new, jnp.exp(s_new - m2), 0.0)
        l2 = a2 * l + p_new
        acc2 = a2 * acc + p_new.astype(BF16).astype(F32) * kn[:, K_COLS:].astype(BF16).astype(F32)
        o_s = _diag_heads(acc2 / jnp.maximum(l2, 1e-30))
        w = win_ref[...].astype(BF16)
        w_buf = w.shape[1]
        wn = wnew_ref[...]
        w_pos = past - w_buf + lax.broadcasted_iota(jnp.int32, (1, w_buf), 1)
        ok_w = (w_pos <= past) & (w_pos >= past - WINDOW) & (w_pos >= 0)
        s_w = jnp.where(ok_w, _dot(qbd, w[:K_COLS, :]), -jnp.inf)
        s_wn = jnp.sum(qf * wn[:, :K_COLS].astype(BF16).astype(F32), axis=-1, keepdims=True)
        m_w = jnp.maximum(jnp.max(s_w, axis=-1, keepdims=True), s_wn)
        p_w = jnp.where(ok_w, jnp.exp(s_w - m_w), 0.0)
        p_wn = jnp.exp(s_wn - m_w)
        l_w = jnp.sum(p_w, axis=-1, keepdims=True) + p_wn
        p_w = p_w / l_w
        p_wn = p_wn / l_w
        o_w = _diag_heads(_dot_nt(p_w.astype(BF16), w[K_COLS:, :])
                          + p_wn.astype(BF16).astype(F32) * wn[:, K_COLS:].astype(BF16).astype(F32))
        g3 = g3_ref[...]
        o_ref[...] = oc_ref[...] + g3[:, 1:2] * o_s + g3[:, 2:3] * o_w


def _sel_win_sample(cache_t, page_base, page_table, qbd, sel_groups, sel_new, kvs_new, win_t, win_base,
                    kvw_new, oc, g3, past):
    bs, n_pages = page_table.shape
    pg = cache_t.shape[2]
    n_pg = _decode_pages_per_step(n_pages)
    bps = n_pg * pg // L_SEL
    w_buf = win_t.shape[2]

    def page_map(b, j, pt, k):
        return (page_base + pt[b, j * n_pg + k], 0, 0)

    per_b = lambda *shape: pl.BlockSpec((None,) + shape, lambda b, j, pt: (b,) + (0,) * len(shape))
    grid_spec = pltpu.PrefetchScalarGridSpec(
        num_scalar_prefetch=1,
        grid=(bs, n_pages // n_pg),
        in_specs=([pl.BlockSpec((None, KV_COLS, pg), functools.partial(page_map, k=k)) for k in range(n_pg)]
                  + [per_b(ATT_HEADS, K_COLS),
                     pl.BlockSpec((None, None, ATT_HEADS, bps), lambda b, j, pt: (b, j, 0, 0)),
                     per_b(ATT_HEADS, LANES), per_b(1, KV_COLS),
                     pl.BlockSpec((None, KV_COLS, w_buf), lambda b, j, pt: (win_base + b, 0, 0)),
                     per_b(1, KV_COLS), per_b(ATT_HEADS, HEAD_DIM), per_b(ATT_HEADS, LANES)]),
        out_specs=per_b(ATT_HEADS, HEAD_DIM),
        scratch_shapes=[pltpu.VMEM((ATT_HEADS, 1), F32), pltpu.VMEM((ATT_HEADS, 1), F32),
                        pltpu.VMEM((ATT_HEADS, K_COLS), F32), pltpu.VMEM((KV_COLS, n_pg * pg), BF16)],
    )
    return pl.pallas_call(
        functools.partial(_sel_win_sample_kernel, past, n_pg),
        grid_spec=grid_spec,
        out_shape=jax.ShapeDtypeStruct((bs, ATT_HEADS, HEAD_DIM), F32),
        compiler_params=_cparams(("parallel", "arbitrary")),
        name="sel_win_sample",
    )(page_table, *([cache_t] * n_pg), qbd, sel_groups, sel_new, kvs_new, win_t, kvw_new, oc, g3)


def _column_layout(d_model):
    ssm_dim = d_model // 2
    conv_ch = ssm_dim + 2 * SSM_GROUPS * SSM_STATE
    nsa = d_model - ssm_dim
    n_heads = ssm_dim // SSM_HEAD_DIM
    sizes = (ssm_dim, conv_ch, n_heads, nsa, KV_COLS, KV_COLS, KV_COLS, 3 * ATT_HEADS)
    offs = [0]
    for s in sizes:
        offs.append(offs[-1] + s)
    src = dict(zip(("z", "xbc", "dt", "q", "kvc", "kvs", "kvw", "gt"), zip(offs[:-1], sizes)))
    order = ("xbc", "z", "q", "kvc", "kvs", "kvw")
    col, start = {}, {}
    pos = 0
    for name in order:
        o, w = src[name]
        assert pos % w == 0
        col[name] = pos // w
        start[name] = pos
        pos += w
    col["small"] = pos // LANES
    start["small"] = pos
    return src, order, col, start, pos + LANES


def _dense_weights(p, d_model):
    src, order, _, _, _ = _column_layout(d_model)
    w_in = p["w_in"]
    n_heads = src["dt"][1]
    small = jnp.concatenate([w_in[:, :, src["dt"][0]:src["dt"][0] + n_heads],
                             w_in[:, :, src["gt"][0]:src["gt"][0] + src["gt"][1]]], axis=2)
    small = jnp.pad(small, ((0, 0), (0, 0), (0, LANES - small.shape[2])))
    w_in_packed = jnp.concatenate([w_in[:, :, src[n][0]:src[n][0] + src[n][1]] for n in order] + [small],
                                  axis=2).astype(BF16)
    return {"w_in": w_in_packed, "w_out": p["w_out"].astype(BF16), "w_gu": p["w_gu"].astype(BF16),
            "w_down": p["w_down"].astype(BF16)}


def _layer_weights(l, p, d_model):
    src, order, col, start, n_cols = _column_layout(d_model)
    ssm_dim = src["z"][1]
    rep = lambda v: jnp.repeat(v, SSM_HEAD_DIM)[None, :]
    pad_l = lambda v: jnp.pad(v, (0, LANES - v.shape[0]))[None, :]
    tile2 = lambda v: jnp.tile(v, 2)[None, :]
    cw = jnp.transpose(p["cmp_w"][l], (1, 0, 2, 3))
    zero = jnp.zeros_like(cw)
    pairs = jnp.concatenate([jnp.concatenate([cw, zero], axis=-1),
                             jnp.concatenate([zero, cw], axis=-1)], axis=-2)
    halves = jnp.transpose(pairs, (1, 0, 2, 3)).reshape(2, 2, CMP_STRIDE * LANES, LANES)
    w_cat = jnp.concatenate([halves[:, 0], halves[:, 1]], axis=-1).astype(BF16)
    pe = p["cmp_pe"][l]
    pe_rows = jnp.broadcast_to(pe[:, :, None, :], (L_CMP, 2, KV_HEADS, HEAD_DIM)).reshape(L_CMP, KV_COLS)
    head_expand = (jnp.arange(LANES)[:, None] == jnp.arange(ssm_dim)[None, :] // SSM_HEAD_DIM).astype(BF16)
    return {
        "col": col, "start": start,
        "norm1_g": p["norm1_g"][l][None, :],
        "conv_w": p["conv_w"][l], "conv_b": p["conv_b"][l][None, :],
        "dt_bias": pad_l(p["dt_bias"][l]), "a_log": pad_l(p["a_log"][l]),
        "dt_bias_e": rep(p["dt_bias"][l]), "a_log_e": rep(p["a_log"][l]),
        "dsk": rep(p["d_skip"][l]), "ssm_norm_g": p["ssm_norm_g"][l][None, :],
        "head_expand": head_expand,
        "g_q": tile2(p["q_norm_g"][l]), "g_kc": tile2(p["k_norm_g"][l, 0]),
        "g_ks": tile2(p["k_norm_g"][l, 1]), "g_kw": tile2(p["k_norm_g"][l, 2]),
        "cmp_pe_rows": pe_rows, "cmp_w_cat": w_cat,
        "norm2_g": p["norm2_g"][l][None, :],
    }


def _decode_pages_per_step(n_pages):
    return _pick(n_pages, (2 * PAGES_PER_STEP, PAGES_PER_STEP))


def _pick(n, prefs):
    for c in prefs:
        if n % c == 0:
            return c
    return n


def _dense_tail(x, y_ssd, atts, lw, big, layer, tm):
    d = x.shape[1]
    h = _out_proj(x, y_ssd, atts, big["w_out"], layer, tm, _pick(d, (512,)))
    d_ff = big["w_down"].shape[1]
    act = _ffn_up(h, lw["norm2_g"], big["w_gu"], layer, tm, _pick(d_ff, (512, 256, 128)))
    return _ffn_down(act, big["w_down"], layer, h, tm, _pick(d, (512,)))


def kernel(x_prompt, x_sample, cache_cmp_kv, cache_slc_kv, state_win_kv, state_ssm, state_conv, page_table,
           norm1_g, w_in, conv_w, conv_b, dt_bias, a_log, d_skip, ssm_norm_g, q_norm_g, k_norm_g,
           cmp_pe, cmp_w, w_out, norm2_g, w_gu, w_down):
    params = dict(norm1_g=norm1_g, w_in=w_in, conv_w=conv_w, conv_b=conv_b, dt_bias=dt_bias, a_log=a_log,
                  d_skip=d_skip, ssm_norm_g=ssm_norm_g, q_norm_g=q_norm_g, k_norm_g=k_norm_g, cmp_pe=cmp_pe,
                  cmp_w=cmp_w, w_out=w_out, norm2_g=norm2_g, w_gu=w_gu, w_down=w_down)
    bp, t, d = x_prompt.shape
    bs, dec_t, _ = x_sample.shape
    depth = w_in.shape[0]
    n_pool, pg = cache_cmp_kv.shape[1:3]
    n_pages = page_table.shape[1]
    past = n_pages * pg
    w_buf = state_win_kv.shape[2]
    assert bp == 1 and dec_t == 1
    assert t % (PAGES_PER_STEP * pg) == 0 and past % (PAGES_PER_STEP * pg) == 0 and t // L_SEL <= LANES
    ssm_dim = d // 2
    n_heads = ssm_dim // SSM_HEAD_DIM
    kv_shape = (2, KV_HEADS, HEAD_DIM)

    xp = x_prompt[0]
    xs = x_sample[:, 0]
    feature_major = lambda a: jnp.transpose(a, (0, 1, 3, 4, 5, 2)).reshape(a.shape[0] * a.shape[1], KV_COLS,
                                                                           a.shape[2])
    cmp_pages_t = feature_major(cache_cmp_kv)
    slc_pages_t = feature_major(cache_slc_kv)
    win_t = feature_major(state_win_kv)
    cmp_rows = _pick(t, (2 * PAGES_PER_STEP * pg, PAGES_PER_STEP * pg))
    outs = {k: [] for k in ("cmp_p", "cmp_s", "slc_p", "slc_s", "win_p", "win_s", "ssm_p", "ssm_s",
                            "conv_p", "conv_s")}
    tm_p = _pick(t, (512, 256, 128))
    big = _dense_weights(params, d)
    ssm_all = state_ssm.reshape(depth * bs, ssm_dim, SSM_STATE)
    tn_in = _pick(big["w_in"].shape[2], (1152, 640, 384, 128))
    for l in range(depth):
        lw = _layer_weights(l, params, d)
        col, start = lw["col"], lw["start"]

        u = _norm_matmul(xp, lw["norm1_g"], big["w_in"], l, tm_p, tn_in)
        qn, kvs_n, kvw_n, k_s, k_w, gates_h, q_t, vt_s, vt_w = _prep(u, col, lw["g_q"], lw["g_ks"], lw["g_kw"],
                                                                     _pick(t, (256, 128)), True)
        y_ssd, h_fin = _ssd_prompt(u, col, lw)
        kvc = u[:, start["kvc"]:start["kvc"] + KV_COLS]
        xbc_tail = u[t - (CONV_K - 1):, start["xbc"]:start["xbc"] + lw["conv_w"].shape[1]]
        ck, cv = _compress_prompt(jnp.pad(kvc, ((0, CMP_STRIDE), (0, 0))), lw, cmp_rows)
        o_c, sel_t = _cmp_prompt(qn, ck, cv, gates_h, t)
        o_s = _sel_prompt(q_t, k_s, vt_s, sel_t, gates_h, t)
        o_w = _win_prompt(q_t, k_w, vt_w, gates_h, t)
        xp = _dense_tail(xp, y_ssd, [o_c, o_s, o_w], lw, big, l, tm_p)
        outs["cmp_p"].append(kvc.reshape((1, t) + kv_shape))
        outs["slc_p"].append(kvs_n.reshape((1, t) + kv_shape))
        outs["win_p"].append(kvw_n[t - min(WINDOW, t):].reshape((1, min(WINDOW, t)) + kv_shape))
        outs["ssm_p"].append(h_fin.reshape(1, n_heads, SSM_HEAD_DIM, SSM_STATE))
        outs["conv_p"].append(xbc_tail[None])

        us = _norm_matmul(xs, lw["norm1_g"], big["w_in"], l, bs, tn_in)
        qn_s, kvs_s, kvw_s, _, _, gates_s = _prep(us, col, lw["g_q"], lw["g_ks"], lw["g_kw"], bs, False)
        conv_t = jnp.transpose(state_conv[l], (1, 0, 2))
        y_ssd_s, h_new = _ssd_sample(us, col, conv_t, ssm_all, l * bs, lw)
        xbc_s = us[:, start["xbc"]:start["xbc"] + lw["conv_w"].shape[1]]
        kvc_s = us[:, start["kvc"]:start["kvc"] + KV_COLS]
        ck_s, cv_s = _compress_sample(cmp_pages_t, l * n_pool, page_table, lw)
        q3 = qn_s.reshape(bs, ATT_HEADS, HEAD_DIM)
        g3 = jnp.transpose(gates_s[:, :, :3 * GQA].reshape(KV_HEADS, bs, GQA, 3), (1, 0, 2, 3))
        g3 = jnp.pad(g3.reshape(bs, ATT_HEADS, 3), ((0, 0), (0, 0), (0, LANES - 3)))
        oc_s, sel_s = _cmp_sample(q3, ck_s, cv_s, g3, past)
        sel16 = jnp.repeat(sel_s[:, :KV_HEADS], GQA, axis=1)
        bps = _decode_pages_per_step(n_pages) * pg // L_SEL
        sel_groups = jnp.transpose(sel16[:, :, :past // L_SEL].reshape(bs, ATT_HEADS, past // L_SEL // bps, bps),
                                   (0, 2, 1, 3))
        sel_new = jnp.pad(sel16[:, :, past // L_SEL:past // L_SEL + 1], ((0, 0), (0, 0), (0, LANES - 1)))
        head_of_col = jnp.arange(K_COLS) // HEAD_DIM
        qbd = jnp.where(head_of_col[None, None, :] == (jnp.arange(ATT_HEADS) // GQA)[None, :, None],
                        jnp.tile(q3, (1, 1, KV_HEADS)), jnp.zeros((), BF16))
        o_att_s = _sel_win_sample(slc_pages_t, l * n_pool, page_table, qbd, sel_groups, sel_new,
                                  kvs_s[:, None, :], win_t, l * bs, kvw_s[:, None, :], oc_s, g3, past)
        xs = _dense_tail(xs, y_ssd_s.astype(BF16), [o_att_s.reshape(bs, ATT_HEADS * HEAD_DIM)], lw, big, l, bs)
        outs["cmp_s"].append(kvc_s.reshape((bs, 1) + kv_shape))
        outs["slc_s"].append(kvs_s.reshape((bs, 1) + kv_shape))
        outs["win_s"].append(kvw_s.reshape((bs, 1) + kv_shape))
        outs["ssm_s"].append(h_new.reshape(bs, n_heads, SSM_HEAD_DIM, SSM_STATE))
        outs["conv_s"].append(jnp.concatenate([state_conv[l], xbc_s[:, None, :]], axis=1)[:, -(CONV_K - 1):])

    st = lambda k: jnp.stack(outs[k])
    win_s = jnp.concatenate([state_win_kv, st("win_s")], axis=2)[:, :, -w_buf:]
    return (xp[None], xs[:, None], st("cmp_p"), st("cmp_s"), st("slc_p"), st("slc_s"), st("win_p"),
            win_s, st("ssm_p"), st("ssm_s"), st("conv_p"), st("conv_s"))
```

```python
import functools

import jax
import jax.numpy as jnp
from jax import lax
from jax.experimental import pallas as pl
from jax.experimental.pallas import tpu as pltpu

F32 = jnp.float32
BF16 = jnp.bfloat16

RMS_EPS = 1e-6
SSM_HEAD_DIM = 64
SSM_GROUPS = 4
SSM_STATE = 128
CONV_K = 4
SSD_CHUNK = 128
HEAD_DIM = 64
KV_HEADS = 4
GQA = 4
ATT_HEADS = KV_HEADS * GQA
CMP_STRIDE = 16
L_CMP = 32
L_SEL = 64
N_SEL = 16
WINDOW = 512
SEL_FORCE = 1e9
KV_COLS = 2 * KV_HEADS * HEAD_DIM
K_COLS = KV_HEADS * HEAD_DIM
V_ROWS = HEAD_DIM + 16

LANES = 128
Q_TILE = 128
SEL_Q_TILE = 256
SEL_KEY_TILE = 512
PAGES_PER_STEP = 16
VMEM_LIMIT = 56 * 1024 * 1024
NEG_BIG = -1e30
LOG2_E = 1.4426950408889634


def _cparams(sem):
    return pltpu.CompilerParams(dimension_semantics=sem, vmem_limit_bytes=VMEM_LIMIT)


def _sigmoid(x):
    return 1.0 / (1.0 + jnp.exp(-x))


def _silu(x):
    return x * _sigmoid(x)


def _softplus(x):
    return jnp.maximum(x, 0.0) + jnp.log(1.0 + jnp.exp(-jnp.abs(x)))


def _split3(x):
    hi = x.astype(BF16)
    r = x - hi.astype(F32)
    mid = r.astype(BF16)
    lo = (r - mid.astype(F32)).astype(BF16)
    return hi, mid, lo


def _dot(a, b):
    return jnp.dot(a, b, preferred_element_type=F32)


def _dot_nt(a, b):
    return lax.dot_general(a, b, (((1,), (1,)), ((), ())), preferred_element_type=F32)


def _dot_tn(a, b):
    return lax.dot_general(a, b, (((0,), (0,)), ((), ())), preferred_element_type=F32)


def _dot3(x, m):
    hi, mid, lo = _split3(x)
    return _dot(hi, m) + _dot(mid, m) + _dot(lo, m)


def _head_rms(x, g2):
    lane = lax.broadcasted_iota(jnp.int32, (1, LANES), 1)
    lo = lane < HEAD_DIM
    outs = []
    for j in range(x.shape[1] // LANES):
        xt = x[:, j * LANES:(j + 1) * LANES]
        sq = xt * xt
        s_lo = jnp.sum(jnp.where(lo, sq, 0.0), axis=-1, keepdims=True)
        s_hi = jnp.sum(jnp.where(lo, 0.0, sq), axis=-1, keepdims=True)
        ms = jnp.where(lo, s_lo, s_hi) * (1.0 / HEAD_DIM)
        outs.append(xt * lax.rsqrt(ms + RMS_EPS) * g2)
    return outs[0] if len(outs) == 1 else jnp.concatenate(outs, axis=-1)


def _masked_softmax(s, mask):
    s = jnp.where(mask, s, -jnp.inf)
    m = jnp.max(s, axis=-1, keepdims=True)
    m = jnp.where(m == -jnp.inf, 0.0, m)
    e = jnp.where(mask, jnp.exp(s - m), 0.0)
    return e / jnp.maximum(jnp.sum(e, axis=-1, keepdims=True), 1e-30)


RANK_GROUP = 32


def _top_k_mask_t(score_t, cnt_ref, k, n_cand, live_groups=None):
    sub = 8
    n_rows, cols = score_t.shape
    row_in_blk = lax.broadcasted_iota(jnp.int32, (sub, cols), 0)
    n_blk = -(-n_cand // sub)
    n_grp = -(-n_cand // RANK_GROUP)
    blk_per_grp = RANK_GROUP // sub
    cnt_ref[...] = jnp.zeros(cnt_ref.shape, F32)

    def count_tile(rival_grp, cand_grp):
        for r in range(cand_grp * blk_per_grp, min((cand_grp + 1) * blk_per_grp, n_blk)):
            block = score_t[r * sub:(r + 1) * sub, :]
            cnt = cnt_ref[r * sub:(r + 1) * sub, :]
            for c in range(rival_grp * RANK_GROUP, min((rival_grp + 1) * RANK_GROUP, n_cand)):
                row = score_t[c:c + 1, :]
                gt = jnp.where(row > block, 1.0, 0.0)
                ge = jnp.where(row >= block, 1.0, 0.0)
                if r * sub + sub - 1 < c:
                    cnt = cnt + gt
                elif r * sub > c:
                    cnt = cnt + ge
                else:
                    cnt = cnt + jnp.where(row_in_blk + r * sub > c, ge, gt)
            cnt_ref[r * sub:(r + 1) * sub, :] = cnt

    for rival_grp in range(n_grp):
        for cand_grp in range(n_grp):
            if live_groups is None:
                count_tile(rival_grp, cand_grp)
            else:
                pl.when(max(rival_grp, cand_grp) < live_groups)(functools.partial(count_tile, rival_grp, cand_grp))
    row = lax.broadcasted_iota(jnp.int32, (n_rows, cols), 0)
    return jnp.where(row < n_blk * sub, jnp.where(cnt_ref[...] < k, 1.0, 0.0), 0.0)


def _norm_matmul_kernel(x_ref, g_ref, w_ref, o_ref, xn_ref):
    @pl.when(pl.program_id(1) == 0)
    def _():
        x = x_ref[...]
        ms = jnp.mean(x * x, axis=-1, keepdims=True)
        xn_ref[...] = (x * lax.rsqrt(ms + RMS_EPS) * g_ref[...]).astype(BF16)

    o_ref[...] = _dot(xn_ref[...], w_ref[...]).astype(o_ref.dtype)


def _norm_matmul(x, g, w, layer, tm, tn):
    rows, d = x.shape
    n = w.shape[2]
    return pl.pallas_call(
        _norm_matmul_kernel,
        grid=(rows // tm, n // tn),
        in_specs=[pl.BlockSpec((tm, d), lambda i, j: (i, 0)),
                  pl.BlockSpec((1, d), lambda i, j: (0, 0)),
                  pl.BlockSpec((None, d, tn), lambda i, j: (layer, 0, j))],
        out_specs=pl.BlockSpec((tm, tn), lambda i, j: (i, j)),
        out_shape=jax.ShapeDtypeStruct((rows, n), F32),
        scratch_shapes=[pltpu.VMEM((tm, d), BF16)],
        compiler_params=_cparams(("parallel", "arbitrary")),
        name="in_proj",
    )(x, g, w)


def _ffn_up_kernel(x_ref, g_ref, wg_ref, wv_ref, o_ref, xn_ref):
    @pl.when(pl.program_id(1) == 0)
    def _():
        x = x_ref[...]
        ms = jnp.mean(x * x, axis=-1, keepdims=True)
        xn_ref[...] = (x * lax.rsqrt(ms + RMS_EPS) * g_ref[...]).astype(BF16)

    xn = xn_ref[...]
    gate = _dot(xn, wg_ref[...])
    val = _dot(xn, wv_ref[...])
    o_ref[...] = (_silu(gate) * val).astype(o_ref.dtype)


def _ffn_up(x, g, w_gu, layer, tm, tn):
    rows, d = x.shape
    d_ff = w_gu.shape[2] // 2
    nj = d_ff // tn
    return pl.pallas_call(
        _ffn_up_kernel,
        grid=(rows // tm, nj),
        in_specs=[pl.BlockSpec((tm, d), lambda i, j: (i, 0)),
                  pl.BlockSpec((1, d), lambda i, j: (0, 0)),
                  pl.BlockSpec((None, d, tn), lambda i, j: (layer, 0, j)),
                  pl.BlockSpec((None, d, tn), lambda i, j: (layer, 0, j + nj))],
        out_specs=pl.BlockSpec((tm, tn), lambda i, j: (i, j)),
        out_shape=jax.ShapeDtypeStruct((rows, d_ff), BF16),
        scratch_shapes=[pltpu.VMEM((tm, d), BF16)],
        compiler_params=_cparams(("parallel", "arbitrary")),
        name="ffn_up",
    )(x, g, w_gu, w_gu)


def _ffn_down_kernel(a_ref, w_ref, h_ref, o_ref):
    o_ref[...] = h_ref[...] + _dot(a_ref[...], w_ref[...])


def _ffn_down(act, w_down, layer, h, tm, tn):
    rows, d_ff = act.shape
    d = w_down.shape[2]
    return pl.pallas_call(
        _ffn_down_kernel,
        grid=(rows // tm, d // tn),
        in_specs=[pl.BlockSpec((tm, d_ff), lambda i, j: (i, 0)),
                  pl.BlockSpec((None, d_ff, tn), lambda i, j: (layer, 0, j)),
                  pl.BlockSpec((tm, tn), lambda i, j: (i, j))],
        out_specs=pl.BlockSpec((tm, tn), lambda i, j: (i, j)),
        out_shape=jax.ShapeDtypeStruct((rows, d), F32),
        compiler_params=_cparams(("parallel", "arbitrary")),
        name="ffn_down",
    )(act, w_down, h)


def _out_proj_kernel(n_att, x_ref, y_ref, *refs):
    att_refs = refs[:n_att]
    w1_ref, w2_ref, o_ref = refs[n_att:]
    att = att_refs[0][...]
    for r in att_refs[1:]:
        att = att + r[...]
    o_ref[...] = (x_ref[...] + _dot(y_ref[...], w1_ref[...])
                  + _dot(att.astype(BF16), w2_ref[...]))


def _out_proj(x, y_ssd, atts, w_out, layer, tm, tn):
    rows, d = x.shape
    half = y_ssd.shape[1]
    n_att = len(atts)
    return pl.pallas_call(
        functools.partial(_out_proj_kernel, n_att),
        grid=(rows // tm, d // tn),
        in_specs=([pl.BlockSpec((tm, tn), lambda i, j: (i, j)),
                   pl.BlockSpec((tm, half), lambda i, j: (i, 0))]
                  + [pl.BlockSpec((tm, half), lambda i, j: (i, 0)) for _ in atts]
                  + [pl.BlockSpec((None, half, tn), lambda i, j: (layer, 0, j)),
                     pl.BlockSpec((None, half, tn), lambda i, j: (layer, 1, j))]),
        out_specs=pl.BlockSpec((tm, tn), lambda i, j: (i, j)),
        out_shape=jax.ShapeDtypeStruct((rows, d), F32),
        compiler_params=_cparams(("parallel", "arbitrary")),
        name="out_proj",
    )(x, y_ssd, *atts, w_out, w_out)


def _prep_kernel(transposed, q_ref, ks_ref, kw_ref, sm_ref, gq_ref, gks_ref, gkw_ref,
                 qn_ref, kvs_ref, kvw_ref, khs_ref, khw_ref, gates_ref, *t_refs):
    qn = _head_rms(q_ref[...], gq_ref[...]) * (HEAD_DIM ** -0.5)
    qn_ref[...] = qn.astype(BF16)
    if transposed:
        qt_ref, vts_ref, vtw_ref = t_refs
        q2 = qn * LOG2_E
        for h in range(KV_HEADS):
            qt_ref[h] = q2[:, h * GQA * HEAD_DIM:(h + 1) * GQA * HEAD_DIM].T.astype(BF16)
    for src, g_ref, full_ref, kh_ref, vt_ref in ((ks_ref, gks_ref, kvs_ref, khs_ref, vts_ref if transposed else None),
                                                 (kw_ref, gkw_ref, kvw_ref, khw_ref, vtw_ref if transposed else None)):
        kv = src[...]
        kn = _head_rms(kv[:, :K_COLS], g_ref[...])
        v = kv[:, K_COLS:]
        full_ref[:, :K_COLS] = kn
        full_ref[:, K_COLS:] = v
        for h in range(KV_HEADS):
            kh_ref[h] = kn[:, h * HEAD_DIM:(h + 1) * HEAD_DIM].astype(BF16)
        if transposed:
            for pair in range(KV_HEADS // 2):
                vt = v[:, pair * LANES:(pair + 1) * LANES].T.astype(BF16)
                ones = jnp.ones((V_ROWS - HEAD_DIM, vt.shape[1]), BF16)
                vt_ref[2 * pair] = jnp.concatenate([vt[:HEAD_DIM], ones], axis=0)
                vt_ref[2 * pair + 1] = jnp.concatenate([vt[HEAD_DIM:], ones], axis=0)
    sig = _sigmoid(sm_ref[...])
    n_dt = LANES // 8
    for h in range(KV_HEADS):
        gates_ref[h] = pltpu.roll(sig, LANES - n_dt - 3 * GQA * h, axis=1)


def _prep(u, col, g_q, g_ks, g_kw, tm, transposed):
    rows = u.shape[0]
    nsa = ATT_HEADS * HEAD_DIM
    row_spec = lambda w, c: pl.BlockSpec((tm, w), lambda i, c=c: (i, c))
    head_out = jax.ShapeDtypeStruct((KV_HEADS, rows, HEAD_DIM), BF16)
    head_spec = pl.BlockSpec((KV_HEADS, tm, HEAD_DIM), lambda i: (0, i, 0))
    vec = pl.BlockSpec((1, LANES), lambda i: (0, 0))
    out_specs = [pl.BlockSpec((tm, nsa), lambda i: (i, 0)),
                 pl.BlockSpec((tm, KV_COLS), lambda i: (i, 0)),
                 pl.BlockSpec((tm, KV_COLS), lambda i: (i, 0)),
                 head_spec, head_spec,
                 pl.BlockSpec((KV_HEADS, tm, LANES), lambda i: (0, i, 0))]
    out_shape = [jax.ShapeDtypeStruct((rows, nsa), BF16),
                 jax.ShapeDtypeStruct((rows, KV_COLS), F32),
                 jax.ShapeDtypeStruct((rows, KV_COLS), F32),
                 head_out, head_out,
                 jax.ShapeDtypeStruct((KV_HEADS, rows, LANES), F32)]
    if transposed:
        vt_spec = pl.BlockSpec((KV_HEADS, V_ROWS, tm), lambda i: (0, 0, i))
        vt_out = jax.ShapeDtypeStruct((KV_HEADS, V_ROWS, rows), BF16)
        out_specs += [pl.BlockSpec((KV_HEADS, GQA * HEAD_DIM, tm), lambda i: (0, 0, i)), vt_spec, vt_spec]
        out_shape += [jax.ShapeDtypeStruct((KV_HEADS, GQA * HEAD_DIM, rows), BF16), vt_out, vt_out]
    return pl.pallas_call(
        functools.partial(_prep_kernel, transposed),
        grid=(rows // tm,),
        in_specs=[row_spec(nsa, col["q"]), row_spec(KV_COLS, col["kvs"]),
                  row_spec(KV_COLS, col["kvw"]), row_spec(LANES, col["small"]), vec, vec, vec],
        out_specs=out_specs,
        out_shape=out_shape,
        compiler_params=_cparams(("parallel",)),
        name="head_prep",
    )(u, u, u, u, g_q, g_ks, g_kw)


def _gated_norm(y, xs, z, dsk, gn):
    y = (y + dsk * xs) * _silu(z)
    ms = jnp.mean(y * y, axis=-1, keepdims=True)
    return y * lax.rsqrt(ms + RMS_EPS) * gn


def _ssd_prompt_kernel(xbc_ref, z_ref, sm_ref, cw_ref, cb_ref, dtb_ref, alog_ref, dsk_ref, gn_ref,
                       y_ref, hout_ref, xb_ref, st_ref):
    c = pl.program_id(0)
    q = SSD_CHUNK
    ssm_dim = z_ref.shape[1]
    gw = SSM_STATE
    pad = 8

    @pl.when(c == 0)
    def _():
        xb_ref[0:pad, :] = jnp.zeros((pad, xb_ref.shape[1]), F32)
        st_ref[...] = jnp.zeros(st_ref.shape, F32)

    xb_ref[pad:pad + q, :] = xbc_ref[...]
    conv = cb_ref[...]
    for k in range(CONV_K):
        conv = conv + cw_ref[k:k + 1, :] * xb_ref[pl.ds(pad - (CONV_K - 1) + k, q), :]
    xb_ref[0:pad, :] = xb_ref[q:q + pad, :]
    act = _silu(conv)
    xs = act[:, :ssm_dim]
    bm = act[:, ssm_dim:ssm_dim + SSM_GROUPS * gw].astype(BF16)
    cm = act[:, ssm_dim + SSM_GROUPS * gw:].astype(BF16)

    lane = lax.broadcasted_iota(jnp.int32, (1, LANES), 1)
    n_heads = ssm_dim // SSM_HEAD_DIM
    dt = jnp.where(lane < n_heads, _softplus(sm_ref[...] + dtb_ref[...]), 0.0)
    a = -jnp.exp(alog_ref[...])
    ri = lax.broadcasted_iota(jnp.int32, (q, q), 0)
    ci = lax.broadcasted_iota(jnp.int32, (q, q), 1)
    causal = ri >= ci
    tri = jnp.where(causal, 1.0, 0.0).astype(BF16)
    da_hi, da_mid, da_lo = _split3(dt * a)
    acum = _dot(tri, da_hi) + _dot(tri, da_mid) + _dot(tri, da_lo)
    acum_t = acum.T
    dt_t = dt.T
    e_acum = jnp.exp(acum)
    a_last = acum[q - 1:q, :]
    w_end = dt * jnp.exp(a_last - acum)
    e_last = jnp.exp(a_last)
    lo = lane < SSM_HEAD_DIM
    row_lo = lax.broadcasted_iota(jnp.int32, (LANES, 1), 0) < SSM_HEAD_DIM

    ys = []
    heads_per_group = n_heads // SSM_GROUPS
    for pr in range(n_heads // 2):
        h0, h1 = 2 * pr, 2 * pr + 1
        g = h0 // heads_per_group
        cg = cm[:, g * gw:(g + 1) * gw]
        bg = bm[:, g * gw:(g + 1) * gw]
        cb = _dot_nt(cg, bg)
        x2 = xs[:, pr * LANES:(pr + 1) * LANES]
        x2b = x2.astype(BF16)
        yd = []
        for h in (h0, h1):
            seg = acum[:, h:h + 1] - acum_t[h:h + 1, :]
            dec = jnp.where(causal, jnp.exp(jnp.where(causal, seg, 0.0)), 0.0)
            m = (cb * dec * dt_t[h:h + 1, :]).astype(BF16)
            yd.append(_dot(m, x2b))
        y_diag = jnp.where(lo, yd[0], yd[1])
        sp = st_ref[pr * LANES:(pr + 1) * LANES, :]
        ea = jnp.where(lo, e_acum[:, h0:h0 + 1], e_acum[:, h1:h1 + 1])
        y_off = _dot_nt(cg, sp.astype(BF16)) * ea
        w2 = jnp.where(lo, w_end[:, h0:h0 + 1], w_end[:, h1:h1 + 1])
        xw_t = (x2 * w2).T.astype(BF16)
        cd = jnp.where(row_lo, e_last[:, h0:h0 + 1], e_last[:, h1:h1 + 1])
        st_ref[pr * LANES:(pr + 1) * LANES, :] = sp * cd + _dot(xw_t, bg)
        ys.append(y_diag + y_off)
    y = jnp.concatenate(ys, axis=-1)
    y_ref[...] = _gated_norm(y, xs, z_ref[...], dsk_ref[...], gn_ref[...]).astype(y_ref.dtype)

    @pl.when(c == pl.num_programs(0) - 1)
    def _():
        hout_ref[...] = st_ref[...]


def _ssd_prompt(u, col, lw):
    t = u.shape[0]
    ssm_dim = lw["dsk"].shape[1]
    conv_ch = lw["conv_w"].shape[1]
    q = SSD_CHUNK
    full = lambda r, w: pl.BlockSpec((r, w), lambda c: (0, 0))
    return pl.pallas_call(
        _ssd_prompt_kernel,
        grid=(t // q,),
        in_specs=[pl.BlockSpec((q, conv_ch), lambda c: (c, col["xbc"])),
                  pl.BlockSpec((q, ssm_dim), lambda c: (c, col["z"])),
                  pl.BlockSpec((q, LANES), lambda c: (c, col["small"])),
                  full(CONV_K, conv_ch), full(1, conv_ch), full(1, LANES), full(1, LANES),
                  full(1, ssm_dim), full(1, ssm_dim)],
        out_specs=[pl.BlockSpec((q, ssm_dim), lambda c: (c, 0)),
                   pl.BlockSpec((ssm_dim, SSM_STATE), lambda c: (0, 0))],
        out_shape=[jax.ShapeDtypeStruct((t, ssm_dim), BF16),
                   jax.ShapeDtypeStruct((ssm_dim, SSM_STATE), F32)],
        scratch_shapes=[pltpu.VMEM((q + 8, conv_ch), F32),
                        pltpu.VMEM((ssm_dim, SSM_STATE), F32)],
        compiler_params=_cparams(("arbitrary",)),
        name="ssd_prompt",
    )(u, u, u, lw["conv_w"], lw["conv_b"], lw["dt_bias"], lw["a_log"], lw["dsk"], lw["ssm_norm_g"])


def _ssd_sample_kernel(xbc_ref, z_ref, sm_ref, cbuf_ref, h_ref, cw_ref, cb_ref, dtb_ref, alog_ref,
                       dsk_ref, gn_ref, exp_ref, y_ref, hout_ref):
    b = pl.program_id(0)
    ssm_dim = z_ref.shape[1]
    gw = SSM_STATE
    n_heads = ssm_dim // SSM_HEAD_DIM
    conv = cb_ref[...] + cw_ref[CONV_K - 1:CONV_K, :] * xbc_ref[pl.ds(b, 1), :]
    for k in range(CONV_K - 1):
        conv = conv + cw_ref[k:k + 1, :] * cbuf_ref[k, pl.ds(b, 1), :]
    act = _silu(conv)
    xs = act[:, :ssm_dim]
    lane = lax.broadcasted_iota(jnp.int32, (1, LANES), 1)
    dt_raw = jnp.broadcast_to(jnp.where(lane < n_heads, sm_ref[pl.ds(b, 1), :], 0.0), (8, LANES))
    dt = _softplus(_dot3(dt_raw, exp_ref[...])[0:1, :] + dtb_ref[...])
    dec = jnp.exp(dt * (-jnp.exp(alog_ref[...])))
    dtx = dt * xs
    row8 = lax.broadcasted_iota(jnp.int32, (8, 1), 0)

    def rows8(pieces):
        out = jnp.zeros((8, pieces[0].shape[1]), F32)
        for r, p in enumerate(pieces):
            out = jnp.where(row8 == r, p.astype(F32), out)
        return out.astype(BF16)

    dec8 = rows8(_split3(dec))
    dtx8 = rows8(_split3(dtx))
    ones8 = jnp.where(row8 < 3, 1.0, 0.0).astype(BF16) * jnp.ones((1, gw), BF16)
    gh = (n_heads // SSM_GROUPS) * SSM_HEAD_DIM
    ys = []
    for g in range(SSM_GROUPS):
        bg = act[:, ssm_dim + g * gw:ssm_dim + (g + 1) * gw].astype(BF16)
        cg = act[:, ssm_dim + (SSM_GROUPS + g) * gw:ssm_dim + (SSM_GROUPS + g + 1) * gw].astype(BF16)
        b8 = jnp.where(row8 < 3, 1.0, 0.0).astype(BF16) * bg
        c8 = jnp.where(row8 < 1, 1.0, 0.0).astype(BF16) * cg
        dec_col = _dot_tn(dec8[:, g * gh:(g + 1) * gh], ones8)
        dbx = _dot_tn(dtx8[:, g * gh:(g + 1) * gh], b8)
        s_new = h_ref[g * gh:(g + 1) * gh, :] * dec_col + dbx
        hout_ref[g * gh:(g + 1) * gh, :] = s_new
        ys.append(_dot_nt(c8, s_new.astype(BF16))[0:1, :])
    y = jnp.concatenate(ys, axis=-1)
    y_ref[pl.ds(b, 1), :] = _gated_norm(y, xs, z_ref[pl.ds(b, 1), :], dsk_ref[...], gn_ref[...])


def _ssd_sample(u, col, conv_t, h_all, h_base, lw):
    bs = u.shape[0]
    ssm_dim = lw["dsk"].shape[1]
    conv_ch = lw["conv_w"].shape[1]
    full = lambda r, w: pl.BlockSpec((r, w), lambda b: (0, 0))
    return pl.pallas_call(
        _ssd_sample_kernel,
        grid=(bs,),
        in_specs=[pl.BlockSpec((bs, conv_ch), lambda b: (0, col["xbc"])),
                  pl.BlockSpec((bs, ssm_dim), lambda b: (0, col["z"])),
                  pl.BlockSpec((bs, LANES), lambda b: (0, col["small"])),
                  pl.BlockSpec((CONV_K - 1, bs, conv_ch), lambda b: (0, 0, 0)),
                  pl.BlockSpec((None, ssm_dim, SSM_STATE), lambda b: (h_base + b, 0, 0)),
                  full(CONV_K, conv_ch), full(1, conv_ch), full(1, ssm_dim), full(1, ssm_dim),
                  full(1, ssm_dim), full(1, ssm_dim), full(LANES, ssm_dim)],
        out_specs=[pl.BlockSpec((bs, ssm_dim), lambda b: (0, 0)),
                   pl.BlockSpec((None, ssm_dim, SSM_STATE), lambda b: (b, 0, 0))],
        out_shape=[jax.ShapeDtypeStruct((bs, ssm_dim), F32),
                   jax.ShapeDtypeStruct((bs, ssm_dim, SSM_STATE), F32)],
        compiler_params=_cparams(("arbitrary",)),
        name="ssd_sample",
    )(u, u, u, conv_t, h_all, lw["conv_w"], lw["conv_b"], lw["dt_bias_e"], lw["a_log_e"], lw["dsk"],
      lw["ssm_norm_g"], lw["head_expand"])


CMP_EXTRA_CHUNKS = 8


def _compress_rows(buf_ref, xcat_ref, res_ref, pe_ref, w_ref, gk_ref):
    n_blocks = res_ref.shape[0] - CMP_EXTRA_CHUNKS
    rows = res_ref.shape[0]
    pe = pe_ref[...]
    pe_hi = pe.astype(BF16).astype(F32)
    pe_lo = pe - pe_hi
    half = L_CMP // 2
    acc = []
    for p in range(KV_COLS // LANES):
        tile = slice(p * LANES, (p + 1) * LANES)
        base = (n_blocks + 1) * CMP_STRIDE
        extra = [None, pe_hi[:half, tile], pe_lo[:half, tile], pe_hi[half:, tile], pe_lo[half:, tile], None, None]
        for c, piece in enumerate(extra):
            dst = buf_ref.at[p, base + c * CMP_STRIDE:base + (c + 1) * CMP_STRIDE, :]
            dst[...] = jnp.zeros((CMP_STRIDE, LANES), F32) if piece is None else piece
        for phase in range(CMP_STRIDE):
            xcat_ref[:, phase * LANES:(phase + 1) * LANES] = (
                buf_ref[p, pl.ds(phase, rows, stride=CMP_STRIDE), :].astype(BF16))
        res_ref[...] = _dot(xcat_ref[...], w_ref[p // 2])
        pe_term = (res_ref[n_blocks + 2:n_blocks + 3, :LANES] + res_ref[n_blocks + 3:n_blocks + 4, :LANES]
                   + res_ref[n_blocks + 4:n_blocks + 5, LANES:] + res_ref[n_blocks + 5:n_blocks + 6, LANES:])
        acc.append(res_ref[0:n_blocks, :LANES] + res_ref[1:n_blocks + 1, LANES:] + pe_term)
    ck = _head_rms(jnp.concatenate(acc[:2], axis=-1), gk_ref[...])
    cv = jnp.concatenate(acc[2:], axis=-1)
    return ck, cv


def _store_heads(ck, cv, ck_ref, cv_ref):
    for h in range(KV_HEADS):
        ck_ref[h] = ck[:, h * HEAD_DIM:(h + 1) * HEAD_DIM].astype(BF16)
        cv_ref[h] = cv[:, h * HEAD_DIM:(h + 1) * HEAD_DIM].astype(BF16)


def _compress_prompt_kernel(main_ref, nxt_ref, pe_ref, w_ref, gk_ref, ck_ref, cv_ref, buf_ref, xcat_ref, res_ref):
    rows = main_ref.shape[0]
    for p in range(KV_COLS // LANES):
        buf_ref[p, 0:rows, :] = main_ref[:, p * LANES:(p + 1) * LANES]
        buf_ref[p, rows:rows + CMP_STRIDE, :] = nxt_ref[:, p * LANES:(p + 1) * LANES]
    ck, cv = _compress_rows(buf_ref, xcat_ref, res_ref, pe_ref, w_ref, gk_ref)
    _store_heads(ck, cv, ck_ref, cv_ref)


def _compress_scratch(rows):
    chunks = rows // CMP_STRIDE + CMP_EXTRA_CHUNKS
    return [pltpu.VMEM((KV_COLS // LANES, chunks * CMP_STRIDE, LANES), F32),
            pltpu.VMEM((chunks, CMP_STRIDE * LANES), BF16),
            pltpu.VMEM((chunks, 2 * LANES), F32)]


def _compress_prompt(kvc_pad, lw, rows_per_step):
    t = kvc_pad.shape[0] - CMP_STRIDE
    nb = rows_per_step // CMP_STRIDE
    out = jax.ShapeDtypeStruct((KV_HEADS, t // CMP_STRIDE, HEAD_DIM), BF16)
    out_spec = pl.BlockSpec((KV_HEADS, nb, HEAD_DIM), lambda i: (0, i, 0))
    return pl.pallas_call(
        _compress_prompt_kernel,
        grid=(t // rows_per_step,),
        in_specs=[pl.BlockSpec((rows_per_step, KV_COLS), lambda i: (i, 0)),
                  pl.BlockSpec((CMP_STRIDE, KV_COLS), lambda i: ((i + 1) * nb, 0)),
                  pl.BlockSpec((L_CMP, KV_COLS), lambda i: (0, 0)),
                  pl.BlockSpec((2, CMP_STRIDE * LANES, 2 * LANES), lambda i: (0, 0, 0)),
                  pl.BlockSpec((1, LANES), lambda i: (0, 0))],
        out_specs=[out_spec, out_spec],
        out_shape=[out, out],
        scratch_shapes=_compress_scratch(rows_per_step),
        compiler_params=_cparams(("parallel",)),
        name="compress_prompt",
    )(kvc_pad, kvc_pad, lw["cmp_pe_rows"], lw["cmp_w_cat"], lw["g_kc"])


def _compress_sample_kernel(n_pg, pt_ref, *refs):
    page_refs = refs[:n_pg]
    nxt_ref, pe_ref, w_ref, gk_ref, ck_ref, cv_ref, buf_ref, xcat_ref, res_ref = refs[n_pg:]
    pg = page_refs[0].shape[1]
    for p in range(KV_COLS // LANES):
        for k in range(n_pg):
            buf_ref[p, k * pg:(k + 1) * pg, :] = page_refs[k][p * LANES:(p + 1) * LANES, :].T
        buf_ref[p, n_pg * pg:n_pg * pg + CMP_STRIDE, :] = nxt_ref[p * LANES:(p + 1) * LANES, :].T[:CMP_STRIDE]
    ck, cv = _compress_rows(buf_ref, xcat_ref, res_ref, pe_ref, w_ref, gk_ref)
    _store_heads(ck, cv, ck_ref, cv_ref)


def _compress_sample(cache_t, page_base, page_table, lw):
    bs, n_pages = page_table.shape
    pg = cache_t.shape[2]
    n_pg = _pick(n_pages, (2 * PAGES_PER_STEP, PAGES_PER_STEP))
    nb = n_pg * pg // CMP_STRIDE
    past = n_pages * pg

    def page_map(b, j, pt, k):
        return (page_base + pt[b, j * n_pg + k], 0, 0)

    def next_map(b, j, pt):
        return (page_base + pt[b, jnp.minimum((j + 1) * n_pg, n_pages - 1)], 0, 0)

    out = jax.ShapeDtypeStruct((bs, KV_HEADS, past // CMP_STRIDE, HEAD_DIM), BF16)
    out_spec = pl.BlockSpec((None, KV_HEADS, nb, HEAD_DIM), lambda b, j, pt: (b, 0, j, 0))
    grid_spec = pltpu.PrefetchScalarGridSpec(
        num_scalar_prefetch=1,
        grid=(bs, n_pages // n_pg),
        in_specs=([pl.BlockSpec((None, KV_COLS, pg), functools.partial(page_map, k=k)) for k in range(n_pg)]
                  + [pl.BlockSpec((None, KV_COLS, pg), next_map),
                     pl.BlockSpec((L_CMP, KV_COLS), lambda b, j, pt: (0, 0)),
                     pl.BlockSpec((2, CMP_STRIDE * LANES, 2 * LANES), lambda b, j, pt: (0, 0, 0)),
                     pl.BlockSpec((1, LANES), lambda b, j, pt: (0, 0))]),
        out_specs=[out_spec, out_spec],
        scratch_shapes=_compress_scratch(n_pg * pg),
    )
    return pl.pallas_call(
        functools.partial(_compress_sample_kernel, n_pg),
        grid_spec=grid_spec,
        out_shape=[out, out],
        compiler_params=_cparams(("parallel", "parallel")),
        name="compress_sample",
    )(page_table, *([cache_t] * (n_pg + 1)), lw["cmp_pe_rows"], lw["cmp_w_cat"], lw["g_kc"])


def _cover_matrix(nc, n_sel_blocks, width):
    n = lax.broadcasted_iota(jnp.int32, (nc, width), 0) * CMP_STRIDE
    s = lax.broadcasted_iota(jnp.int32, (nc, width), 1)
    hit = (n < s * L_SEL + L_SEL) & (n + L_CMP > s * L_SEL) & (s < n_sel_blocks)
    return jnp.where(hit, 1.0, 0.0).astype(BF16)


def _block_scores(imp, q_pos, n_sel_blocks):
    blk = lax.broadcasted_iota(jnp.int32, imp.shape, 1)
    valid = blk * L_SEL <= q_pos
    cur = q_pos // L_SEL
    forced = (blk == 0) | (blk == cur) | (blk == cur - 1)
    score = jnp.where(valid & forced, SEL_FORCE, jnp.where(valid, imp, -1.0))
    return jnp.where(blk < n_sel_blocks, score, -2.0)


def _cmp_prompt_kernel(q_ref, ck_ref, cv_ref, gates_ref, oc_ref, sel_ref, cnt_ref):
    i = pl.program_id(0)
    tq = q_ref.shape[0]
    nc = ck_ref.shape[1]
    n_sel_blocks = nc * CMP_STRIDE // L_SEL
    q_pos = i * tq + lax.broadcasted_iota(jnp.int32, (tq, 1), 0)
    n_end = lax.broadcasted_iota(jnp.int32, (1, nc), 1) * CMP_STRIDE + (L_CMP - 1)
    mask = (n_end <= q_pos) & (n_end < nc * CMP_STRIDE)
    cover = _cover_matrix(nc, n_sel_blocks, LANES)
    q = q_ref[...]
    outs = []
    pieces = []
    for h in range(KV_HEADS):
        q4 = jnp.concatenate([q[:, (h * GQA + g) * HEAD_DIM:(h * GQA + g + 1) * HEAD_DIM] for g in range(GQA)],
                             axis=0)
        s4 = _dot_nt(q4, ck_ref[h])
        ps = [_masked_softmax(s4[g * tq:(g + 1) * tq], mask) for g in range(GQA)]
        o4 = _dot(jnp.concatenate([p.astype(BF16) for p in ps], axis=0), cv_ref[h])
        gates = gates_ref[h]
        outs += [o4[g * tq:(g + 1) * tq] * gates[:, 3 * g:3 * g + 1] for g in range(GQA)]
        pieces += list(_split3(ps[0] + ps[1] + ps[2] + ps[3]))
    oc_ref[...] = jnp.concatenate(outs, axis=-1)
    imp_all = _dot(jnp.concatenate(pieces, axis=0), cover)
    for h in range(KV_HEADS):
        imp = sum(imp_all[(3 * h + r) * tq:(3 * h + r + 1) * tq] for r in range(3))
        score_t = _block_scores(imp, q_pos, n_sel_blocks).T
        live_groups = (i * tq + tq - 1) // (L_SEL * RANK_GROUP) + 1
        sel_ref[h] = _top_k_mask_t(score_t, cnt_ref, N_SEL, n_sel_blocks, live_groups)


def _cmp_prompt(qn, ck, cv, gates_h, t):
    nsa = qn.shape[1]
    nc = ck.shape[1]
    tq = Q_TILE
    kv_spec = pl.BlockSpec((KV_HEADS, nc, HEAD_DIM), lambda i: (0, 0, 0))
    return pl.pallas_call(
        _cmp_prompt_kernel,
        grid=(t // tq,),
        in_specs=[pl.BlockSpec((tq, nsa), lambda i: (i, 0)), kv_spec, kv_spec,
                  pl.BlockSpec((KV_HEADS, tq, LANES), lambda i: (0, i, 0))],
        out_specs=[pl.BlockSpec((tq, nsa), lambda i: (i, 0)),
                   pl.BlockSpec((KV_HEADS, LANES, tq), lambda i: (0, 0, i))],
        out_shape=[jax.ShapeDtypeStruct((t, nsa), F32),
                   jax.ShapeDtypeStruct((KV_HEADS, LANES, t), F32)],
        scratch_shapes=[pltpu.VMEM((LANES, tq), F32)],
        compiler_params=_cparams(("parallel",)),
        name="cmp_topk_prompt",
    )(qn, ck, cv, gates_h)


def _tile_softmax_t(s_t, bias_t, tq, maxes, p_ref):
    new_maxes, alphas = [], []
    for g in range(GQA):
        sb = s_t[:, g * tq:(g + 1) * tq] + bias_t
        m_new = jnp.maximum(maxes[g], jnp.max(sb, axis=0, keepdims=True))
        alphas.append(jnp.exp2(maxes[g] - m_new))
        p_ref[:, g * tq:(g + 1) * tq] = jnp.exp2(sb - m_new).astype(BF16)
        new_maxes.append(m_new)
    return tuple(new_maxes), jnp.concatenate(alphas, axis=1)


def _stack_q_t(qt_ref):
    return jnp.concatenate([qt_ref[g * HEAD_DIM:(g + 1) * HEAD_DIM, :] for g in range(GQA)], axis=1)


def _finish_t(acc_t, gates, branch, tq):
    o_t = acc_t[:HEAD_DIM] * (1.0 / acc_t[HEAD_DIM:HEAD_DIM + 1])
    lane = lax.broadcasted_iota(jnp.int32, (1, LANES), 1)
    outs = []
    for pair in range(GQA // 2):
        g0, g1 = 2 * pair, 2 * pair + 1
        two = jnp.concatenate([o_t[:, g0 * tq:(g0 + 1) * tq], o_t[:, g1 * tq:(g1 + 1) * tq]], axis=0)
        gate = jnp.where(lane < HEAD_DIM, gates[:, 3 * g0 + branch:3 * g0 + branch + 1],
                         gates[:, 3 * g1 + branch:3 * g1 + branch + 1])
        outs.append(two.T * gate)
    return jnp.concatenate(outs, axis=1)


def _init_maxes(tq):
    return tuple(jnp.full((1, tq), NEG_BIG, F32) for _ in range(GQA))


def _sel_prompt_kernel(qt_ref, k_ref, vt_ref, sel_ref, gates_ref, o_ref, p0_ref, p1_ref, acc0_ref, acc1_ref):
    p_refs, acc_refs = (p0_ref, p1_ref), (acc0_ref, acc1_ref)
    i = pl.program_id(1)
    tq = sel_ref.shape[1]
    tk = SEL_KEY_TILE
    bpt = tk // L_SEL
    q4t = _stack_q_t(qt_ref)
    q_pos = i * tq + lax.broadcasted_iota(jnp.int32, (1, tq), 1)
    key_off = lax.broadcasted_iota(jnp.int32, (L_SEL, 1), 0)
    n_tiles = (i * tq + tq - 1) // tk + 1
    last_tile = k_ref.shape[0] // tk - 1
    for acc_ref in acc_refs:
        acc_ref[...] = jnp.zeros(acc_ref.shape, F32)

    def tile(j, maxes, p_ref, acc_ref):
        live = (j < n_tiles).astype(jnp.int32)
        jc = jnp.minimum(j, last_tile)
        at = pl.ds(pl.multiple_of(jc * tk, tk), tk)
        q_lim = q_pos * live - (1 - live)
        rows = []
        for blk in range(bpt):
            picked = jnp.where(sel_ref[pl.ds(jc * bpt + blk, 1), :] > 0.5, 0.0, NEG_BIG)
            visible = key_off + (jc * tk + blk * L_SEL) <= q_lim
            rows.append(jnp.where(visible, picked, NEG_BIG))
        bias_t = jnp.concatenate(rows, axis=0)
        maxes, alpha = _tile_softmax_t(_dot(k_ref[at, :], q4t), bias_t, tq, maxes, p_ref)
        acc_ref[...] = alpha * acc_ref[...] + _dot(vt_ref[:, at], p_ref[...])
        return maxes

    def body(jj, carry):
        return tuple(tile(2 * jj + c, carry[c], p_refs[c], acc_refs[c]) for c in range(2))

    chains = lax.fori_loop(0, (n_tiles + 1) // 2, body, (_init_maxes(tq), _init_maxes(tq)))
    scale = [[], []]
    for m0, m1 in zip(*chains):
        m = jnp.maximum(m0, m1)
        scale[0].append(jnp.exp2(m0 - m))
        scale[1].append(jnp.exp2(m1 - m))
    acc = sum(jnp.concatenate(scale[c], axis=1) * acc_refs[c][...] for c in range(2))
    o_ref[...] = _finish_t(acc, gates_ref[...], 1, tq)


def _attn_specs(t, tq):
    return [pl.BlockSpec((None, GQA * HEAD_DIM, tq), lambda h, i: (h, 0, i)),
            pl.BlockSpec((None, t, HEAD_DIM), lambda h, i: (h, 0, 0)),
            pl.BlockSpec((None, V_ROWS, t), lambda h, i: (h, 0, 0)),
            pl.BlockSpec((None, tq, LANES), lambda h, i: (h, i, 0))]


def _sel_prompt(q_t, k_h, v_t, sel_t, gates_h, t):
    tq = _pick(t, (SEL_Q_TILE, Q_TILE))
    q_spec, k_spec, v_spec, g_spec = _attn_specs(t, tq)
    return pl.pallas_call(
        _sel_prompt_kernel,
        grid=(KV_HEADS, t // tq),
        in_specs=[q_spec, k_spec, v_spec, pl.BlockSpec((None, LANES, tq), lambda h, i: (h, 0, i)), g_spec],
        out_specs=pl.BlockSpec((tq, GQA * HEAD_DIM), lambda h, i: (i, h)),
        out_shape=jax.ShapeDtypeStruct((t, ATT_HEADS * HEAD_DIM), F32),
        scratch_shapes=([pltpu.VMEM((SEL_KEY_TILE, GQA * tq), BF16)] * 2
                        + [pltpu.VMEM((V_ROWS, GQA * tq), F32)] * 2),
        compiler_params=_cparams(("parallel", "arbitrary")),
        name="sel_prompt",
    )(q_t, k_h, v_t, sel_t, gates_h)


def _win_prompt_kernel(qt_ref, k_ref, vt_ref, gates_ref, o_ref, p_ref):
    i = pl.program_id(1)
    tq = gates_ref.shape[0]
    span = WINDOW + tq
    at = pl.ds(pl.multiple_of(jnp.maximum(i * tq - WINDOW, 0), tq), span)
    q_pos = i * tq + lax.broadcasted_iota(jnp.int32, (1, tq), 1)
    k_pos = jnp.maximum(i * tq - WINDOW, 0) + lax.broadcasted_iota(jnp.int32, (span, 1), 0)
    bias_t = jnp.where((k_pos <= q_pos) & (k_pos >= q_pos - WINDOW), 0.0, NEG_BIG)
    _tile_softmax_t(_dot(k_ref[at, :], _stack_q_t(qt_ref)), bias_t, tq, _init_maxes(tq), p_ref)
    o_ref[...] = _finish_t(_dot(vt_ref[:, at], p_ref[...]), gates_ref[...], 2, tq)


def _win_prompt(q_t, k_h, v_t, gates_h, t):
    tq = _pick(t, (SEL_Q_TILE, Q_TILE))
    return pl.pallas_call(
        _win_prompt_kernel,
        grid=(KV_HEADS, t // tq),
        in_specs=_attn_specs(t, tq),
        out_specs=pl.BlockSpec((tq, GQA * HEAD_DIM), lambda h, i: (i, h)),
        out_shape=jax.ShapeDtypeStruct((t, ATT_HEADS * HEAD_DIM), F32),
        scratch_shapes=[pltpu.VMEM((WINDOW + tq, GQA * tq), BF16)],
        compiler_params=_cparams(("parallel", "arbitrary")),
        name="win_prompt",
    )(q_t, k_h, v_t, gates_h)


def _own_head_rows(parts, width):
    row_head = lax.broadcasted_iota(jnp.int32, (ATT_HEADS, 1), 0) // GQA
    out = jnp.zeros((ATT_HEADS, width), F32)
    for h in range(KV_HEADS):
        out = jnp.where(row_head == h, parts[h], out)
    return out


def _cmp_sample_kernel(past, q_ref, ck_ref, cv_ref, g3_ref, oc_ref, sel_ref, score_ref, cnt_ref):
    b = pl.program_id(0)
    nc = ck_ref.shape[1]
    width = sel_ref.shape[1]
    n_sel_blocks = past // L_SEL + 1
    q = q_ref[...]
    n_end = lax.broadcasted_iota(jnp.int32, (1, nc), 1) * CMP_STRIDE + (L_CMP - 1)
    mask = n_end <= past
    s = _own_head_rows([_dot_nt(q, ck_ref[h]) for h in range(KV_HEADS)], nc)
    p = _masked_softmax(s, mask)
    pb = p.astype(BF16)
    o = _own_head_rows([_dot(pb, cv_ref[h]) for h in range(KV_HEADS)], HEAD_DIM)
    oc_ref[...] = o * g3_ref[:, 0:1]
    row_head = lax.broadcasted_iota(jnp.int32, (ATT_HEADS, 1), 0) // GQA
    row8 = lax.broadcasted_iota(jnp.int32, (8, 1), 0)
    p_sum = jnp.zeros((8, nc), F32)
    for h in range(KV_HEADS):
        ph = jnp.sum(jnp.where(row_head == h, p, 0.0), axis=0, keepdims=True)
        p_sum = jnp.where(row8 == h, ph, p_sum)
    imp = _dot3(p_sum, _cover_matrix(nc, n_sel_blocks, width))
    q_pos = jnp.full((8, 1), past, jnp.int32)
    score_ref[pl.ds(pl.multiple_of(b * 8, 8), 8), :] = _block_scores(imp, q_pos, n_sel_blocks)

    @pl.when(b == pl.num_programs(0) - 1)
    def _():
        sel_ref[...] = _top_k_mask_t(score_ref[...].T, cnt_ref, N_SEL, n_sel_blocks).T


def _cmp_sample(q3, ck_s, cv_s, g3, past):
    bs = q3.shape[0]
    nc = ck_s.shape[2]
    width = -(-(past // L_SEL + 1) // LANES) * LANES
    kv_spec = pl.BlockSpec((None, KV_HEADS, nc, HEAD_DIM), lambda b: (b, 0, 0, 0))
    oc, sel = pl.pallas_call(
        functools.partial(_cmp_sample_kernel, past),
        grid=(bs,),
        in_specs=[pl.BlockSpec((None, ATT_HEADS, HEAD_DIM), lambda b: (b, 0, 0)), kv_spec, kv_spec,
                  pl.BlockSpec((None, ATT_HEADS, LANES), lambda b: (b, 0, 0))],
        out_specs=[pl.BlockSpec((None, ATT_HEADS, HEAD_DIM), lambda b: (b, 0, 0)),
                   pl.BlockSpec((bs * 8, width), lambda b: (0, 0))],
        out_shape=[jax.ShapeDtypeStruct((bs, ATT_HEADS, HEAD_DIM), F32),
                   jax.ShapeDtypeStruct((bs * 8, width), F32)],
        scratch_shapes=[pltpu.VMEM((bs * 8, width), F32), pltpu.VMEM((width, bs * 8), F32)],
        compiler_params=_cparams(("arbitrary",)),
        name="cmp_topk_sample",
    )(q3, ck_s, cv_s, g3)
    return oc, sel.reshape(bs, 8, width)


def _diag_heads(x):
    return _own_head_rows([x[:, h * HEAD_DIM:(h + 1) * HEAD_DIM] for h in range(KV_HEADS)], HEAD_DIM)


def _sel_win_sample_kernel(past, n_pg, pt_ref, *refs):
    page_refs = refs[:n_pg]
    (qbd_ref, selg_ref, selnew_ref, knew_ref, win_ref, wnew_ref, oc_ref, g3_ref,
     o_ref, m_ref, l_ref, acc_ref, kv_ref) = refs[n_pg:]
    j = pl.program_id(1)
    pg = page_refs[0].shape[1]
    qbd = qbd_ref[...]
    bpp = pg // L_SEL

    @pl.when(j == 0)
    def _():
        m_ref[...] = jnp.full(m_ref.shape, NEG_BIG, F32)
        l_ref[...] = jnp.zeros(l_ref.shape, F32)
        acc_ref[...] = jnp.zeros(acc_ref.shape, F32)

    sel = selg_ref[...].astype(BF16)
    n_blk = n_pg * bpp
    expand = jnp.where(lax.broadcasted_iota(jnp.int32, (n_blk, n_pg * pg), 0)
                       == lax.broadcasted_iota(jnp.int32, (n_blk, n_pg * pg), 1) // L_SEL, 1.0, 0.0).astype(BF16)
    picked = _dot(sel, expand)
    for k in range(n_pg):
        kv_ref[:, k * pg:(k + 1) * pg] = page_refs[k][...].astype(BF16)
    m, l, acc = m_ref[...], l_ref[...], acc_ref[...]
    s = _dot(qbd, kv_ref[0:K_COLS, :]) + jnp.where(picked > 0.5, 0.0, NEG_BIG)
    m_new = jnp.maximum(m, jnp.max(s, axis=-1, keepdims=True))
    alpha = jnp.exp(m - m_new)
    p = jnp.exp(s - m_new)
    l = alpha * l + jnp.sum(p, axis=-1, keepdims=True)
    acc = alpha * acc + _dot_nt(p.astype(BF16), kv_ref[K_COLS:, :])
    m = m_new
    m_ref[...], l_ref[...], acc_ref[...] = m, l, acc

    @pl.when(j == pl.num_programs(1) - 1)
    def _():
        qf = qbd.astype(F32)
        kn = knew_ref[...]
        s_new = jnp.sum(qf * kn[:, :K_COLS].astype(BF16).astype(F32), axis=-1, keepdims=True)
        ok_new = selnew_ref[:, 0:1] > 0.5
        s_new = jnp.where(ok_new, s_new, NEG_BIG)
        m2 = jnp.maximum(m, s_new)
        a2 = jnp.exp(m - m2)
        p_new = jnp.where(ok_new, jnp.exp(s_new - m2), 0.0)
        l2 = a2 * l + p_new
        acc2 = a2 * acc + p_new.astype(BF16).astype(F32) * kn[:, K_COLS:].astype(BF16).astype(F32)
        o_s = _diag_heads(acc2 / jnp.maximum(l2, 1e-30))
        w = win_ref[...].astype(BF16)
        w_buf = w.shape[1]
        wn = wnew_ref[...]
        w_pos = past - w_buf + lax.broadcasted_iota(jnp.int32, (1, w_buf), 1)
        ok_w = (w_pos <= past) & (w_pos >= past - WINDOW) & (w_pos >= 0)
        s_w = jnp.where(ok_w, _dot(qbd, w[:K_COLS, :]), -jnp.inf)
        s_wn = jnp.sum(qf * wn[:, :K_COLS].astype(BF16).astype(F32), axis=-1, keepdims=True)
        m_w = jnp.maximum(jnp.max(s_w, axis=-1, keepdims=True), s_wn)
        p_w = jnp.where(ok_w, jnp.exp(s_w - m_w), 0.0)
        p_wn = jnp.exp(s_wn - m_w)
        l_w = jnp.sum(p_w, axis=-1, keepdims=True) + p_wn
        p_w = p_w / l_w
        p_wn = p_wn / l_w
        o_w = _diag_heads(_dot_nt(p_w.astype(BF16), w[K_COLS:, :])
                          + p_wn.astype(BF16).astype(F32) * wn[:, K_COLS:].astype(BF16).astype(F32))
        g3 = g3_ref[...]
        o_ref[...] = oc_ref[...] + g3[:, 1:2] * o_s + g3[:, 2:3] * o_w


def _sel_win_sample(cache_t, page_base, page_table, qbd, sel_groups, sel_new, kvs_new, win_t, win_base,
                    kvw_new, oc, g3, past):
    bs, n_pages = page_table.shape
    pg = cache_t.shape[2]
    n_pg = _decode_pages_per_step(n_pages)
    bps = n_pg * pg // L_SEL
    w_buf = win_t.shape[2]

    def page_map(b, j, pt, k):
        return (page_base + pt[b, j * n_pg + k], 0, 0)

    per_b = lambda *shape: pl.BlockSpec((None,) + shape, lambda b, j, pt: (b,) + (0,) * len(shape))
    grid_spec = pltpu.PrefetchScalarGridSpec(
        num_scalar_prefetch=1,
        grid=(bs, n_pages // n_pg),
        in_specs=([pl.BlockSpec((None, KV_COLS, pg), functools.partial(page_map, k=k)) for k in range(n_pg)]
                  + [per_b(ATT_HEADS, K_COLS),
                     pl.BlockSpec((None, None, ATT_HEADS, bps), lambda b, j, pt: (b, j, 0, 0)),
                     per_b(ATT_HEADS, LANES), per_b(1, KV_COLS),
                     pl.BlockSpec((None, KV_COLS, w_buf), lambda b, j, pt: (win_base + b, 0, 0)),
                     per_b(1, KV_COLS), per_b(ATT_HEADS, HEAD_DIM), per_b(ATT_HEADS, LANES)]),
        out_specs=per_b(ATT_HEADS, HEAD_DIM),
        scratch_shapes=[pltpu.VMEM((ATT_HEADS, 1), F32), pltpu.VMEM((ATT_HEADS, 1), F32),
                        pltpu.VMEM((ATT_HEADS, K_COLS), F32), pltpu.VMEM((KV_COLS, n_pg * pg), BF16)],
    )
    return pl.pallas_call(
        functools.partial(_sel_win_sample_kernel, past, n_pg),
        grid_spec=grid_spec,
        out_shape=jax.ShapeDtypeStruct((bs, ATT_HEADS, HEAD_DIM), F32),
        compiler_params=_cparams(("parallel", "arbitrary")),
        name="sel_win_sample",
    )(page_table, *([cache_t] * n_pg), qbd, sel_groups, sel_new, kvs_new, win_t, kvw_new, oc, g3)


def _column_layout(d_model):
    ssm_dim = d_model // 2
    conv_ch = ssm_dim + 2 * SSM_GROUPS * SSM_STATE
    nsa = d_model - ssm_dim
    n_heads = ssm_dim // SSM_HEAD_DIM
    sizes = (ssm_dim, conv_ch, n_heads, nsa, KV_COLS, KV_COLS, KV_COLS, 3 * ATT_HEADS)
    offs = [0]
    for s in sizes:
        offs.append(offs[-1] + s)
    src = dict(zip(("z", "xbc", "dt", "q", "kvc", "kvs", "kvw", "gt"), zip(offs[:-1], sizes)))
    order = ("xbc", "z", "q", "kvc", "kvs", "kvw")
    col, start = {}, {}
    pos = 0
    for name in order:
        o, w = src[name]
        assert pos % w == 0
        col[name] = pos // w
        start[name] = pos
        pos += w
    col["small"] = pos // LANES
    start["small"] = pos
    return src, order, col, start, pos + LANES


def _dense_weights(p, d_model):
    src, order, _, _, _ = _column_layout(d_model)
    w_in = p["w_in"]
    n_heads = src["dt"][1]
    small = jnp.concatenate([w_in[:, :, src["dt"][0]:src["dt"][0] + n_heads],
                             w_in[:, :, src["gt"][0]:src["gt"][0] + src["gt"][1]]], axis=2)
    small = jnp.pad(small, ((0, 0), (0, 0), (0, LANES - small.shape[2])))
    w_in_packed = jnp.concatenate([w_in[:, :, src[n][0]:src[n][0] + src[n][1]] for n in order] + [small],
                                  axis=2).astype(BF16)
    return {"w_in": w_in_packed, "w_out": p["w_out"].astype(BF16), "w_gu": p["w_gu"].astype(BF16),
            "w_down": p["w_down"].astype(BF16)}


def _layer_weights(l, p, d_model):
    src, order, col, start, n_cols = _column_layout(d_model)
    ssm_dim = src["z"][1]
    rep = lambda v: jnp.repeat(v, SSM_HEAD_DIM)[None, :]
    pad_l = lambda v: jnp.pad(v, (0, LANES - v.shape[0]))[None, :]
    tile2 = lambda v: jnp.tile(v, 2)[None, :]
    cw = jnp.transpose(p["cmp_w"][l], (1, 0, 2, 3))
    zero = jnp.zeros_like(cw)
    pairs = jnp.concatenate([jnp.concatenate([cw, zero], axis=-1),
                             jnp.concatenate([zero, cw], axis=-1)], axis=-2)
    halves = jnp.transpose(pairs, (1, 0, 2, 3)).reshape(2, 2, CMP_STRIDE * LANES, LANES)
    w_cat = jnp.concatenate([halves[:, 0], halves[:, 1]], axis=-1).astype(BF16)
    pe = p["cmp_pe"][l]
    pe_rows = jnp.broadcast_to(pe[:, :, None, :], (L_CMP, 2, KV_HEADS, HEAD_DIM)).reshape(L_CMP, KV_COLS)
    head_expand = (jnp.arange(LANES)[:, None] == jnp.arange(ssm_dim)[None, :] // SSM_HEAD_DIM).astype(BF16)
    return {
        "col": col, "start": start,
        "norm1_g": p["norm1_g"][l][None, :],
        "conv_w": p["conv_w"][l], "conv_b": p["conv_b"][l][None, :],
        "dt_bias": pad_l(p["dt_bias"][l]), "a_log": pad_l(p["a_log"][l]),
        "dt_bias_e": rep(p["dt_bias"][l]), "a_log_e": rep(p["a_log"][l]),
        "dsk": rep(p["d_skip"][l]), "ssm_norm_g": p["ssm_norm_g"][l][None, :],
        "head_expand": head_expand,
        "g_q": tile2(p["q_norm_g"][l]), "g_kc": tile2(p["k_norm_g"][l, 0]),
        "g_ks": tile2(p["k_norm_g"][l, 1]), "g_kw": tile2(p["k_norm_g"][l, 2]),
        "cmp_pe_rows": pe_rows, "cmp_w_cat": w_cat,
        "norm2_g": p["norm2_g"][l][None, :],
    }


def _decode_pages_per_step(n_pages):
    return _pick(n_pages, (2 * PAGES_PER_STEP, PAGES_PER_STEP))


def _pick(n, prefs):
    for c in prefs:
        if n % c == 0:
            return c
    return n


def _dense_tail(x, y_ssd, atts, lw, big, layer, tm):
    d = x.shape[1]
    h = _out_proj(x, y_ssd, atts, big["w_out"], layer, tm, _pick(d, (512,)))
    d_ff = big["w_down"].shape[1]
    act = _ffn_up(h, lw["norm2_g"], big["w_gu"], layer, tm, _pick(d_ff, (512, 256, 128)))
    return _ffn_down(act, big["w_down"], layer, h, tm, _pick(d, (512,)))


def kernel(x_prompt, x_sample, cache_cmp_kv, cache_slc_kv, state_win_kv, state_ssm, state_conv, page_table,
           norm1_g, w_in, conv_w, conv_b, dt_bias, a_log, d_skip, ssm_norm_g, q_norm_g, k_norm_g,
           cmp_pe, cmp_w, w_out, norm2_g, w_gu, w_down):
    params = dict(norm1_g=norm1_g, w_in=w_in, conv_w=conv_w, conv_b=conv_b, dt_bias=dt_bias, a_log=a_log,
                  d_skip=d_skip, ssm_norm_g=ssm_norm_g, q_norm_g=q_norm_g, k_norm_g=k_norm_g, cmp_pe=cmp_pe,
                  cmp_w=cmp_w, w_out=w_out, norm2_g=norm2_g, w_gu=w_gu, w_down=w_down)
    bp, t, d = x_prompt.shape
    bs, dec_t, _ = x_sample.shape
    depth = w_in.shape[0]
    n_pool, pg = cache_cmp_kv.shape[1:3]
    n_pages = page_table.shape[1]
    past = n_pages * pg
    w_buf = state_win_kv.shape[2]
    assert bp == 1 and dec_t == 1
    assert t % (PAGES_PER_STEP * pg) == 0 and past % (PAGES_PER_STEP * pg) == 0 and t // L_SEL <= LANES
    ssm_dim = d // 2
    n_heads = ssm_dim // SSM_HEAD_DIM
    kv_shape = (2, KV_HEADS, HEAD_DIM)

    xp = x_prompt[0]
    xs = x_sample[:, 0]
    feature_major = lambda a: jnp.transpose(a, (0, 1, 3, 4, 5, 2)).reshape(a.shape[0] * a.shape[1], KV_COLS,
                                                                           a.shape[2])
    cmp_pages_t = feature_major(cache_cmp_kv)
    slc_pages_t = feature_major(cache_slc_kv)
    win_t = feature_major(state_win_kv)
    cmp_rows = _pick(t, (2 * PAGES_PER_STEP * pg, PAGES_PER_STEP * pg))
    outs = {k: [] for k in ("cmp_p", "cmp_s", "slc_p", "slc_s", "win_p", "win_s", "ssm_p", "ssm_s",
                            "conv_p", "conv_s")}
    tm_p = _pick(t, (512, 256, 128))
    big = _dense_weights(params, d)
    ssm_all = state_ssm.reshape(depth * bs, ssm_dim, SSM_STATE)
    tn_in = _pick(big["w_in"].shape[2], (1152, 640, 384, 128))
    for l in range(depth):
        lw = _layer_weights(l, params, d)
        col, start = lw["col"], lw["start"]

        u = _norm_matmul(xp, lw["norm1_g"], big["w_in"], l, tm_p, tn_in)
        qn, kvs_n, kvw_n, k_s, k_w, gates_h, q_t, vt_s, vt_w = _prep(u, col, lw["g_q"], lw["g_ks"], lw["g_kw"],
                                                                     _pick(t, (256, 128)), True)
        y_ssd, h_fin = _ssd_prompt(u, col, lw)
        kvc = u[:, start["kvc"]:start["kvc"] + KV_COLS]
        xbc_tail = u[t - (CONV_K - 1):, start["xbc"]:start["xbc"] + lw["conv_w"].shape[1]]
        ck, cv = _compress_prompt(jnp.pad(kvc, ((0, CMP_STRIDE), (0, 0))), lw, cmp_rows)
        o_c, sel_t = _cmp_prompt(qn, ck, cv, gates_h, t)
        o_s = _sel_prompt(q_t, k_s, vt_s, sel_t, gates_h, t)
        o_w = _win_prompt(q_t, k_w, vt_w, gates_h, t)
        xp = _dense_tail(xp, y_ssd, [o_c, o_s, o_w], lw, big, l, tm_p)
        outs["cmp_p"].append(kvc.reshape((1, t) + kv_shape))
        outs["slc_p"].append(kvs_n.reshape((1, t) + kv_shape))
        outs["win_p"].append(kvw_n[t - min(WINDOW, t):].reshape((1, min(WINDOW, t)) + kv_shape))
        outs["ssm_p"].append(h_fin.reshape(1, n_heads, SSM_HEAD_DIM, SSM_STATE))
        outs["conv_p"].append(xbc_tail[None])

        us = _norm_matmul(xs, lw["norm1_g"], big["w_in"], l, bs, tn_in)
        qn_s, kvs_s, kvw_s, _, _, gates_s = _prep(us, col, lw["g_q"], lw["g_ks"], lw["g_kw"], bs, False)
        conv_t = jnp.transpose(state_conv[l], (1, 0, 2))
        y_ssd_s, h_new = _ssd_sample(us, col, conv_t, ssm_all, l * bs, lw)
        xbc_s = us[:, start["xbc"]:start["xbc"] + lw["conv_w"].shape[1]]
        kvc_s = us[:, start["kvc"]:start["kvc"] + KV_COLS]
        ck_s, cv_s = _compress_sample(cmp_pages_t, l * n_pool, page_table, lw)
        q3 = qn_s.reshape(bs, ATT_HEADS, HEAD_DIM)
        g3 = jnp.transpose(gates_s[:, :, :3 * GQA].reshape(KV_HEADS, bs, GQA, 3), (1, 0, 2, 3))
        g3 = jnp.pad(g3.reshape(bs, ATT_HEADS, 3), ((0, 0), (0, 0), (0, LANES - 3)))
        oc_s, sel_s = _cmp_sample(q3, ck_s, cv_s, g3, past)
        sel16 = jnp.repeat(sel_s[:, :KV_HEADS], GQA, axis=1)
        bps = _decode_pages_per_step(n_pages) * pg // L_SEL
        sel_groups = jnp.transpose(sel16[:, :, :past // L_SEL].reshape(bs, ATT_HEADS, past // L_SEL // bps, bps),
                                   (0, 2, 1, 3))
        sel_new = jnp.pad(sel16[:, :, past // L_SEL:past // L_SEL + 1], ((0, 0), (0, 0), (0, LANES - 1)))
        head_of_col = jnp.arange(K_COLS) // HEAD_DIM
        qbd = jnp.where(head_of_col[None, None, :] == (jnp.arange(ATT_HEADS) // GQA)[None, :, None],
                        jnp.tile(q3, (1, 1, KV_HEADS)), jnp.zeros((), BF16))
        o_att_s = _sel_win_sample(slc_pages_t, l * n_pool, page_table, qbd, sel_groups, sel_new,
                                  kvs_s[:, None, :], win_t, l * bs, kvw_s[:, None, :], oc_s, g3, past)
        xs = _dense_tail(xs, y_ssd_s.astype(BF16), [o_att_s.reshape(bs, ATT_HEADS * HEAD_DIM)], lw, big, l, bs)
        outs["cmp_s"].append(kvc_s.reshape((bs, 1) + kv_shape))
        outs["slc_s"].append(kvs_s.reshape((bs, 1) + kv_shape))
        outs["win_s"].append(kvw_s.reshape((bs, 1) + kv_shape))
        outs["ssm_s"].append(h_new.reshape(bs, n_heads, SSM_HEAD_DIM, SSM_STATE))
        outs["conv_s"].append(jnp.concatenate([state_conv[l], xbc_s[:, None, :]], axis=1)[:, -(CONV_K - 1):])

    st = lambda k: jnp.stack(outs[k])
    win_s = jnp.concatenate([state_win_kv, st("win_s")], axis=2)[:, :, -w_buf:]
    return (xp[None], xs[:, None], st("cmp_p"), st("cmp_s"), st("slc_p"), st("slc_s"), st("win_p"),
            win_s, st("ssm_p"), st("ssm_s"), st("conv_p"), st("conv_s"))
```
